```python
import math
import jax, jax.numpy as jnp
from jax import lax
import numpy as np

D_MODEL = 2048
BATCH = 4
SEQ = 2048
DEPTH = 4
DEC_BATCH = 32
DEC_SEQ = 8
PAST_LEN = 16384
PAGE_SIZE = 128

N_MIXERS = 2
N_ATTN_LAYERS = (DEPTH + 1) // 2
N_REC_LAYERS = DEPTH // 2
HEAD_DIM = 64
N_Q_HEADS = D_MODEL // HEAD_DIM
N_KV_HEADS = N_Q_HEADS // 8
GQA_GROUP = N_Q_HEADS // N_KV_HEADS
WINDOW = 128
CACHE_WIN = min(WINDOW, PAST_LEN)
ROT_DIM = HEAD_DIM // 4
ROPE_THETA = 500000.0
EXPAND = 128
REC_HEADS = D_MODEL // EXPAND
REC_DK = EXPAND
REC_DV = D_MODEL // REC_HEADS
REC_CHUNK = 64
D_FF = 4 * D_MODEL
EPS = 1e-6

kernel_name = 'hybrid_swa_sink_hgrn2_decode_step'


def rms_norm(x, gain):
    xf = x.astype(jnp.float32)
    y = xf * lax.rsqrt(jnp.mean(xf * xf, axis=-1, keepdims=True) + EPS)
    return (y * gain.astype(jnp.float32)).astype(x.dtype)


def partial_rope(x, pos):
    half = ROT_DIM // 2
    inv = ROPE_THETA ** (-jnp.arange(half, dtype=jnp.float32) / half)
    ang = pos.astype(jnp.float32)[:, None] * inv[None, :]
    cos = jnp.cos(ang)[:, None, :]
    sin = jnp.sin(ang)[:, None, :]
    xf = x.astype(jnp.float32)
    x1 = xf[..., :half]
    x2 = xf[..., half:ROT_DIM]
    out = jnp.concatenate([x1 * cos - x2 * sin, x2 * cos + x1 * sin, xf[..., ROT_DIM:]], axis=-1)
    return out.astype(x.dtype)


def attn_qkv(h, w_qkv, q_gain, k_gain, pos):
    B, L, _ = h.shape
    qkv = h @ w_qkv
    q, k, v = jnp.split(qkv, [N_Q_HEADS * HEAD_DIM, (N_Q_HEADS + N_KV_HEADS) * HEAD_DIM], axis=-1)
    q = q.reshape(B, L, N_Q_HEADS, HEAD_DIM)
    k = k.reshape(B, L, N_KV_HEADS, HEAD_DIM)
    v = v.reshape(B, L, N_KV_HEADS, HEAD_DIM)
    q = partial_rope(rms_norm(q, q_gain), pos)
    k = partial_rope(rms_norm(k, k_gain), pos)
    return q, k, v


def sink_attend(q, k, v, q_pos, k_pos, sinks):
    B, N, Lq = q.shape[:3]
    qg = q.reshape(B, N, Lq, N_KV_HEADS, GQA_GROUP, HEAD_DIM)
    s = jnp.einsum('bnqhgd,bnshd->bnhgqs', qg, k, preferred_element_type=jnp.float32) * (HEAD_DIM ** -0.5)
    dist = q_pos[:, :, None] - k_pos[:, None, :]
    ok = (dist >= 0) & (dist < WINDOW) & (k_pos[:, None, :] >= 0)
    s = jnp.where(ok[None, :, None, None], s, -jnp.inf)
    sink = jnp.broadcast_to(sinks.astype(jnp.float32).reshape(1, 1, N_KV_HEADS, GQA_GROUP, 1, 1), s.shape[:-1] + (1,))
    p = jax.nn.softmax(jnp.concatenate([s, sink], axis=-1), axis=-1)[..., :-1]
    o = jnp.einsum('bnhgqs,bnshd->bnqhgd', p.astype(v.dtype), v)
    return o.reshape(B, N, Lq, N_Q_HEADS * HEAD_DIM)


def swa_prompt(h, w_qkv, q_gain, k_gain, sinks, w_o):
    B, L, _ = h.shape
    pos = jnp.arange(L, dtype=jnp.int32)
    q, k, v = attn_qkv(h, w_qkv, q_gain, k_gain, pos)
    nb = L // WINDOW
    def band(t):
        tp = jnp.pad(t, ((0, 0), (WINDOW, 0), (0, 0), (0, 0)))
        prev = tp[:, :L].reshape(B, nb, WINDOW, N_KV_HEADS, HEAD_DIM)
        cur = tp[:, WINDOW:].reshape(B, nb, WINDOW, N_KV_HEADS, HEAD_DIM)
        return jnp.concatenate([prev, cur], axis=2)
    kpos = jnp.arange(-WINDOW, L, dtype=jnp.int32)
    kpos_b = jnp.concatenate([kpos[:L].reshape(nb, WINDOW), kpos[WINDOW:].reshape(nb, WINDOW)], axis=1)
    qb = q.reshape(B, nb, WINDOW, N_Q_HEADS, HEAD_DIM)
    o = sink_attend(qb, band(k), band(v), pos.reshape(nb, WINDOW), kpos_b, sinks)
    y = o.reshape(B, L, N_Q_HEADS * HEAD_DIM) @ w_o
    return y, (k[:, L - CACHE_WIN:], v[:, L - CACHE_WIN:])


def swa_sample(h, cache_k, cache_v, w_qkv, q_gain, k_gain, sinks, w_o):
    B, L, _ = h.shape
    pos = PAST_LEN + jnp.arange(L, dtype=jnp.int32)
    q, k, v = attn_qkv(h, w_qkv, q_gain, k_gain, pos)
    kk = jnp.concatenate([cache_k.astype(k.dtype), k], axis=1)
    vv = jnp.concatenate([cache_v.astype(v.dtype), v], axis=1)
    kpos = jnp.arange(PAST_LEN - CACHE_WIN, PAST_LEN + L, dtype=jnp.int32)
    o = sink_attend(q[:, None], kk[:, None], vv[:, None], pos[None], kpos[None], sinks)
    y = o.reshape(B, L, N_Q_HEADS * HEAD_DIM) @ w_o
    return y, (kk[:, -CACHE_WIN:], vv[:, -CACHE_WIN:])


def hgrn2_scan(q, k, v, logf, S0, chunk):
    B, L, H, _ = q.shape
    nc = L // chunk
    def to_chunks(t):
        return t.reshape(B, nc, chunk, H, t.shape[-1]).transpose(1, 0, 3, 2, 4)
    tri = jnp.tril(jnp.ones((chunk, chunk), dtype=bool))
    def step(S, xs):
        qc, kc, vc, gc = xs
        G = jnp.cumsum(gc, axis=2)
        dec = jnp.where(tri[:, :, None], G[:, :, :, None, :] - G[:, :, None, :, :], -jnp.inf)
        A = jnp.einsum('bhtk,bhsk,bhtsk->bhts', qc, kc, jnp.exp(dec))
        o = jnp.einsum('bhtk,bhkv->bhtv', qc * jnp.exp(G), S) + jnp.einsum('bhts,bhsv->bhtv', A, vc)
        Gl = G[:, :, -1:, :]
        S = jnp.exp(Gl[:, :, 0, :, None]) * S + jnp.einsum('bhsk,bhsv->bhkv', kc * jnp.exp(Gl - G), vc)
        return S, o
    S, o = lax.scan(step, S0, (to_chunks(q), to_chunks(k), to_chunks(v), to_chunks(logf)))
    o = o.transpose(1, 0, 3, 2, 4).reshape(B, L, H, v.shape[-1])
    return o, S


def hgrn2_mixer(h, S0, w_in, lb, o_gain, w_o):
    B, L, _ = h.shape
    f32 = jnp.float32
    qk = REC_HEADS * REC_DK
    dv = REC_HEADS * REC_DV
    proj = h @ w_in
    q, fz, i, gz = jnp.split(proj, [qk, 2 * qk, 2 * qk + dv], axis=-1)
    fz = fz.astype(f32)
    lb = lb.astype(f32)
    f = lb + (1.0 - lb) * jax.nn.sigmoid(fz)
    logf = jnp.log(f).reshape(B, L, REC_HEADS, REC_DK)
    k = ((1.0 - lb) * jax.nn.sigmoid(-fz)).reshape(B, L, REC_HEADS, REC_DK)
    q = jax.nn.silu(q.astype(f32)).reshape(B, L, REC_HEADS, REC_DK)
    v = i.astype(f32).reshape(B, L, REC_HEADS, REC_DV)
    chunk = math.gcd(L, REC_CHUNK)
    o, S = hgrn2_scan(q, k, v, logf, S0.astype(f32), chunk)
    o = rms_norm(o, o_gain.reshape(REC_HEADS, REC_DV)).reshape(B, L, dv).astype(h.dtype)
    y = (o * jax.nn.silu(gz)) @ w_o
    return y, (S.astype(h.dtype),)


def lower_bounds(logits):
    cum = jnp.cumsum(jax.nn.softmax(logits.astype(jnp.float32), axis=0), axis=0)
    return cum - cum[0:1]


def block(x, c, mix_fn, gain_mix, gain_mlp, w_ada, b_ada, w_up, w_down):
    mod = (jax.nn.silu(c) @ w_ada + b_ada)[:, None, :]
    sh1, sc1, g1, sh2, sc2, g2 = jnp.split(mod, 6, axis=-1)
    mixed, new_state = mix_fn(rms_norm(x, gain_mix) * (1 + sc1) + sh1)
    x = x + g1 * mixed
    h = rms_norm(x, gain_mlp) * (1 + sc2) + sh2
    x = x + g2 * (jnp.square(jax.nn.relu(h @ w_up)) @ w_down)
    return x, new_state


def setup_inputs(seed: int = 0) -> dict:
    key = jax.random.key(seed)
    ks = jax.random.split(key, 24)
    def nrm(k, shape, s=1.0):
        return jax.random.normal(k, shape, jnp.float32) * s
    D = D_MODEL
    qkv_w = (N_Q_HEADS + 2 * N_KV_HEADS) * HEAD_DIM
    rec_in = 2 * REC_HEADS * REC_DK + 2 * REC_HEADS * REC_DV
    return {
        'x_prompt': nrm(ks[0], (BATCH, SEQ, D)),
        'x_sample': nrm(ks[1], (DEC_BATCH, DEC_SEQ, D)),
        'cache_k_win': nrm(ks[2], (N_ATTN_LAYERS, DEC_BATCH, CACHE_WIN, N_KV_HEADS, HEAD_DIM)),
        'cache_v_win': nrm(ks[3], (N_ATTN_LAYERS, DEC_BATCH, CACHE_WIN, N_KV_HEADS, HEAD_DIM)),
        'state_hgrn': nrm(ks[4], (N_REC_LAYERS, DEC_BATCH, REC_HEADS, REC_DK, REC_DV), 0.5),
        'c_prompt': nrm(ks[5], (BATCH, D)),
        'c_sample': nrm(ks[6], (DEC_BATCH, D)),
        'norm_gain': 1.0 + nrm(ks[7], (DEPTH, 2, D), 0.02),
        'w_ada': nrm(ks[8], (DEPTH, D, 6 * D), 0.5 * D ** -0.5),
        'b_ada': nrm(ks[9], (DEPTH, 6 * D), 0.01),
        'attn_w_qkv': nrm(ks[10], (N_ATTN_LAYERS, D, qkv_w), D ** -0.5),
        'attn_q_gain': 1.0 + nrm(ks[11], (N_ATTN_LAYERS, HEAD_DIM), 0.02),
        'attn_k_gain': 1.0 + nrm(ks[12], (N_ATTN_LAYERS, HEAD_DIM), 0.02),
        'attn_sinks': nrm(ks[13], (N_ATTN_LAYERS, N_Q_HEADS), 0.5),
        'attn_w_o': nrm(ks[14], (N_ATTN_LAYERS, N_Q_HEADS * HEAD_DIM, D), (N_Q_HEADS * HEAD_DIM) ** -0.5),
        'rec_w_in': nrm(ks[15], (N_REC_LAYERS, D, rec_in), D ** -0.5),
        'rec_lb_logits': nrm(ks[16], (N_REC_LAYERS, REC_HEADS * REC_DK), 0.5),
        'rec_o_gain': 1.0 + nrm(ks[17], (N_REC_LAYERS, REC_HEADS * REC_DV), 0.02),
        'rec_w_o': nrm(ks[18], (N_REC_LAYERS, REC_HEADS * REC_DV, D), (REC_HEADS * REC_DV) ** -0.5),
        'mlp_w_up': nrm(ks[19], (DEPTH, D, D_FF), D ** -0.5),
        'mlp_w_down': nrm(ks[20], (DEPTH, D_FF, D), D_FF ** -0.5),
    }


def reference(x_prompt, x_sample, cache_k_win, cache_v_win, state_hgrn, c_prompt, c_sample,
              norm_gain, w_ada, b_ada, attn_w_qkv, attn_q_gain, attn_k_gain, attn_sinks, attn_w_o,
              rec_w_in, rec_lb_logits, rec_o_gain, rec_w_o, mlp_w_up, mlp_w_down):
    lbs = lower_bounds(rec_lb_logits)
    xp, xs = x_prompt, x_sample
    kwp, vwp, kws, vws, sp, ss = [], [], [], [], [], []
    for i in range(DEPTH):
        j = i // N_MIXERS
        common = (norm_gain[i, 0], norm_gain[i, 1], w_ada[i], b_ada[i], mlp_w_up[i], mlp_w_down[i])
        if i % N_MIXERS == 0:
            aw = (attn_w_qkv[j], attn_q_gain[j], attn_k_gain[j], attn_sinks[j], attn_w_o[j])
            xp, (kp, vp) = block(xp, c_prompt, lambda h: swa_prompt(h, *aw), *common)
            xs, (kn, vn) = block(xs, c_sample, lambda h: swa_sample(h, cache_k_win[j], cache_v_win[j], *aw), *common)
            kwp.append(kp); vwp.append(vp); kws.append(kn); vws.append(vn)
        else:
            rw = (rec_w_in[j], lbs[j], rec_o_gain[j], rec_w_o[j])
            S0p = jnp.zeros((xp.shape[0], REC_HEADS, REC_DK, REC_DV), xp.dtype)
            xp, (Sp,) = block(xp, c_prompt, lambda h: hgrn2_mixer(h, S0p, *rw), *common)
            xs, (Ss,) = block(xs, c_sample, lambda h: hgrn2_mixer(h, state_hgrn[j], *rw), *common)
            sp.append(Sp); ss.append(Ss)
    k_win_prompt = jnp.stack(kwp)
    v_win_prompt = jnp.stack(vwp)
    k_win_sample = jnp.stack(kws)
    v_win_sample = jnp.stack(vws)
    hgrn_state_prompt = jnp.stack(sp)
    hgrn_state_sample = jnp.stack(ss)
    return (xp, xs, k_win_prompt, v_win_prompt, k_win_sample, v_win_sample, hgrn_state_prompt, hgrn_state_sample)
```

```python
import functools

import numpy as np
import jax
import jax.numpy as jnp
from jax import lax
from jax.experimental import pallas as pl
from jax.experimental.pallas import tpu as pltpu

F32 = jnp.float32
BF16 = jnp.bfloat16

D_MODEL = 2048
DEPTH = 4
N_MIXERS = 2
PAST_LEN = 16384
HEAD_DIM = 64
N_Q_HEADS = D_MODEL // HEAD_DIM
N_KV_HEADS = N_Q_HEADS // 8
GQA_GROUP = N_Q_HEADS // N_KV_HEADS
KV_WIDTH = N_KV_HEADS * HEAD_DIM
WINDOW = 128
ROT_DIM = HEAD_DIM // 4
ROPE_THETA = 500000.0
REC_HEADS = 16
REC_DK = 128
REC_DV = D_MODEL // REC_HEADS
D_FF = 4 * D_MODEL
EPS = 1e-6

LANES = 128
MIB = 1024 * 1024
SCAN_ROWS = 128
SCAN_HEADS = 4


def _compiler_params(semantics, vmem_mib):
    return pltpu.CompilerParams(dimension_semantics=semantics, vmem_limit_bytes=vmem_mib * MIB)


def _sigmoid(x):
    e = jnp.exp(-jnp.abs(x))
    r = 1.0 / (1.0 + e)
    return jnp.where(x >= 0, r, e * r)


def _silu(x):
    return x * _sigmoid(x)


def _modnorm(x, gain, scale, shift):
    var = jnp.mean(x * x, axis=-1, keepdims=True)
    y = x * lax.rsqrt(var + EPS) * gain
    return y * (1.0 + scale) + shift


def _split_bf16(x, parts):
    out = []
    for _ in range(parts - 1):
        hi = x.astype(BF16)
        out.append(hi)
        x = x - hi.astype(F32)
    out.append(x.astype(BF16))
    return out


class _Mod:
    def __init__(self, mod, rows_per_group, per_row):
        self.rows_per_group = rows_per_group
        self.per_row = per_row
        parts = jnp.split(mod, 6, axis=-1)
        if per_row:
            self.parts = [jnp.repeat(p, rows_per_group, axis=0)[None] for p in parts]
        else:
            self.parts = [p[:, None, :] for p in parts]

    def spec(self, tm):
        if self.per_row:
            return pl.BlockSpec((1, tm, D_MODEL), lambda i, *_: (0, i, 0))
        assert self.rows_per_group % tm == 0
        per = self.rows_per_group // tm
        return pl.BlockSpec((1, 1, D_MODEL), lambda i, *_: (i // per, 0, 0))


def _ada_kernel(c_ref, w_ref, b_ref, o_ref):
    s = _silu(c_ref[...]).astype(BF16)
    o_ref[0] = jnp.dot(s, w_ref[0].astype(BF16), preferred_element_type=F32) + b_ref[0]


def _ada_mods(c_all, w_ada, b_ada):
    rows = c_all.shape[0]
    tn = 1024
    n = 6 * D_MODEL
    return pl.pallas_call(
        _ada_kernel,
        grid=(DEPTH, n // tn),
        in_specs=[
            pl.BlockSpec((rows, D_MODEL), lambda l, j: (0, 0)),
            pl.BlockSpec((1, D_MODEL, tn), lambda l, j: (l, 0, j)),
            pl.BlockSpec((1, 1, tn), lambda l, j: (l, 0, j)),
        ],
        out_specs=pl.BlockSpec((1, rows, tn), lambda l, j: (l, 0, j)),
        out_shape=jax.ShapeDtypeStruct((DEPTH, rows, n), F32),
        compiler_params=_compiler_params(("parallel", "parallel"), 40),
        name="ada_mods",
    )(c_all, w_ada, b_ada.reshape(DEPTH, 1, n))


def _norm_matmul_kernel(x_ref, gain_ref, sc_ref, sh_ref, w_ref, o_ref, h_scr):
    @pl.when(pl.program_id(1) == 0)
    def _():
        h_scr[...] = _modnorm(x_ref[...], gain_ref[...], sc_ref[0], sh_ref[0]).astype(BF16)

    o_ref[...] = jnp.dot(h_scr[...], w_ref[...], preferred_element_type=F32)


def _norm_matmul(x, gain, mod, w, tm, tn):
    rows = x.shape[0]
    n = w.shape[1]
    return pl.pallas_call(
        _norm_matmul_kernel,
        grid=(rows // tm, n // tn),
        in_specs=[
            pl.BlockSpec((tm, D_MODEL), lambda i, j: (i, 0)),
            pl.BlockSpec((1, D_MODEL), lambda i, j: (0, 0)),
            mod.spec(tm),
            mod.spec(tm),
            pl.BlockSpec((D_MODEL, tn), lambda i, j: (0, j)),
        ],
        out_specs=pl.BlockSpec((tm, tn), lambda i, j: (i, j)),
        out_shape=jax.ShapeDtypeStruct((rows, n), F32),
        scratch_shapes=[pltpu.VMEM((tm, D_MODEL), BF16)],
        compiler_params=_compiler_params(("parallel", "arbitrary"), 48),
        name="norm_matmul",
    )(x, gain, mod.parts[1], mod.parts[0], w)


def _qkv_kernel(x_ref, gain_ref, sc_ref, sh_ref, w_ref, qg_ref, kg_ref, cos_ref, s1_ref, s2_ref, bd_ref,
                q_ref, k_ref, v_ref):
    h = _modnorm(x_ref[...], gain_ref[...], sc_ref[0], sh_ref[0]).astype(BF16)
    acc = jnp.dot(h, w_ref[...], preferred_element_type=F32)
    cos, s1, s2 = cos_ref[...], s1_ref[...], s2_ref[...]
    bd = bd_ref[...]
    qk_width = D_MODEL + KV_WIDTH
    for c in range(qk_width // LANES):
        xc = acc[:, c * LANES:(c + 1) * LANES]
        hi, lo = _split_bf16(xc * xc, 2)
        ssq = (jnp.dot(hi, bd, preferred_element_type=F32) + jnp.dot(lo, bd, preferred_element_type=F32))
        is_q = c * LANES < D_MODEL
        y = xc * lax.rsqrt(ssq * (1.0 / HEAD_DIM) + EPS) * (qg_ref[...] if is_q else kg_ref[...])
        r = y * cos + pltpu.roll(y, LANES - ROT_DIM // 2, 1) * s1 + pltpu.roll(y, ROT_DIM // 2, 1) * s2
        if is_q:
            q_ref[:, c * LANES:(c + 1) * LANES] = (r * (HEAD_DIM ** -0.5)).astype(q_ref.dtype)
        else:
            k_ref[:, c * LANES - D_MODEL:(c + 1) * LANES - D_MODEL] = r
    v_ref[...] = acc[:, qk_width:]


def _rope_tables(pos):
    half = ROT_DIM // 2
    inv = ROPE_THETA ** (-jnp.arange(half, dtype=F32) / half)
    ang = pos.astype(F32)[:, None] * inv[None, :]
    cos, sin = jnp.cos(ang), jnp.sin(ang)
    n = pos.shape[0]
    pad = jnp.zeros((n, HEAD_DIM - ROT_DIM), F32)
    zero = jnp.zeros((n, half), F32)
    c = jnp.concatenate([cos, cos, pad + 1.0], axis=1)
    s1 = jnp.concatenate([-sin, zero, pad], axis=1)
    s2 = jnp.concatenate([zero, sin, pad], axis=1)
    return tuple(jnp.tile(t, (1, LANES // HEAD_DIM)) for t in (c, s1, s2))


def _head_sum_matrix():
    idx = np.arange(LANES) // HEAD_DIM
    return jnp.asarray(idx[:, None] == idx[None, :], dtype=BF16)


def _qkv(x, gain, mod, w, q_gain, k_gain, tables, table_rows, tm, q_dtype):
    rows = x.shape[0]
    n = w.shape[1]
    per = table_rows // tm
    tspec = pl.BlockSpec((tm, LANES), lambda i: (i % per, 0))
    vspec = pl.BlockSpec((1, LANES), lambda i: (0, 0))
    reps = LANES // HEAD_DIM
    return pl.pallas_call(
        _qkv_kernel,
        grid=(rows // tm,),
        in_specs=[
            pl.BlockSpec((tm, D_MODEL), lambda i: (i, 0)),
            pl.BlockSpec((1, D_MODEL), lambda i: (0, 0)),
            mod.spec(tm),
            mod.spec(tm),
            pl.BlockSpec((D_MODEL, n), lambda i: (0, 0)),
            vspec, vspec, tspec, tspec, tspec,
            pl.BlockSpec((LANES, LANES), lambda i: (0, 0)),
        ],
        out_specs=[
            pl.BlockSpec((tm, D_MODEL), lambda i: (i, 0)),
            pl.BlockSpec((tm, KV_WIDTH), lambda i: (i, 0)),
            pl.BlockSpec((tm, KV_WIDTH), lambda i: (i, 0)),
        ],
        out_shape=[
            jax.ShapeDtypeStruct((rows, D_MODEL), q_dtype),
            jax.ShapeDtypeStruct((rows, KV_WIDTH), F32),
            jax.ShapeDtypeStruct((rows, KV_WIDTH), F32),
        ],
        compiler_params=_compiler_params(("parallel",), 48),
        name="qkv",
    )(x, gain, mod.parts[1], mod.parts[0], w,
      jnp.tile(q_gain.reshape(1, HEAD_DIM), (1, reps)), jnp.tile(k_gain.reshape(1, HEAD_DIM), (1, reps)),
      *tables, _head_sum_matrix())


def _softmax_with_sink(s, sink):
    m = jnp.maximum(jnp.max(s, axis=-1, keepdims=True), sink)
    p = jnp.exp(s - m)
    denom = jnp.sum(p, axis=-1, keepdims=True) + jnp.exp(sink - m)
    return p, denom


def _attn_prompt_kernel(sink_ref, q_ref, kp_ref, kc_ref, vp_ref, vc_ref, o_ref):
    j = pl.program_id(1)
    kk = jnp.concatenate([kp_ref[...], kc_ref[...]], axis=0).astype(BF16)
    vv = jnp.concatenate([vp_ref[...], vc_ref[...]], axis=0).astype(BF16)
    t = lax.broadcasted_iota(jnp.int32, (WINDOW, 2 * WINDOW), 0)
    s = lax.broadcasted_iota(jnp.int32, (WINDOW, 2 * WINDOW), 1)
    ok = (s > t) & (s <= t + WINDOW) & (s >= jnp.where(j > 0, 0, WINDOW))
    for g in range(N_KV_HEADS):
        kg = kk[:, g * HEAD_DIM:(g + 1) * HEAD_DIM]
        vg = vv[:, g * HEAD_DIM:(g + 1) * HEAD_DIM]
        for hp in range(GQA_GROUP // 2):
            outs = []
            for h in (g * GQA_GROUP + 2 * hp, g * GQA_GROUP + 2 * hp + 1):
                qh = q_ref[:, h * HEAD_DIM:(h + 1) * HEAD_DIM]
                sc = lax.dot_general(qh, kg, (((1,), (1,)), ((), ())), preferred_element_type=F32)
                p, denom = _softmax_with_sink(jnp.where(ok, sc, -jnp.inf), sink_ref[h])
                outs.append(jnp.dot(p.astype(BF16), vg, preferred_element_type=F32) / denom)
            lo = 2 * (g * GQA_GROUP // 2 + hp) * HEAD_DIM
            o_ref[:, lo:lo + 2 * HEAD_DIM] = jnp.concatenate(outs, axis=1).astype(o_ref.dtype)


def _attn_prompt(q, k, v, sinks, batch, seq):
    nb = seq // WINDOW
    rows = batch * seq
    cur = lambda b, j: (b * nb + j, 0)
    prev = lambda b, j: (b * nb + jnp.maximum(j - 1, 0), 0)
    kv_block = (WINDOW, KV_WIDTH)
    return pl.pallas_call(
        _attn_prompt_kernel,
        grid=(batch, nb),
        in_specs=[
            pl.BlockSpec(memory_space=pltpu.SMEM),
            pl.BlockSpec((WINDOW, D_MODEL), cur),
            pl.BlockSpec(kv_block, prev), pl.BlockSpec(kv_block, cur),
            pl.BlockSpec(kv_block, prev), pl.BlockSpec(kv_block, cur),
        ],
        out_specs=pl.BlockSpec((WINDOW, D_MODEL), cur),
        out_shape=jax.ShapeDtypeStruct((rows, D_MODEL), BF16),
        compiler_params=_compiler_params(("parallel", "parallel"), 32),
        name="attn_prompt",
    )(sinks, q, k, k, v, v)


def _attn_sample_kernel(sink_ref, q_ref, k_ref, v_ref, ck_ref, cv_ref, o_ref, nk_ref, nv_ref, *, steps):
    kc, vc = ck_ref[0], cv_ref[0]
    kn, vn = k_ref[...], v_ref[...]
    nk_ref[0, :WINDOW - steps] = kc[steps:]
    nk_ref[0, WINDOW - steps:] = kn
    nv_ref[0, :WINDOW - steps] = vc[steps:]
    nv_ref[0, WINDOW - steps:] = vn
    zpad = jnp.zeros((WINDOW - steps, KV_WIDTH), F32)
    kk = jnp.concatenate([kc, kn, zpad], axis=0).astype(BF16)
    vv = jnp.concatenate([vc, vn, zpad], axis=0).astype(BF16)
    rows = GQA_GROUP * steps
    assert steps & (steps - 1) == 0
    t = lax.broadcasted_iota(jnp.int32, (rows, 2 * WINDOW), 0) & (steps - 1)
    s = lax.broadcasted_iota(jnp.int32, (rows, 2 * WINDOW), 1)
    ok = ((s < WINDOW) & (s > t)) | ((s >= WINDOW) & (s - WINDOW <= t))
    q = q_ref[...]
    for g in range(N_KV_HEADS):
        heads = range(g * GQA_GROUP, (g + 1) * GQA_GROUP)
        qg = jnp.concatenate([q[:, h * HEAD_DIM:(h + 1) * HEAD_DIM] for h in heads], axis=0).astype(BF16)
        sc = lax.dot_general(qg, kk[:, g * HEAD_DIM:(g + 1) * HEAD_DIM], (((1,), (1,)), ((), ())),
                             preferred_element_type=F32)
        p, denom = _softmax_with_sink(jnp.where(ok, sc, -jnp.inf), sink_ref[g])
        og = jnp.dot(p.astype(BF16), vv[:, g * HEAD_DIM:(g + 1) * HEAD_DIM], preferred_element_type=F32) / denom
        for i, h in enumerate(heads):
            o_ref[:, h * HEAD_DIM:(h + 1) * HEAD_DIM] = og[i * steps:(i + 1) * steps]


def _attn_sample(q, k, v, cache_k, cache_v, sinks, batch, steps):
    rows = batch * steps
    sink_rows = jnp.repeat(sinks.reshape(N_KV_HEADS, GQA_GROUP), steps, axis=1)[..., None]
    row = lambda b: (b, 0)
    win = lambda b: (b, 0, 0)
    cache_block = (1, WINDOW, KV_WIDTH)
    return pl.pallas_call(
        functools.partial(_attn_sample_kernel, steps=steps),
        grid=(batch,),
        in_specs=[
            pl.BlockSpec((N_KV_HEADS, GQA_GROUP * steps, 1), lambda b: (0, 0, 0)),
            pl.BlockSpec((steps, D_MODEL), row),
            pl.BlockSpec((steps, KV_WIDTH), row), pl.BlockSpec((steps, KV_WIDTH), row),
            pl.BlockSpec(cache_block, win), pl.BlockSpec(cache_block, win),
        ],
        out_specs=[
            pl.BlockSpec((steps, D_MODEL), row),
            pl.BlockSpec(cache_block, win), pl.BlockSpec(cache_block, win),
        ],
        out_shape=[
            jax.ShapeDtypeStruct((rows, D_MODEL), F32),
            jax.ShapeDtypeStruct((batch, WINDOW, KV_WIDTH), F32),
            jax.ShapeDtypeStruct((batch, WINDOW, KV_WIDTH), F32),
        ],
        compiler_params=_compiler_params(("parallel",), 32),
        name="attn_sample",
    )(sink_rows, q, k, v, cache_k, cache_v)


def _scan_matrix(seq_rows, levels):
    n = SCAN_ROWS
    t = np.arange(n)[:, None]
    r = np.arange(n)[None, :]
    same = (t // seq_rows) == (r // seq_rows)
    mats = [same & (r <= t), same & (r > t)]
    for b in levels:
        mid = (t // b) * b + b // 2
        upper = (t & (b // 2)) != 0
        mats.append(np.where(upper, (r >= mid) & (r <= t), (r > t) & (r < mid)))
    return jnp.asarray(np.concatenate(mats, axis=0), dtype=BF16)


def _scan_kernel(q_ref, f_ref, i_ref, g_ref, lbl_ref, og_ref, mat_ref, s0_ref, a_ref, st_ref, *,
                 layer, n_seq, levels):
    n = SCAN_ROWS
    seq_rows = n // n_seq
    chunk = pl.program_id(2)

    @pl.when(chunk == 0)
    def _():
        for s in range(n_seq):
            for h in range(SCAN_HEADS):
                st_ref[s, h] = s0_ref[s, h].T

    logits = lbl_ref[...]
    pexp = jnp.exp(logits - jnp.max(logits, axis=0, keepdims=True))
    psum = jnp.sum(pexp, axis=0, keepdims=True)
    lb = jnp.zeros_like(psum)
    for i in range(1, layer + 1):
        lb = lb + pexp[i:i + 1] / psum

    qs = _silu(q_ref[...])
    fz = f_ref[...]
    e = jnp.exp(-jnp.abs(fz))
    r = 1.0 / (1.0 + e)
    pos = fz >= 0
    logf = jnp.log(lb + (1.0 - lb) * jnp.where(pos, r, e * r))
    kin = (1.0 - lb) * jnp.where(pos, e * r, r)
    val = i_ref[...]
    gate = _silu(g_ref[...])

    mat = mat_ref[...]
    sums = None
    for part in _split_bf16(logf, 2):
        d = jnp.dot(mat, part, preferred_element_type=F32)
        sums = d if sums is None else sums + d
    gcum = sums[:n]
    gtail = sums[n:2 * n]

    t_lane = lax.broadcasted_iota(jnp.int32, (n, LANES), 0)
    tt = lax.broadcasted_iota(jnp.int32, (n, n), 0)
    ss = lax.broadcasted_iota(jnp.int32, (n, n), 1)
    contract_lanes = (((1,), (1,)), ((), ()))
    contract_rows = (((0,), (0,)), ((), ()))

    for h in range(SCAN_HEADS):
        sl = slice(h * LANES, (h + 1) * LANES)
        qh, kh, vh = qs[:, sl], kin[:, sl], val[:, sl]
        vb = vh.astype(BF16)
        a = jnp.zeros((n, n), F32)
        for l, b in enumerate(levels):
            decay = jnp.exp(sums[(l + 2) * n:(l + 3) * n, sl])
            upper = (t_lane & (b // 2)) != 0
            qt = jnp.where(upper, qh * decay, 0.0).astype(BF16)
            kt = jnp.where(upper, 0.0, kh * decay).astype(BF16)
            al = lax.dot_general(qt, kt, contract_lanes, preferred_element_type=F32)
            a = a + (al if b == n else jnp.where((tt ^ ss) < b, al, 0.0))
        o = jnp.dot(a.astype(BF16), vb, preferred_element_type=F32)
        o = o + jnp.sum(qh * kh, axis=-1, keepdims=True) * vh
        qe = qh * jnp.exp(gcum[:, sl])
        kt = kh * jnp.exp(gtail[:, sl])
        for s in range(n_seq):
            last = (s + 1) * seq_rows - 1
            if n_seq == 1:
                qe_s, kt_s = qe, kt
            else:
                inside = (t_lane >= s * seq_rows) & (t_lane <= last)
                qe_s, kt_s = jnp.where(inside, qe, 0.0), jnp.where(inside, kt, 0.0)
            st = st_ref[s, h]
            o = o + lax.dot_general(qe_s.astype(BF16), st.astype(BF16), contract_lanes,
                                    preferred_element_type=F32)
            st_ref[s, h] = st * jnp.exp(gcum[last:last + 1, sl]) + lax.dot_general(
                vb, kt_s.astype(BF16), contract_rows, preferred_element_type=F32)
        rs = lax.rsqrt(jnp.mean(o * o, axis=-1, keepdims=True) + EPS)
        a_ref[:, sl] = (o * rs * og_ref[:, sl] * gate[:, sl]).astype(a_ref.dtype)

    @pl.when(chunk == pl.num_programs(2) - 1)
    def _():
        for s in range(n_seq):
            for h in range(SCAN_HEADS):
                st_ref[s, h] = st_ref[s, h].T


def _hgrn_scan(proj, lb_logits, o_gain, s0, layer, seq_rows):
    rows = proj.shape[0]
    n_state = s0.shape[0]
    n = SCAN_ROWS
    n_seq = max(n // seq_rows, 1)
    chunks = max(seq_rows // n, 1)
    levels = [min(seq_rows, n) >> l for l in range(int(np.log2(min(seq_rows, n))))]
    mat = _scan_matrix(min(seq_rows, n), levels)
    hg = REC_HEADS // SCAN_HEADS
    wb = SCAN_HEADS * LANES
    sect = lambda k: pl.BlockSpec((n, wb), lambda b, h, c: (b * chunks + c, k * hg + h))
    head_vec = lambda r: pl.BlockSpec((r, wb), lambda b, h, c: (0, h))
    state = pl.BlockSpec((n_seq, SCAN_HEADS, REC_DK, REC_DV), lambda b, h, c: (b, h, 0, 0))
    return pl.pallas_call(
        functools.partial(_scan_kernel, layer=layer, n_seq=n_seq, levels=levels),
        grid=(n_state // n_seq, hg, chunks),
        in_specs=[
            sect(0), sect(1), sect(2), sect(3),
            head_vec(lb_logits.shape[0]), head_vec(1),
            pl.BlockSpec(mat.shape, lambda b, h, c: (0, 0)),
            state,
        ],
        out_specs=[pl.BlockSpec((n, wb), lambda b, h, c: (b * chunks + c, h)), state],
        out_shape=[
            jax.ShapeDtypeStruct((rows, D_MODEL), BF16),
            jax.ShapeDtypeStruct(s0.shape, F32),
        ],
        compiler_params=_compiler_params(("parallel", "parallel", "arbitrary"), 48),
        name="hgrn_scan",
    )(proj, proj, proj, proj, lb_logits, o_gain.reshape(1, D_MODEL), mat, s0)


def _outproj_kernel(a_ref, x_ref, g_ref, w_ref, o_ref):
    y = jnp.dot(a_ref[...].astype(BF16), w_ref[...], preferred_element_type=F32)
    o_ref[...] = x_ref[...] + g_ref[0] * y


def _outproj(a, x, mod, w, tm):
    rows = x.shape[0]
    row = lambda i: (i, 0)
    return pl.pallas_call(
        _outproj_kernel,
        grid=(rows // tm,),
        in_specs=[
            pl.BlockSpec((tm, D_MODEL), row),
            pl.BlockSpec((tm, D_MODEL), row),
            mod.spec(tm),
            pl.BlockSpec((D_MODEL, D_MODEL), lambda i: (0, 0)),
        ],
        out_specs=pl.BlockSpec((tm, D_MODEL), row),
        out_shape=jax.ShapeDtypeStruct((rows, D_MODEL), F32),
        compiler_params=_compiler_params(("parallel",), 48),
        name="outproj",
    )(a, x, mod.parts[2], w)


def _mlp_kernel(x_ref, gain_ref, sc_ref, sh_ref, g_ref, wu_ref, wd_ref, o_ref, h_scr):
    j = pl.program_id(1)

    @pl.when(j == 0)
    def _():
        h_scr[...] = _modnorm(x_ref[...], gain_ref[...], sc_ref[0], sh_ref[0]).astype(BF16)

    u = jnp.maximum(jnp.dot(h_scr[...], wu_ref[...], preferred_element_type=F32), 0.0)
    d = jnp.dot((u * u).astype(BF16), wd_ref[...], preferred_element_type=F32)

    @pl.when(j == 0)
    def _():
        o_ref[...] = d

    @pl.when(j > 0)
    def _():
        o_ref[...] += d

    @pl.when(j == pl.num_programs(1) - 1)
    def _():
        o_ref[...] = x_ref[...] + g_ref[0] * o_ref[...]


def _mlp(x, gain, mod, w_up, w_down, tm, tf):
    rows = x.shape[0]
    row = lambda i, j: (i, 0)
    return pl.pallas_call(
        _mlp_kernel,
        grid=(rows // tm, D_FF // tf),
        in_specs=[
            pl.BlockSpec((tm, D_MODEL), row),
            pl.BlockSpec((1, D_MODEL), lambda i, j: (0, 0)),
            mod.spec(tm), mod.spec(tm), mod.spec(tm),
            pl.BlockSpec((D_MODEL, tf), lambda i, j: (0, j)),
            pl.BlockSpec((tf, D_MODEL), lambda i, j: (j, 0)),
        ],
        out_specs=pl.BlockSpec((tm, D_MODEL), row),
        out_shape=jax.ShapeDtypeStruct((rows, D_MODEL), F32),
        scratch_shapes=[pltpu.VMEM((tm, D_MODEL), BF16)],
        compiler_params=_compiler_params(("parallel", "arbitrary"), 56),
        name="mlp",
    )(x, gain, mod.parts[4], mod.parts[3], mod.parts[5], w_up, w_down)


def _row_tile(rows):
    return min(rows, 512)


def _attn_layer(x, mod, gain_mix, w_qkv, q_gain, k_gain, sinks, w_o, batch, seq, cache=None):
    rows = batch * seq
    tm = _row_tile(rows)
    if cache is None:
        tables = _rope_tables(jnp.arange(seq, dtype=jnp.int32))
        q, k, v = _qkv(x, gain_mix, mod, w_qkv, q_gain, k_gain, tables, seq, tm, BF16)
        o = _attn_prompt(q, k, v, sinks, batch, seq)
        k_win = k.reshape(batch, seq, N_KV_HEADS, HEAD_DIM)[:, seq - WINDOW:]
        v_win = v.reshape(batch, seq, N_KV_HEADS, HEAD_DIM)[:, seq - WINDOW:]
    else:
        pos = PAST_LEN + jnp.arange(seq, dtype=jnp.int32)
        tables = tuple(jnp.tile(t, (batch, 1)) for t in _rope_tables(pos))
        q, k, v = _qkv(x, gain_mix, mod, w_qkv, q_gain, k_gain, tables, rows, tm, F32)
        ck = cache[0].reshape(batch, WINDOW, KV_WIDTH)
        cv = cache[1].reshape(batch, WINDOW, KV_WIDTH)
        o, k_win, v_win = _attn_sample(q, k, v, ck, cv, sinks, batch, seq)
        k_win = k_win.reshape(batch, WINDOW, N_KV_HEADS, HEAD_DIM)
        v_win = v_win.reshape(batch, WINDOW, N_KV_HEADS, HEAD_DIM)
    return _outproj(o, x, mod, w_o, tm), k_win, v_win


def _hgrn_layer(x, mod, gain_mix, w_in, lb_logits, o_gain, w_o, s0, layer, batch, seq):
    rows = batch * seq
    tm = _row_tile(rows)
    proj = _norm_matmul(x, gain_mix, mod, w_in, tm, D_MODEL)
    a, state = _hgrn_scan(proj, lb_logits, o_gain, s0, layer, seq)
    return _outproj(a, x, mod, w_o, tm), state


def kernel(x_prompt, x_sample, cache_k_win, cache_v_win, state_hgrn, c_prompt, c_sample, norm_gain, w_ada, b_ada,
           attn_w_qkv, attn_q_gain, attn_k_gain, attn_sinks, attn_w_o, rec_w_in, rec_lb_logits, rec_o_gain,
           rec_w_o, mlp_w_up, mlp_w_down):
    bp, lp, _ = x_prompt.shape
    bs, ls, _ = x_sample.shape
    n_c = bp + bs
    c_all = jnp.concatenate([c_prompt, c_sample, jnp.zeros((-n_c % 16, D_MODEL), F32)], axis=0)
    mods = _ada_mods(c_all, w_ada, b_ada)

    xp = x_prompt.reshape(bp * lp, D_MODEL)
    xs = x_sample.reshape(bs * ls, D_MODEL)
    lb_logits = rec_lb_logits.astype(F32)
    kwp, vwp, kws, vws, sp, ss = [], [], [], [], [], []
    for i in range(DEPTH):
        j = i // N_MIXERS
        mod_p = _Mod(mods[i, :bp], lp, per_row=False)
        mod_s = _Mod(mods[i, bp:n_c], ls, per_row=True)
        gain_mix = norm_gain[i, 0].reshape(1, D_MODEL)
        gain_mlp = norm_gain[i, 1].reshape(1, D_MODEL)
        if i % N_MIXERS == 0:
            aw = (attn_w_qkv[j].astype(BF16), attn_q_gain[j], attn_k_gain[j], attn_sinks[j],
                  attn_w_o[j].astype(BF16))
            xp, kp, vp = _attn_layer(xp, mod_p, gain_mix, *aw, bp, lp)
            xs, kn, vn = _attn_layer(xs, mod_s, gain_mix, *aw, bs, ls, cache=(cache_k_win[j], cache_v_win[j]))
            kwp.append(kp); vwp.append(vp); kws.append(kn); vws.append(vn)
        else:
            rw = (rec_w_in[j].astype(BF16), lb_logits, rec_o_gain[j], rec_w_o[j].astype(BF16))
            s0p = jnp.zeros((bp, REC_HEADS, REC_DK, REC_DV), F32)
            xp, s_p = _hgrn_layer(xp, mod_p, gain_mix, *rw, s0p, j, bp, lp)
            xs, s_s = _hgrn_layer(xs, mod_s, gain_mix, *rw, state_hgrn[j], j, bs, ls)
            sp.append(s_p); ss.append(s_s)
        w_up = mlp_w_up[i].astype(BF16)
        w_down = mlp_w_down[i].astype(BF16)
        xp = _mlp(xp, gain_mlp, mod_p, w_up, w_down, _row_tile(bp * lp), 1024)
        xs = _mlp(xs, gain_mlp, mod_s, w_up, w_down, _row_tile(bs * ls), 1024)
    return (xp.reshape(bp, lp, D_MODEL), xs.reshape(bs, ls, D_MODEL),
            jnp.stack(kwp), jnp.stack(vwp), jnp.stack(kws), jnp.stack(vws), jnp.stack(sp), jnp.stack(ss))
```

```python
import functools

import numpy as np
import jax
import jax.numpy as jnp
from jax import lax
from jax.experimental import pallas as pl
from jax.experimental.pallas import tpu as pltpu

F32 = jnp.float32
BF16 = jnp.bfloat16

D_MODEL = 2048
DEPTH = 4
N_MIXERS = 2
PAST_LEN = 16384
HEAD_DIM = 64
N_Q_HEADS = D_MODEL // HEAD_DIM
N_KV_HEADS = N_Q_HEADS // 8
GQA_GROUP = N_Q_HEADS // N_KV_HEADS
KV_WIDTH = N_KV_HEADS * HEAD_DIM
WINDOW = 128
ROT_DIM = HEAD_DIM // 4
ROPE_THETA = 500000.0
REC_HEADS = 16
REC_DK = 128
REC_DV = D_MODEL // REC_HEADS
D_FF = 4 * D_MODEL
EPS = 1e-6

LANES = 128
MIB = 1024 * 1024
SCAN_ROWS = 128
SCAN_HEADS = 4


def _compiler_params(semantics, vmem_mib):
    return pltpu.CompilerParams(dimension_semantics=semantics, vmem_limit_bytes=vmem_mib * MIB)


def _sigmoid(x):
    e = jnp.exp(-jnp.abs(x))
    r = 1.0 / (1.0 + e)
    return jnp.where(x >= 0, r, e * r)


def _silu(x):
    return x * _sigmoid(x)


def _modnorm(x, gain, scale, shift):
    var = jnp.mean(x * x, axis=-1, keepdims=True)
    y = x * lax.rsqrt(var + EPS) * gain
    return y * (1.0 + scale) + shift


def _split_bf16(x, parts):
    out = []
    for _ in range(parts - 1):
        hi = x.astype(BF16)
        out.append(hi)
        x = x - hi.astype(F32)
    out.append(x.astype(BF16))
    return out


class _Mod:
    def __init__(self, mod, rows_per_group, per_row):
        self.rows_per_group = rows_per_group
        self.per_row = per_row
        parts = jnp.split(mod, 6, axis=-1)
        if per_row:
            self.parts = [jnp.repeat(p, rows_per_group, axis=0)[None] for p in parts]
        else:
            self.parts = [p[:, None, :] for p in parts]

    def spec(self, tm, tn=D_MODEL, col=lambda *_: 0):
        if self.per_row:
            return pl.BlockSpec((1, tm, tn), lambda i, *rest: (0, i, col(i, *rest)))
        assert self.rows_per_group % tm == 0
        per = self.rows_per_group // tm
        return pl.BlockSpec((1, 1, tn), lambda i, *rest: (i // per, 0, col(i, *rest)))


def _ada_kernel(c_ref, w_ref, b_ref, o_ref):
    s = _silu(c_ref[...]).astype(BF16)
    o_ref[0] = jnp.dot(s, w_ref[0].astype(BF16), preferred_element_type=F32) + b_ref[0]


def _ada_mods(c_all, w_ada, b_ada):
    rows = c_all.shape[0]
    tn = 1024
    n = 6 * D_MODEL
    return pl.pallas_call(
        _ada_kernel,
        grid=(DEPTH, n // tn),
        in_specs=[
            pl.BlockSpec((rows, D_MODEL), lambda l, j: (0, 0)),
            pl.BlockSpec((1, D_MODEL, tn), lambda l, j: (l, 0, j)),
            pl.BlockSpec((1, 1, tn), lambda l, j: (l, 0, j)),
        ],
        out_specs=pl.BlockSpec((1, rows, tn), lambda l, j: (l, 0, j)),
        out_shape=jax.ShapeDtypeStruct((DEPTH, rows, n), F32),
        compiler_params=_compiler_params(("parallel", "parallel"), 40),
        name="ada_mods",
    )(c_all, w_ada, b_ada.reshape(DEPTH, 1, n))


def _norm_matmul_kernel(x_ref, gain_ref, sc_ref, sh_ref, w_ref, o_ref, h_scr):
    @pl.when(pl.program_id(1) == 0)
    def _():
        h_scr[...] = _modnorm(x_ref[...], gain_ref[...], sc_ref[0], sh_ref[0]).astype(BF16)

    o_ref[...] = jnp.dot(h_scr[...], w_ref[...], preferred_element_type=F32)


def _norm_matmul(x, gain, mod, w, layer, tm, tn):
    rows = x.shape[0]
    n = w.shape[-1]
    return pl.pallas_call(
        _norm_matmul_kernel,
        grid=(rows // tm, n // tn),
        in_specs=[
            pl.BlockSpec((tm, D_MODEL), lambda i, j: (i, 0)),
            pl.BlockSpec((1, D_MODEL), lambda i, j: (0, 0)),
            mod.spec(tm),
            mod.spec(tm),
            pl.BlockSpec((None, D_MODEL, tn), lambda i, j: (layer, 0, j)),
        ],
        out_specs=pl.BlockSpec((tm, tn), lambda i, j: (i, j)),
        out_shape=jax.ShapeDtypeStruct((rows, n), F32),
        scratch_shapes=[pltpu.VMEM((tm, D_MODEL), BF16)],
        compiler_params=_compiler_params(("parallel", "arbitrary"), 48),
        name="norm_matmul",
    )(x, gain, mod.parts[1], mod.parts[0], w)


def _qkv_kernel(x_ref, gain_ref, sc_ref, sh_ref, w_ref, qg_ref, kg_ref, cos_ref, s1_ref, s2_ref, bd_ref,
                q_ref, k_ref, v_ref):
    h = _modnorm(x_ref[...], gain_ref[...], sc_ref[0], sh_ref[0]).astype(BF16)
    acc = jnp.dot(h, w_ref[...], preferred_element_type=F32)
    cos, s1, s2 = cos_ref[...], s1_ref[...], s2_ref[...]
    bd = bd_ref[...]
    qk_width = D_MODEL + KV_WIDTH
    for c in range(qk_width // LANES):
        xc = acc[:, c * LANES:(c + 1) * LANES]
        hi, lo = _split_bf16(xc * xc, 2)
        ssq = (jnp.dot(hi, bd, preferred_element_type=F32) + jnp.dot(lo, bd, preferred_element_type=F32))
        is_q = c * LANES < D_MODEL
        y = xc * lax.rsqrt(ssq * (1.0 / HEAD_DIM) + EPS) * (qg_ref[...] if is_q else kg_ref[...])
        r = y * cos + pltpu.roll(y, LANES - ROT_DIM // 2, 1) * s1 + pltpu.roll(y, ROT_DIM // 2, 1) * s2
        if is_q:
            q_ref[:, c * LANES:(c + 1) * LANES] = (r * (HEAD_DIM ** -0.5)).astype(q_ref.dtype)
        else:
            k_ref[:, c * LANES - D_MODEL:(c + 1) * LANES - D_MODEL] = r
    v_ref[...] = acc[:, qk_width:]


def _rope_tables(pos):
    half = ROT_DIM // 2
    inv = ROPE_THETA ** (-jnp.arange(half, dtype=F32) / half)
    ang = pos.astype(F32)[:, None] * inv[None, :]
    cos, sin = jnp.cos(ang), jnp.sin(ang)
    n = pos.shape[0]
    pad = jnp.zeros((n, HEAD_DIM - ROT_DIM), F32)
    zero = jnp.zeros((n, half), F32)
    c = jnp.concatenate([cos, cos, pad + 1.0], axis=1)
    s1 = jnp.concatenate([-sin, zero, pad], axis=1)
    s2 = jnp.concatenate([zero, sin, pad], axis=1)
    return tuple(jnp.tile(t, (1, LANES // HEAD_DIM)) for t in (c, s1, s2))


def _head_sum_matrix():
    idx = np.arange(LANES) // HEAD_DIM
    return jnp.asarray(idx[:, None] == idx[None, :], dtype=BF16)


def _qkv(x, gain, mod, w, layer, q_gain, k_gain, tables, table_rows, tm, q_dtype):
    rows = x.shape[0]
    n = w.shape[-1]
    per = table_rows // tm
    tspec = pl.BlockSpec((tm, LANES), lambda i: (i % per, 0))
    vspec = pl.BlockSpec((1, LANES), lambda i: (0, 0))
    reps = LANES // HEAD_DIM
    return pl.pallas_call(
        _qkv_kernel,
        grid=(rows // tm,),
        in_specs=[
            pl.BlockSpec((tm, D_MODEL), lambda i: (i, 0)),
            pl.BlockSpec((1, D_MODEL), lambda i: (0, 0)),
            mod.spec(tm),
            mod.spec(tm),
            pl.BlockSpec((None, D_MODEL, n), lambda i: (layer, 0, 0)),
            vspec, vspec, tspec, tspec, tspec,
            pl.BlockSpec((LANES, LANES), lambda i: (0, 0)),
        ],
        out_specs=[
            pl.BlockSpec((tm, D_MODEL), lambda i: (i, 0)),
            pl.BlockSpec((tm, KV_WIDTH), lambda i: (i, 0)),
            pl.BlockSpec((tm, KV_WIDTH), lambda i: (i, 0)),
        ],
        out_shape=[
            jax.ShapeDtypeStruct((rows, D_MODEL), q_dtype),
            jax.ShapeDtypeStruct((rows, KV_WIDTH), F32),
            jax.ShapeDtypeStruct((rows, KV_WIDTH), F32),
        ],
        compiler_params=_compiler_params(("parallel",), 48),
        name="qkv",
    )(x, gain, mod.parts[1], mod.parts[0], w,
      jnp.tile(q_gain.reshape(1, HEAD_DIM), (1, reps)), jnp.tile(k_gain.reshape(1, HEAD_DIM), (1, reps)),
      *tables, _head_sum_matrix())


def _softmax_with_sink(s, sink):
    m = jnp.maximum(jnp.max(s, axis=-1, keepdims=True), sink)
    p = jnp.exp(s - m)
    denom = jnp.sum(p, axis=-1, keepdims=True) + jnp.exp(sink - m)
    return p, denom


def _attn_prompt_kernel(sink_ref, q_ref, kp_ref, kc_ref, vp_ref, vc_ref, o_ref):
    j = pl.program_id(1)
    kk = jnp.concatenate([kp_ref[...], kc_ref[...]], axis=0).astype(BF16)
    vv = jnp.concatenate([vp_ref[...], vc_ref[...]], axis=0).astype(BF16)
    t = lax.broadcasted_iota(jnp.int32, (WINDOW, 2 * WINDOW), 0)
    s = lax.broadcasted_iota(jnp.int32, (WINDOW, 2 * WINDOW), 1)
    ok = (s > t) & (s <= t + WINDOW) & (s >= jnp.where(j > 0, 0, WINDOW))
    for g in range(N_KV_HEADS):
        kg = kk[:, g * HEAD_DIM:(g + 1) * HEAD_DIM]
        vg = vv[:, g * HEAD_DIM:(g + 1) * HEAD_DIM]
        for hp in range(GQA_GROUP // 2):
            outs = []
            for h in (g * GQA_GROUP + 2 * hp, g * GQA_GROUP + 2 * hp + 1):
                qh = q_ref[:, h * HEAD_DIM:(h + 1) * HEAD_DIM]
                sc = lax.dot_general(qh, kg, (((1,), (1,)), ((), ())), preferred_element_type=F32)
                p, denom = _softmax_with_sink(jnp.where(ok, sc, -jnp.inf), sink_ref[h])
                outs.append(jnp.dot(p.astype(BF16), vg, preferred_element_type=F32) / denom)
            lo = 2 * (g * GQA_GROUP // 2 + hp) * HEAD_DIM
            o_ref[:, lo:lo + 2 * HEAD_DIM] = jnp.concatenate(outs, axis=1).astype(o_ref.dtype)


def _attn_prompt(q, k, v, sinks, batch, seq):
    nb = seq // WINDOW
    rows = batch * seq
    cur = lambda b, j: (b * nb + j, 0)
    prev = lambda b, j: (b * nb + jnp.maximum(j - 1, 0), 0)
    kv_block = (WINDOW, KV_WIDTH)
    return pl.pallas_call(
        _attn_prompt_kernel,
        grid=(batch, nb),
        in_specs=[
            pl.BlockSpec(memory_space=pltpu.SMEM),
            pl.BlockSpec((WINDOW, D_MODEL), cur),
            pl.BlockSpec(kv_block, prev), pl.BlockSpec(kv_block, cur),
            pl.BlockSpec(kv_block, prev), pl.BlockSpec(kv_block, cur),
        ],
        out_specs=pl.BlockSpec((WINDOW, D_MODEL), cur),
        out_shape=jax.ShapeDtypeStruct((rows, D_MODEL), BF16),
        compiler_params=_compiler_params(("parallel", "parallel"), 32),
        name="attn_prompt",
    )(sinks, q, k, k, v, v)


def _attn_sample_kernel(sink_ref, q_ref, k_ref, v_ref, ck_ref, cv_ref, o_ref, nk_ref, nv_ref, *, steps):
    kc, vc = ck_ref[0], cv_ref[0]
    kn, vn = k_ref[...], v_ref[...]
    nk_ref[0, :WINDOW - steps] = kc[steps:]
    nk_ref[0, WINDOW - steps:] = kn
    nv_ref[0, :WINDOW - steps] = vc[steps:]
    nv_ref[0, WINDOW - steps:] = vn
    zpad = jnp.zeros((WINDOW - steps, KV_WIDTH), F32)
    kk = jnp.concatenate([kc, kn, zpad], axis=0).astype(BF16)
    vv = jnp.concatenate([vc, vn, zpad], axis=0).astype(BF16)
    rows = GQA_GROUP * steps
    assert steps & (steps - 1) == 0
    t = lax.broadcasted_iota(jnp.int32, (rows, 2 * WINDOW), 0) & (steps - 1)
    s = lax.broadcasted_iota(jnp.int32, (rows, 2 * WINDOW), 1)
    ok = ((s < WINDOW) & (s > t)) | ((s >= WINDOW) & (s - WINDOW <= t))
    q = q_ref[...]
    for g in range(N_KV_HEADS):
        heads = range(g * GQA_GROUP, (g + 1) * GQA_GROUP)
        qg = jnp.concatenate([q[:, h * HEAD_DIM:(h + 1) * HEAD_DIM] for h in heads], axis=0).astype(BF16)
        sc = lax.dot_general(qg, kk[:, g * HEAD_DIM:(g + 1) * HEAD_DIM], (((1,), (1,)), ((), ())),
                             preferred_element_type=F32)
        p, denom = _softmax_with_sink(jnp.where(ok, sc, -jnp.inf), sink_ref[g])
        og = jnp.dot(p.astype(BF16), vv[:, g * HEAD_DIM:(g + 1) * HEAD_DIM], preferred_element_type=F32) / denom
        for i, h in enumerate(heads):
            o_ref[:, h * HEAD_DIM:(h + 1) * HEAD_DIM] = og[i * steps:(i + 1) * steps]


def _attn_sample(q, k, v, cache_k, cache_v, layer, sinks, batch, steps):
    rows = batch * steps
    sink_rows = jnp.repeat(sinks.reshape(N_KV_HEADS, GQA_GROUP), steps, axis=1)[..., None]
    row = lambda b: (b, 0)
    win = lambda b: (b, 0, 0)
    cache_block = (1, WINDOW, KV_WIDTH)
    return pl.pallas_call(
        functools.partial(_attn_sample_kernel, steps=steps),
        grid=(batch,),
        in_specs=[
            pl.BlockSpec((N_KV_HEADS, GQA_GROUP * steps, 1), lambda b: (0, 0, 0)),
            pl.BlockSpec((steps, D_MODEL), row),
            pl.BlockSpec((steps, KV_WIDTH), row), pl.BlockSpec((steps, KV_WIDTH), row),
            pl.BlockSpec((None,) + cache_block, lambda b: (layer, b, 0, 0)),
            pl.BlockSpec((None,) + cache_block, lambda b: (layer, b, 0, 0)),
        ],
        out_specs=[
            pl.BlockSpec((steps, D_MODEL), row),
            pl.BlockSpec(cache_block, win), pl.BlockSpec(cache_block, win),
        ],
        out_shape=[
            jax.ShapeDtypeStruct((rows, D_MODEL), F32),
            jax.ShapeDtypeStruct((batch, WINDOW, KV_WIDTH), F32),
            jax.ShapeDtypeStruct((batch, WINDOW, KV_WIDTH), F32),
        ],
        compiler_params=_compiler_params(("parallel",), 32),
        name="attn_sample",
    )(sink_rows, q, k, v, cache_k, cache_v)


def _scan_matrix(seq_rows, levels):
    n = SCAN_ROWS
    t = np.arange(n)[:, None]
    r = np.arange(n)[None, :]
    same = (t // seq_rows) == (r // seq_rows)
    mats = [same & (r <= t), same & (r > t)]
    for b in levels:
        mid = (t // b) * b + b // 2
        upper = (t & (b // 2)) != 0
        mats.append(np.where(upper, (r >= mid) & (r <= t), (r > t) & (r < mid)))
    return jnp.asarray(np.concatenate(mats, axis=0), dtype=BF16)


def _scan_kernel(q_ref, f_ref, i_ref, g_ref, lbl_ref, og_ref, mat_ref, *rest, layer, n_seq, levels):
    s0_ref = rest[0] if len(rest) == 3 else None
    a_ref, st_ref = rest[-2:]
    n = SCAN_ROWS
    seq_rows = n // n_seq
    chunk = pl.program_id(2)

    @pl.when(chunk == 0)
    def _():
        for s in range(n_seq):
            for h in range(SCAN_HEADS):
                st_ref[s, h] = jnp.zeros((REC_DV, REC_DK), F32) if s0_ref is None else s0_ref[s, h].T

    logits = lbl_ref[...]
    pexp = jnp.exp(logits - jnp.max(logits, axis=0, keepdims=True))
    psum = jnp.sum(pexp, axis=0, keepdims=True)
    lb = jnp.zeros_like(psum)
    for i in range(1, layer + 1):
        lb = lb + pexp[i:i + 1] / psum

    qs = _silu(q_ref[...])
    fz = f_ref[...]
    e = jnp.exp(-jnp.abs(fz))
    r = 1.0 / (1.0 + e)
    pos = fz >= 0
    logf = jnp.log(lb + (1.0 - lb) * jnp.where(pos, r, e * r))
    kin = (1.0 - lb) * jnp.where(pos, e * r, r)
    val = i_ref[...]
    gate = _silu(g_ref[...])

    mat = mat_ref[...]
    sums = None
    for part in _split_bf16(logf, 2):
        d = jnp.dot(mat, part, preferred_element_type=F32)
        sums = d if sums is None else sums + d
    gcum = sums[:n]
    gtail = sums[n:2 * n]

    t_lane = lax.broadcasted_iota(jnp.int32, (n, LANES), 0)
    tt = lax.broadcasted_iota(jnp.int32, (n, n), 0)
    ss = lax.broadcasted_iota(jnp.int32, (n, n), 1)
    contract_lanes = (((1,), (1,)), ((), ()))
    contract_rows = (((0,), (0,)), ((), ()))

    for h in range(SCAN_HEADS):
        sl = slice(h * LANES, (h + 1) * LANES)
        qh, kh, vh = qs[:, sl], kin[:, sl], val[:, sl]
        vb = vh.astype(BF16)
        a = jnp.zeros((n, n), F32)
        for l, b in enumerate(levels):
            decay = jnp.exp(sums[(l + 2) * n:(l + 3) * n, sl])
            upper = (t_lane & (b // 2)) != 0
            qt = jnp.where(upper, qh * decay, 0.0).astype(BF16)
            kt = jnp.where(upper, 0.0, kh * decay).astype(BF16)
            al = lax.dot_general(qt, kt, contract_lanes, preferred_element_type=F32)
            a = a + (al if b == n else jnp.where((tt ^ ss) < b, al, 0.0))
        o = jnp.dot(a.astype(BF16), vb, preferred_element_type=F32)
        o = o + jnp.sum(qh * kh, axis=-1, keepdims=True) * vh
        qe = qh * jnp.exp(gcum[:, sl])
        kt = kh * jnp.exp(gtail[:, sl])
        for s in range(n_seq):
            last = (s + 1) * seq_rows - 1
            if n_seq == 1:
                qe_s, kt_s = qe, kt
            else:
                inside = (t_lane >= s * seq_rows) & (t_lane <= last)
                qe_s, kt_s = jnp.where(inside, qe, 0.0), jnp.where(inside, kt, 0.0)
            st = st_ref[s, h]
            o = o + lax.dot_general(qe_s.astype(BF16), st.astype(BF16), contract_lanes,
                                    preferred_element_type=F32)
            st_ref[s, h] = st * jnp.exp(gcum[last:last + 1, sl]) + lax.dot_general(
                vb, kt_s.astype(BF16), contract_rows, preferred_element_type=F32)
        rs = lax.rsqrt(jnp.mean(o * o, axis=-1, keepdims=True) + EPS)
        a_ref[:, sl] = (o * rs * og_ref[:, sl] * gate[:, sl]).astype(a_ref.dtype)

    @pl.when(chunk == pl.num_programs(2) - 1)
    def _():
        for s in range(n_seq):
            for h in range(SCAN_HEADS):
                st_ref[s, h] = st_ref[s, h].T


def _hgrn_scan(proj, lb_logits, o_gain, s0, layer, n_state, seq_rows):
    rows = proj.shape[0]
    n = SCAN_ROWS
    n_seq = max(n // seq_rows, 1)
    chunks = max(seq_rows // n, 1)
    levels = [min(seq_rows, n) >> l for l in range(int(np.log2(min(seq_rows, n))))]
    mat = _scan_matrix(min(seq_rows, n), levels)
    hg = REC_HEADS // SCAN_HEADS
    wb = SCAN_HEADS * LANES
    sect = lambda k: pl.BlockSpec((n, wb), lambda b, h, c: (b * chunks + c, k * hg + h))
    head_vec = lambda r: pl.BlockSpec((r, wb), lambda b, h, c: (0, h))
    state_block = (n_seq, SCAN_HEADS, REC_DK, REC_DV)
    state = pl.BlockSpec(state_block, lambda b, h, c: (b, h, 0, 0))
    state_in = [] if s0 is None else [pl.BlockSpec((None,) + state_block, lambda b, h, c: (layer, b, h, 0, 0))]
    return pl.pallas_call(
        functools.partial(_scan_kernel, layer=layer, n_seq=n_seq, levels=levels),
        grid=(n_state // n_seq, hg, chunks),
        in_specs=[
            sect(0), sect(1), sect(2), sect(3),
            head_vec(lb_logits.shape[0]), head_vec(1),
            pl.BlockSpec(mat.shape, lambda b, h, c: (0, 0)),
        ] + state_in,
        out_specs=[pl.BlockSpec((n, wb), lambda b, h, c: (b * chunks + c, h)), state],
        out_shape=[
            jax.ShapeDtypeStruct((rows, D_MODEL), BF16),
            jax.ShapeDtypeStruct((n_state, REC_HEADS, REC_DK, REC_DV), F32),
        ],
        compiler_params=_compiler_params(("parallel", "parallel", "arbitrary"), 48),
        name="hgrn_scan",
    )(proj, proj, proj, proj, lb_logits, o_gain.reshape(1, D_MODEL), mat, *([] if s0 is None else [s0]))


def _outproj_kernel(a_ref, x_ref, g_ref, w_ref, o_ref):
    y = jnp.dot(a_ref[...].astype(BF16), w_ref[...], preferred_element_type=F32)
    o_ref[...] = x_ref[...] + g_ref[0] * y


def _outproj(a, x, mod, w, layer, tm):
    rows = x.shape[0]
    row = lambda i: (i, 0)
    return pl.pallas_call(
        _outproj_kernel,
        grid=(rows // tm,),
        in_specs=[
            pl.BlockSpec((tm, D_MODEL), row),
            pl.BlockSpec((tm, D_MODEL), row),
            mod.spec(tm),
            pl.BlockSpec((None, D_MODEL, D_MODEL), lambda i: (layer, 0, 0)),
        ],
        out_specs=pl.BlockSpec((tm, D_MODEL), row),
        out_shape=jax.ShapeDtypeStruct((rows, D_MODEL), F32),
        compiler_params=_compiler_params(("parallel",), 48),
        name="outproj",
    )(a, x, mod.parts[2], w)


def _mlp_kernel(x_ref, gain_ref, sc_ref, sh_ref, xo_ref, g_ref, wu_ref, wd_ref, o_ref, h_scr, u_scr, *, n_up, tf):
    j = pl.program_id(1)

    @pl.when(j == 0)
    def _():
        h_scr[...] = _modnorm(x_ref[...], gain_ref[...], sc_ref[0], sh_ref[0]).astype(BF16)

    @pl.when(j < n_up)
    def _():
        u = jnp.maximum(jnp.dot(h_scr[...], wu_ref[...], preferred_element_type=F32), 0.0)
        u2 = (u * u).astype(BF16)
        for c in range(n_up):
            @pl.when(j == c)
            def _():
                u_scr[:, c * tf:(c + 1) * tf] = u2

    @pl.when(j >= n_up)
    def _():
        d = jnp.dot(u_scr[...], wd_ref[...], preferred_element_type=F32)
        o_ref[...] = xo_ref[...] + g_ref[0] * d


def _mlp(x, gain, mod, w_up, w_down, layer, tm, tf=1024, tn=512):
    rows = x.shape[0]
    n_up, n_down = D_FF // tf, D_MODEL // tn
    row = lambda i, j: (i, 0)
    out_col = lambda i, j: jnp.maximum(j - n_up, 0)
    return pl.pallas_call(
        functools.partial(_mlp_kernel, n_up=n_up, tf=tf),
        grid=(rows // tm, n_up + n_down),
        in_specs=[
            pl.BlockSpec((tm, D_MODEL), row),
            pl.BlockSpec((1, D_MODEL), lambda i, j: (0, 0)),
            mod.spec(tm), mod.spec(tm),
            pl.BlockSpec((tm, tn), lambda i, j: (i, out_col(i, j))),
            mod.spec(tm, tn, out_col),
            pl.BlockSpec((None, D_MODEL, tf), lambda i, j: (layer, 0, jnp.minimum(j, n_up - 1))),
            pl.BlockSpec((None, D_FF, tn), lambda i, j: (layer, 0, out_col(i, j))),
        ],
        out_specs=pl.BlockSpec((tm, tn), lambda i, j: (i, out_col(i, j))),
        out_shape=jax.ShapeDtypeStruct((rows, D_MODEL), F32),
        scratch_shapes=[pltpu.VMEM((tm, D_MODEL), BF16), pltpu.VMEM((tm, D_FF), BF16)],
        compiler_params=_compiler_params(("parallel", "arbitrary"), 56),
        name="mlp",
    )(x, gain, mod.parts[4], mod.parts[3], x, mod.parts[5], w_up, w_down)


def _row_tile(rows):
    return min(rows, 512)


def _attn_layer(x, mod, gain_mix, w_qkv, q_gain, k_gain, sinks, w_o, layer, batch, seq, cache=None):
    rows = batch * seq
    tm = _row_tile(rows)
    if cache is None:
        tables = _rope_tables(jnp.arange(seq, dtype=jnp.int32))
        q, k, v = _qkv(x, gain_mix, mod, w_qkv, layer, q_gain, k_gain, tables, seq, tm, BF16)
        o = _attn_prompt(q, k, v, sinks, batch, seq)
        k_win = k.reshape(batch, seq, N_KV_HEADS, HEAD_DIM)[:, seq - WINDOW:]
        v_win = v.reshape(batch, seq, N_KV_HEADS, HEAD_DIM)[:, seq - WINDOW:]
    else:
        pos = PAST_LEN + jnp.arange(seq, dtype=jnp.int32)
        tables = tuple(jnp.tile(t, (batch, 1)) for t in _rope_tables(pos))
        q, k, v = _qkv(x, gain_mix, mod, w_qkv, layer, q_gain, k_gain, tables, rows, tm, F32)
        ck = cache[0].reshape(-1, batch, WINDOW, KV_WIDTH)
        cv = cache[1].reshape(-1, batch, WINDOW, KV_WIDTH)
        o, k_win, v_win = _attn_sample(q, k, v, ck, cv, layer, sinks, batch, seq)
        k_win = k_win.reshape(batch, WINDOW, N_KV_HEADS, HEAD_DIM)
        v_win = v_win.reshape(batch, WINDOW, N_KV_HEADS, HEAD_DIM)
    return _outproj(o, x, mod, w_o, layer, tm), k_win, v_win


def _hgrn_layer(x, mod, gain_mix, w_in, lb_logits, o_gain, w_o, s0, layer, batch, seq):
    rows = batch * seq
    tm = _row_tile(rows)
    proj = _norm_matmul(x, gain_mix, mod, w_in, layer, tm, D_MODEL)
    a, state = _hgrn_scan(proj, lb_logits, o_gain, s0, layer, batch, seq)
    return _outproj(a, x, mod, w_o, layer, tm), state


def kernel(x_prompt, x_sample, cache_k_win, cache_v_win, state_hgrn, c_prompt, c_sample, norm_gain, w_ada, b_ada,
           attn_w_qkv, attn_q_gain, attn_k_gain, attn_sinks, attn_w_o, rec_w_in, rec_lb_logits, rec_o_gain,
           rec_w_o, mlp_w_up, mlp_w_down):
    bp, lp, _ = x_prompt.shape
    bs, ls, _ = x_sample.shape
    n_c = bp + bs
    c_all = jnp.concatenate([c_prompt, c_sample, jnp.zeros((-n_c % 16, D_MODEL), F32)], axis=0)
    mods = _ada_mods(c_all, w_ada, b_ada)

    xp = x_prompt.reshape(bp * lp, D_MODEL)
    xs = x_sample.reshape(bs * ls, D_MODEL)
    lb_logits = rec_lb_logits.astype(F32)
    w_qkv, w_ao = attn_w_qkv.astype(BF16), attn_w_o.astype(BF16)
    w_in, w_ro = rec_w_in.astype(BF16), rec_w_o.astype(BF16)
    w_up, w_down = mlp_w_up.astype(BF16), mlp_w_down.astype(BF16)
    kwp, vwp, kws, vws, sp, ss = [], [], [], [], [], []
    for i in range(DEPTH):
        j = i // N_MIXERS
        mod_p = _Mod(mods[i, :bp], lp, per_row=False)
        mod_s = _Mod(mods[i, bp:n_c], ls, per_row=True)
        gain_mix = norm_gain[i, 0].reshape(1, D_MODEL)
        gain_mlp = norm_gain[i, 1].reshape(1, D_MODEL)
        if i % N_MIXERS == 0:
            aw = (w_qkv, attn_q_gain[j], attn_k_gain[j], attn_sinks[j], w_ao, j)
            xp, kp, vp = _attn_layer(xp, mod_p, gain_mix, *aw, bp, lp)
            xs, kn, vn = _attn_layer(xs, mod_s, gain_mix, *aw, bs, ls, cache=(cache_k_win, cache_v_win))
            kwp.append(kp); vwp.append(vp); kws.append(kn); vws.append(vn)
        else:
            rw = (w_in, lb_logits, rec_o_gain[j], w_ro)
            xp, s_p = _hgrn_layer(xp, mod_p, gain_mix, *rw, None, j, bp, lp)
            xs, s_s = _hgrn_layer(xs, mod_s, gain_mix, *rw, state_hgrn, j, bs, ls)
            sp.append(s_p); ss.append(s_s)
        xp = _mlp(xp, gain_mlp, mod_p, w_up, w_down, i, _row_tile(bp * lp))
        xs = _mlp(xs, gain_mlp, mod_s, w_up, w_down, i, _row_tile(bs * ls))
    return (xp.reshape(bp, lp, D_MODEL), xs.reshape(bs, ls, D_MODEL),
            jnp.stack(kwp), jnp.stack(vwp), jnp.stack(kws), jnp.stack(vws), jnp.stack(sp), jnp.stack(ss))
```

```python
import functools

import numpy as np
import jax
import jax.numpy as jnp
from jax import lax
from jax.experimental import pallas as pl
from jax.experimental.pallas import tpu as pltpu

F32 = jnp.float32
BF16 = jnp.bfloat16

D_MODEL = 2048
DEPTH = 4
N_MIXERS = 2
PAST_LEN = 16384
HEAD_DIM = 64
N_Q_HEADS = D_MODEL // HEAD_DIM
N_KV_HEADS = N_Q_HEADS // 8
GQA_GROUP = N_Q_HEADS // N_KV_HEADS
KV_WIDTH = N_KV_HEADS * HEAD_DIM
WINDOW = 128
ROT_DIM = HEAD_DIM // 4
ROPE_THETA = 500000.0
REC_HEADS = 16
REC_DK = 128
REC_DV = D_MODEL // REC_HEADS
D_FF = 4 * D_MODEL
EPS = 1e-6

LANES = 128
MIB = 1024 * 1024
SCAN_ROWS = 128
SCAN_HEADS = 4


def _compiler_params(semantics, vmem_mib):
    return pltpu.CompilerParams(dimension_semantics=semantics, vmem_limit_bytes=vmem_mib * MIB)


def _sigmoid(x):
    e = jnp.exp(-jnp.abs(x))
    r = 1.0 / (1.0 + e)
    return jnp.where(x >= 0, r, e * r)


def _silu(x):
    return x * _sigmoid(x)


def _modnorm(x, gain, scale, shift):
    var = jnp.mean(x * x, axis=-1, keepdims=True)
    y = x * lax.rsqrt(var + EPS) * gain
    return y * (1.0 + scale) + shift


def _split_bf16(x, parts):
    out = []
    for _ in range(parts - 1):
        hi = x.astype(BF16)
        out.append(hi)
        x = x - hi.astype(F32)
    out.append(x.astype(BF16))
    return out


class _Mod:
    def __init__(self, mod, rows_per_group, per_row):
        self.rows_per_group = rows_per_group
        self.per_row = per_row
        parts = jnp.split(mod, 6, axis=-1)
        if per_row:
            self.parts = [jnp.repeat(p, rows_per_group, axis=0)[None] for p in parts]
        else:
            self.parts = [p[:, None, :] for p in parts]

    def spec(self, tm, tn=D_MODEL, col=lambda *_: 0):
        if self.per_row:
            return pl.BlockSpec((1, tm, tn), lambda i, *rest: (0, i, col(i, *rest)))
        assert self.rows_per_group % tm == 0
        per = self.rows_per_group // tm
        return pl.BlockSpec((1, 1, tn), lambda i, *rest: (i // per, 0, col(i, *rest)))


def _ada_kernel(c_ref, w_ref, b_ref, o_ref):
    s = _silu(c_ref[...]).astype(BF16)
    o_ref[0] = jnp.dot(s, w_ref[0].astype(BF16), preferred_element_type=F32) + b_ref[0]


def _ada_mods(c_all, w_ada, b_ada):
    rows = c_all.shape[0]
    tn = 1024
    n = 6 * D_MODEL
    return pl.pallas_call(
        _ada_kernel,
        grid=(DEPTH, n // tn),
        in_specs=[
            pl.BlockSpec((rows, D_MODEL), lambda l, j: (0, 0)),
            pl.BlockSpec((1, D_MODEL, tn), lambda l, j: (l, 0, j)),
            pl.BlockSpec((1, 1, tn), lambda l, j: (l, 0, j)),
        ],
        out_specs=pl.BlockSpec((1, rows, tn), lambda l, j: (l, 0, j)),
        out_shape=jax.ShapeDtypeStruct((DEPTH, rows, n), F32),
        compiler_params=_compiler_params(("parallel", "parallel"), 40),
        name="ada_mods",
    )(c_all, w_ada, b_ada.reshape(DEPTH, 1, n))


def _norm_rows_kernel(x_ref, gain_ref, sc_ref, sh_ref, o_ref):
    o_ref[...] = _modnorm(x_ref[...], gain_ref[...], sc_ref[0], sh_ref[0]).astype(o_ref.dtype)


def _norm_rows(x, gain, mod, tm):
    rows = x.shape[0]
    return pl.pallas_call(
        _norm_rows_kernel,
        grid=(rows // tm,),
        in_specs=[
            pl.BlockSpec((tm, D_MODEL), lambda i: (i, 0)),
            pl.BlockSpec((1, D_MODEL), lambda i: (0, 0)),
            mod.spec(tm),
            mod.spec(tm),
        ],
        out_specs=pl.BlockSpec((tm, D_MODEL), lambda i: (i, 0)),
        out_shape=jax.ShapeDtypeStruct((rows, D_MODEL), BF16),
        compiler_params=_compiler_params(("parallel",), 32),
        name="norm_rows",
    )(x, gain, mod.parts[1], mod.parts[0])


def _qkv_kernel(x_ref, gain_ref, sc_ref, sh_ref, w_ref, qg_ref, kg_ref, cos_ref, s1_ref, s2_ref, bd_ref,
                q_ref, k_ref, v_ref):
    h = _modnorm(x_ref[...], gain_ref[...], sc_ref[0], sh_ref[0]).astype(BF16)
    acc = jnp.dot(h, w_ref[...], preferred_element_type=F32)
    cos, s1, s2 = cos_ref[...], s1_ref[...], s2_ref[...]
    bd = bd_ref[...]
    qk_width = D_MODEL + KV_WIDTH
    for c in range(qk_width // LANES):
        xc = acc[:, c * LANES:(c + 1) * LANES]
        hi, lo = _split_bf16(xc * xc, 2)
        ssq = (jnp.dot(hi, bd, preferred_element_type=F32) + jnp.dot(lo, bd, preferred_element_type=F32))
        is_q = c * LANES < D_MODEL
        y = xc * lax.rsqrt(ssq * (1.0 / HEAD_DIM) + EPS) * (qg_ref[...] if is_q else kg_ref[...])
        r = y * cos + pltpu.roll(y, LANES - ROT_DIM // 2, 1) * s1 + pltpu.roll(y, ROT_DIM // 2, 1) * s2
        if is_q:
            q_ref[:, c * LANES:(c + 1) * LANES] = (r * (HEAD_DIM ** -0.5)).astype(q_ref.dtype)
        else:
            k_ref[:, c * LANES - D_MODEL:(c + 1) * LANES - D_MODEL] = r
    v_ref[...] = acc[:, qk_width:]


def _rope_tables(pos):
    half = ROT_DIM // 2
    inv = ROPE_THETA ** (-jnp.arange(half, dtype=F32) / half)
    ang = pos.astype(F32)[:, None] * inv[None, :]
    cos, sin = jnp.cos(ang), jnp.sin(ang)
    n = pos.shape[0]
    pad = jnp.zeros((n, HEAD_DIM - ROT_DIM), F32)
    zero = jnp.zeros((n, half), F32)
    c = jnp.concatenate([cos, cos, pad + 1.0], axis=1)
    s1 = jnp.concatenate([-sin, zero, pad], axis=1)
    s2 = jnp.concatenate([zero, sin, pad], axis=1)
    return tuple(jnp.tile(t, (1, LANES // HEAD_DIM)) for t in (c, s1, s2))


def _head_sum_matrix():
    idx = np.arange(LANES) // HEAD_DIM
    return jnp.asarray(idx[:, None] == idx[None, :], dtype=BF16)


def _qkv(x, gain, mod, w, layer, q_gain, k_gain, tables, table_rows, tm, q_dtype):
    rows = x.shape[0]
    n = w.shape[-1]
    per = table_rows // tm
    tspec = pl.BlockSpec((tm, LANES), lambda i: (i % per, 0))
    vspec = pl.BlockSpec((1, LANES), lambda i: (0, 0))
    reps = LANES // HEAD_DIM
    return pl.pallas_call(
        _qkv_kernel,
        grid=(rows // tm,),
        in_specs=[
            pl.BlockSpec((tm, D_MODEL), lambda i: (i, 0)),
            pl.BlockSpec((1, D_MODEL), lambda i: (0, 0)),
            mod.spec(tm),
            mod.spec(tm),
            pl.BlockSpec((None, D_MODEL, n), lambda i: (layer, 0, 0)),
            vspec, vspec, tspec, tspec, tspec,
            pl.BlockSpec((LANES, LANES), lambda i: (0, 0)),
        ],
        out_specs=[
            pl.BlockSpec((tm, D_MODEL), lambda i: (i, 0)),
            pl.BlockSpec((tm, KV_WIDTH), lambda i: (i, 0)),
            pl.BlockSpec((tm, KV_WIDTH), lambda i: (i, 0)),
        ],
        out_shape=[
            jax.ShapeDtypeStruct((rows, D_MODEL), q_dtype),
            jax.ShapeDtypeStruct((rows, KV_WIDTH), F32),
            jax.ShapeDtypeStruct((rows, KV_WIDTH), F32),
        ],
        compiler_params=_compiler_params(("parallel",), 48),
        name="qkv",
    )(x, gain, mod.parts[1], mod.parts[0], w,
      jnp.tile(q_gain.reshape(1, HEAD_DIM), (1, reps)), jnp.tile(k_gain.reshape(1, HEAD_DIM), (1, reps)),
      *tables, _head_sum_matrix())


def _softmax_with_sink(s, sink):
    m = jnp.maximum(jnp.max(s, axis=-1, keepdims=True), sink)
    p = jnp.exp(s - m)
    denom = jnp.sum(p, axis=-1, keepdims=True) + jnp.exp(sink - m)
    return p, denom


def _attn_prompt_kernel(sink_ref, q_ref, kp_ref, kc_ref, vp_ref, vc_ref, o_ref):
    j = pl.program_id(1)
    kk = jnp.concatenate([kp_ref[...], kc_ref[...]], axis=0).astype(BF16)
    vt = jnp.concatenate([vp_ref[...], vc_ref[...]], axis=0).T.astype(BF16)
    s = lax.broadcasted_iota(jnp.int32, (2 * WINDOW, WINDOW), 0)
    t = lax.broadcasted_iota(jnp.int32, (2 * WINDOW, WINDOW), 1)
    ok = (s > t) & (s <= t + WINDOW) & (s >= jnp.where(j > 0, 0, WINDOW))
    bias = jnp.where(ok, 0.0, -jnp.inf)
    zk = jnp.zeros((2 * WINDOW, HEAD_DIM), BF16)
    zv = jnp.zeros((HEAD_DIM, 2 * WINDOW), BF16)
    for g in range(N_KV_HEADS):
        kg = kk[:, g * HEAD_DIM:(g + 1) * HEAD_DIM]
        vg = vt[g * HEAD_DIM:(g + 1) * HEAD_DIM]
        k2 = jnp.concatenate([jnp.concatenate([kg, zk], axis=1), jnp.concatenate([zk, kg], axis=1)], axis=0)
        v2 = jnp.concatenate([jnp.concatenate([vg, zv], axis=1), jnp.concatenate([zv, vg], axis=1)], axis=0)
        for hp in range(GQA_GROUP // 2):
            h0 = g * GQA_GROUP + 2 * hp
            lanes = slice(h0 * HEAD_DIM, (h0 + 2) * HEAD_DIM)
            st = lax.dot_general(k2, q_ref[:, lanes], (((1,), (1,)), ((), ())), preferred_element_type=F32)
            probs = []
            for i in range(2):
                sh = st[i * 2 * WINDOW:(i + 1) * 2 * WINDOW] + bias
                sink = sink_ref[h0 + i]
                m = jnp.maximum(jnp.max(sh, axis=0, keepdims=True), sink)
                p = jnp.exp(sh - m)
                denom = jnp.sum(p, axis=0, keepdims=True) + jnp.exp(sink - m)
                probs.append((p * (1.0 / denom)).astype(BF16))
            ot = jnp.dot(v2, jnp.concatenate(probs, axis=0), preferred_element_type=F32)
            o_ref[:, lanes] = ot.T.astype(o_ref.dtype)


def _attn_prompt(q, k, v, sinks, batch, seq):
    nb = seq // WINDOW
    rows = batch * seq
    cur = lambda b, j: (b * nb + j, 0)
    prev = lambda b, j: (b * nb + jnp.maximum(j - 1, 0), 0)
    kv_block = (WINDOW, KV_WIDTH)
    return pl.pallas_call(
        _attn_prompt_kernel,
        grid=(batch, nb),
        in_specs=[
            pl.BlockSpec(memory_space=pltpu.SMEM),
            pl.BlockSpec((WINDOW, D_MODEL), cur),
            pl.BlockSpec(kv_block, prev), pl.BlockSpec(kv_block, cur),
            pl.BlockSpec(kv_block, prev), pl.BlockSpec(kv_block, cur),
        ],
        out_specs=pl.BlockSpec((WINDOW, D_MODEL), cur),
        out_shape=jax.ShapeDtypeStruct((rows, D_MODEL), BF16),
        compiler_params=_compiler_params(("parallel", "parallel"), 32),
        name="attn_prompt",
    )(sinks, q, k, k, v, v)


def _attn_sample_kernel(sink_ref, q_ref, k_ref, v_ref, ck_ref, cv_ref, o_ref, nk_ref, nv_ref, *, steps):
    kc, vc = ck_ref[0], cv_ref[0]
    kn, vn = k_ref[...], v_ref[...]
    nk_ref[0, :WINDOW - steps] = kc[steps:]
    nk_ref[0, WINDOW - steps:] = kn
    nv_ref[0, :WINDOW - steps] = vc[steps:]
    nv_ref[0, WINDOW - steps:] = vn
    zpad = jnp.zeros((WINDOW - steps, KV_WIDTH), F32)
    kk = jnp.concatenate([kc, kn, zpad], axis=0).astype(BF16)
    vv = jnp.concatenate([vc, vn, zpad], axis=0).astype(BF16)
    rows = GQA_GROUP * steps
    assert steps & (steps - 1) == 0
    t = lax.broadcasted_iota(jnp.int32, (rows, 2 * WINDOW), 0) & (steps - 1)
    s = lax.broadcasted_iota(jnp.int32, (rows, 2 * WINDOW), 1)
    ok = ((s < WINDOW) & (s > t)) | ((s >= WINDOW) & (s - WINDOW <= t))
    q = q_ref[...]
    for g in range(N_KV_HEADS):
        heads = range(g * GQA_GROUP, (g + 1) * GQA_GROUP)
        qg = jnp.concatenate([q[:, h * HEAD_DIM:(h + 1) * HEAD_DIM] for h in heads], axis=0).astype(BF16)
        sc = lax.dot_general(qg, kk[:, g * HEAD_DIM:(g + 1) * HEAD_DIM], (((1,), (1,)), ((), ())),
                             preferred_element_type=F32)
        p, denom = _softmax_with_sink(jnp.where(ok, sc, -jnp.inf), sink_ref[g])
        og = jnp.dot(p.astype(BF16), vv[:, g * HEAD_DIM:(g + 1) * HEAD_DIM], preferred_element_type=F32) / denom
        for i, h in enumerate(heads):
            o_ref[:, h * HEAD_DIM:(h + 1) * HEAD_DIM] = og[i * steps:(i + 1) * steps]


def _attn_sample(q, k, v, cache_k, cache_v, layer, sinks, batch, steps):
    rows = batch * steps
    sink_rows = jnp.repeat(sinks.reshape(N_KV_HEADS, GQA_GROUP), steps, axis=1)[..., None]
    row = lambda b: (b, 0)
    win = lambda b: (b, 0, 0)
    cache_block = (1, WINDOW, KV_WIDTH)
    return pl.pallas_call(
        functools.partial(_attn_sample_kernel, steps=steps),
        grid=(batch,),
        in_specs=[
            pl.BlockSpec((N_KV_HEADS, GQA_GROUP * steps, 1), lambda b: (0, 0, 0)),
            pl.BlockSpec((steps, D_MODEL), row),
            pl.BlockSpec((steps, KV_WIDTH), row), pl.BlockSpec((steps, KV_WIDTH), row),
            pl.BlockSpec((None,) + cache_block, lambda b: (layer, b, 0, 0)),
            pl.BlockSpec((None,) + cache_block, lambda b: (layer, b, 0, 0)),
        ],
        out_specs=[
            pl.BlockSpec((steps, D_MODEL), row),
            pl.BlockSpec(cache_block, win), pl.BlockSpec(cache_block, win),
        ],
        out_shape=[
            jax.ShapeDtypeStruct((rows, D_MODEL), F32),
            jax.ShapeDtypeStruct((batch, WINDOW, KV_WIDTH), F32),
            jax.ShapeDtypeStruct((batch, WINDOW, KV_WIDTH), F32),
        ],
        compiler_params=_compiler_params(("parallel",), 32),
        name="attn_sample",
    )(sink_rows, q, k, v, cache_k, cache_v)


SUBLANES = 8


def _levels_from_matrix(levels):
    return [b for b in levels if 2 < b <= SUBLANES]


def _scan_matrix(seq_rows, levels, with_suffix):
    n = SCAN_ROWS
    t = np.arange(n)[:, None]
    r = np.arange(n)[None, :]
    same = (t // seq_rows) == (r // seq_rows)
    mats = [same & (r <= t)] + ([same & (r > t)] if with_suffix else [])
    for b in _levels_from_matrix(levels):
        mid = (t // b) * b + b // 2
        upper = (t & (b // 2)) != 0
        mats.append(np.where(upper, (r >= mid) & (r <= t), (r > t) & (r < mid)))
    return jnp.asarray(np.concatenate(mats, axis=0), dtype=BF16)


def _scan_kernel(h_ref, wq_ref, wf_ref, wi_ref, wg_ref, lbl_ref, og_ref, mat_ref, *rest, layer, n_seq, levels):
    s0_ref = rest[0] if len(rest) == 4 else None
    a_ref, st_ref, g_scr = rest[-3:]
    n = SCAN_ROWS
    width = SCAN_HEADS * LANES
    seq_rows = n // n_seq
    chunk = pl.program_id(2)

    @pl.when(chunk == 0)
    def _():
        for s in range(n_seq):
            for h in range(SCAN_HEADS):
                st_ref[s, h] = jnp.zeros((REC_DV, REC_DK), F32) if s0_ref is None else s0_ref[s, h].T

    logits = lbl_ref[...]
    pexp = jnp.exp(logits - jnp.max(logits, axis=0, keepdims=True))
    psum = jnp.sum(pexp, axis=0, keepdims=True)
    lb = jnp.zeros_like(psum)
    for i in range(1, layer + 1):
        lb = lb + pexp[i:i + 1] / psum

    hb = h_ref[...]
    qs = _silu(jnp.dot(hb, wq_ref[...], preferred_element_type=F32))
    fz = jnp.dot(hb, wf_ref[...], preferred_element_type=F32)
    e = jnp.exp(-jnp.abs(fz))
    r = 1.0 / (1.0 + e)
    pos = fz >= 0
    f = lb + (1.0 - lb) * jnp.where(pos, r, e * r)
    logf = jnp.log(f)
    kin = (1.0 - lb) * jnp.where(pos, e * r, r)
    val = jnp.dot(hb, wi_ref[...], preferred_element_type=F32)
    gate = _silu(jnp.dot(hb, wg_ref[...], preferred_element_type=F32))

    mat = mat_ref[...]
    sums = None
    for part in _split_bf16(logf, 2):
        d = jnp.dot(mat, part, preferred_element_type=F32)
        sums = d if sums is None else sums + d
    gcum = sums[:n]
    g_scr[...] = gcum
    if n_seq == 1:
        gtail = g_scr[n - 1:n, :] - gcum
        block = 1
    else:
        gtail = sums[n:2 * n]
        block = 2
    level_sums = {b: sums[(block + i) * n:(block + i + 1) * n] for i, b in enumerate(_levels_from_matrix(levels))}

    t_row = lax.broadcasted_iota(jnp.int32, (n, width), 0)
    q_lv, k_lv = [], []
    for b in levels:
        half = b // 2
        if b > SUBLANES:
            zero = jnp.zeros((half, width), F32)
            qparts, kparts = [], []
            for lo in range(0, n, b):
                mid, hi = lo + half, lo + b
                ref = g_scr[mid - 1:mid, :]
                kparts += [kin[lo:mid] * jnp.exp(ref - gcum[lo:mid]), zero]
                qparts += [zero, qs[mid:hi] * jnp.exp(gcum[mid:hi] - ref)]
            qt, kt = jnp.concatenate(qparts, axis=0), jnp.concatenate(kparts, axis=0)
        else:
            upper = (t_row & half) != 0
            if b == 2:
                qt, kt = jnp.where(upper, qs * f, 0.0), jnp.where(upper, 0.0, kin)
            else:
                decay = jnp.exp(level_sums[b])
                qt, kt = jnp.where(upper, qs * decay, 0.0), jnp.where(upper, 0.0, kin * decay)
        q_lv.append(qt.astype(BF16))
        k_lv.append(kt.astype(BF16))
    qe_all = qs * jnp.exp(gcum)
    kt_all = kin * jnp.exp(gtail)

    t_lane = lax.broadcasted_iota(jnp.int32, (n, LANES), 0)
    tt = lax.broadcasted_iota(jnp.int32, (n, n), 0)
    ss = lax.broadcasted_iota(jnp.int32, (n, n), 1)
    contract_lanes = (((1,), (1,)), ((), ()))
    contract_rows = (((0,), (0,)), ((), ()))

    for h in range(SCAN_HEADS):
        sl = slice(h * LANES, (h + 1) * LANES)
        vh = val[:, sl]
        vb = vh.astype(BF16)
        a = None
        for l, b in enumerate(levels):
            al = lax.dot_general(q_lv[l][:, sl], k_lv[l][:, sl], contract_lanes, preferred_element_type=F32)
            if b == n:
                a = al
            else:
                a = jnp.where((tt ^ ss) < b, al, 0.0 if a is None else a)
        o = jnp.dot(a.astype(BF16), vb, preferred_element_type=F32)
        o = o + jnp.sum(qs[:, sl] * kin[:, sl], axis=-1, keepdims=True) * vh
        qe, kt = qe_all[:, sl], kt_all[:, sl]
        for s in range(n_seq):
            last = (s + 1) * seq_rows - 1
            if n_seq == 1:
                qe_s, kt_s = qe, kt
            else:
                inside = (t_lane >= s * seq_rows) & (t_lane <= last)
                qe_s, kt_s = jnp.where(inside, qe, 0.0), jnp.where(inside, kt, 0.0)
            st = st_ref[s, h]
            o = o + lax.dot_general(qe_s.astype(BF16), st.astype(BF16), contract_lanes,
                                    preferred_element_type=F32)
            st_ref[s, h] = st * jnp.exp(g_scr[last:last + 1, sl]) + lax.dot_general(
                vb, kt_s.astype(BF16), contract_rows, preferred_element_type=F32)
        rs = lax.rsqrt(jnp.mean(o * o, axis=-1, keepdims=True) + EPS)
        a_ref[:, sl] = (o * rs * og_ref[:, sl] * gate[:, sl]).astype(a_ref.dtype)

    @pl.when(chunk == pl.num_programs(2) - 1)
    def _():
        for s in range(n_seq):
            for h in range(SCAN_HEADS):
                st_ref[s, h] = st_ref[s, h].T


def _hgrn_scan(h, w_in, lb_logits, o_gain, s0, layer, n_state, seq_rows):
    rows = h.shape[0]
    n = SCAN_ROWS
    n_seq = max(n // seq_rows, 1)
    chunks = max(seq_rows // n, 1)
    levels = [min(seq_rows, n) >> l for l in range(int(np.log2(min(seq_rows, n))))]
    mat = _scan_matrix(min(seq_rows, n), levels, with_suffix=n_seq > 1)
    hg = REC_HEADS // SCAN_HEADS
    wb = SCAN_HEADS * LANES
    sect = lambda k: pl.BlockSpec((None, D_MODEL, wb), lambda g, b, c: (layer, 0, k * hg + g))
    head_vec = lambda r: pl.BlockSpec((r, wb), lambda g, b, c: (0, g))
    state_block = (n_seq, SCAN_HEADS, REC_DK, REC_DV)
    state = pl.BlockSpec(state_block, lambda g, b, c: (b, g, 0, 0))
    state_in = [] if s0 is None else [pl.BlockSpec((None,) + state_block, lambda g, b, c: (layer, b, g, 0, 0))]
    return pl.pallas_call(
        functools.partial(_scan_kernel, layer=layer, n_seq=n_seq, levels=levels),
        grid=(hg, n_state // n_seq, chunks),
        in_specs=[
            pl.BlockSpec((n, D_MODEL), lambda g, b, c: (b * chunks + c, 0)),
            sect(0), sect(1), sect(2), sect(3),
            head_vec(lb_logits.shape[0]), head_vec(1),
            pl.BlockSpec(mat.shape, lambda g, b, c: (0, 0)),
        ] + state_in,
        out_specs=[pl.BlockSpec((n, wb), lambda g, b, c: (b * chunks + c, g)), state],
        out_shape=[
            jax.ShapeDtypeStruct((rows, D_MODEL), BF16),
            jax.ShapeDtypeStruct((n_state, REC_HEADS, REC_DK, REC_DV), F32),
        ],
        scratch_shapes=[pltpu.VMEM((n, wb), F32)],
        compiler_params=_compiler_params(("parallel", "parallel", "arbitrary"), 48),
        name="hgrn_scan",
    )(h, w_in, w_in, w_in, w_in, lb_logits, o_gain.reshape(1, D_MODEL), mat, *([] if s0 is None else [s0]))


def _outproj_kernel(a_ref, x_ref, g_ref, w_ref, o_ref):
    y = jnp.dot(a_ref[...].astype(BF16), w_ref[...], preferred_element_type=F32)
    o_ref[...] = x_ref[...] + g_ref[0] * y


def _outproj(a, x, mod, w, layer, tm):
    rows = x.shape[0]
    row = lambda i: (i, 0)
    return pl.pallas_call(
        _outproj_kernel,
        grid=(rows // tm,),
        in_specs=[
            pl.BlockSpec((tm, D_MODEL), row),
            pl.BlockSpec((tm, D_MODEL), row),
            mod.spec(tm),
            pl.BlockSpec((None, D_MODEL, D_MODEL), lambda i: (layer, 0, 0)),
        ],
        out_specs=pl.BlockSpec((tm, D_MODEL), row),
        out_shape=jax.ShapeDtypeStruct((rows, D_MODEL), F32),
        compiler_params=_compiler_params(("parallel",), 48),
        name="outproj",
    )(a, x, mod.parts[2], w)


def _mlp_kernel(x_ref, gain_ref, sc_ref, sh_ref, xo_ref, g_ref, wu_ref, wd_ref, o_ref, h_scr, u_scr, *, n_up, tf):
    j = pl.program_id(1)

    @pl.when(j == 0)
    def _():
        h_scr[...] = _modnorm(x_ref[...], gain_ref[...], sc_ref[0], sh_ref[0]).astype(BF16)

    @pl.when(j < n_up)
    def _():
        u = jnp.maximum(jnp.dot(h_scr[...], wu_ref[...], preferred_element_type=F32), 0.0)
        u2 = (u * u).astype(BF16)
        for c in range(n_up):
            @pl.when(j == c)
            def _():
                u_scr[:, c * tf:(c + 1) * tf] = u2

    @pl.when(j >= n_up)
    def _():
        d = jnp.dot(u_scr[...], wd_ref[...], preferred_element_type=F32)
        o_ref[...] = xo_ref[...] + g_ref[0] * d


MLP_UP_BLOCK = 1024
MLP_DOWN_BLOCK = 512


def _column_blocks(w, width):
    layers, k, n = w.shape
    return w.astype(BF16).reshape(layers, k, n // width, width).transpose(0, 2, 1, 3)


def _mlp(x, gain, mod, w_up, w_down, layer, tm):
    rows = x.shape[0]
    n_up, tf = w_up.shape[1], w_up.shape[3]
    n_down, tn = w_down.shape[1], w_down.shape[3]
    row = lambda i, j: (i, 0)
    out_col = lambda i, j: jnp.maximum(j - n_up, 0)
    return pl.pallas_call(
        functools.partial(_mlp_kernel, n_up=n_up, tf=tf),
        grid=(rows // tm, n_up + n_down),
        in_specs=[
            pl.BlockSpec((tm, D_MODEL), row),
            pl.BlockSpec((1, D_MODEL), lambda i, j: (0, 0)),
            mod.spec(tm), mod.spec(tm),
            pl.BlockSpec((tm, tn), lambda i, j: (i, out_col(i, j))),
            mod.spec(tm, tn, out_col),
            pl.BlockSpec((None, None, D_MODEL, tf), lambda i, j: (layer, jnp.minimum(j, n_up - 1), 0, 0)),
            pl.BlockSpec((None, None, D_FF, tn), lambda i, j: (layer, out_col(i, j), 0, 0)),
        ],
        out_specs=pl.BlockSpec((tm, tn), lambda i, j: (i, out_col(i, j))),
        out_shape=jax.ShapeDtypeStruct((rows, D_MODEL), F32),
        scratch_shapes=[pltpu.VMEM((tm, D_MODEL), BF16), pltpu.VMEM((tm, D_FF), BF16)],
        compiler_params=_compiler_params(("parallel", "arbitrary"), 56),
        name="mlp",
    )(x, gain, mod.parts[4], mod.parts[3], x, mod.parts[5], w_up, w_down)


def _row_tile(rows):
    return min(rows, 512)


def _attn_layer(x, mod, gain_mix, w_qkv, q_gain, k_gain, sinks, w_o, layer, batch, seq, cache=None):
    rows = batch * seq
    tm = _row_tile(rows)
    if cache is None:
        tables = _rope_tables(jnp.arange(seq, dtype=jnp.int32))
        q, k, v = _qkv(x, gain_mix, mod, w_qkv, layer, q_gain, k_gain, tables, seq, tm, BF16)
        o = _attn_prompt(q, k, v, sinks, batch, seq)
        k_win = k.reshape(batch, seq, N_KV_HEADS, HEAD_DIM)[:, seq - WINDOW:]
        v_win = v.reshape(batch, seq, N_KV_HEADS, HEAD_DIM)[:, seq - WINDOW:]
    else:
        pos = PAST_LEN + jnp.arange(seq, dtype=jnp.int32)
        tables = tuple(jnp.tile(t, (batch, 1)) for t in _rope_tables(pos))
        q, k, v = _qkv(x, gain_mix, mod, w_qkv, layer, q_gain, k_gain, tables, rows, tm, F32)
        ck = cache[0].reshape(-1, batch, WINDOW, KV_WIDTH)
        cv = cache[1].reshape(-1, batch, WINDOW, KV_WIDTH)
        o, k_win, v_win = _attn_sample(q, k, v, ck, cv, layer, sinks, batch, seq)
        k_win = k_win.reshape(batch, WINDOW, N_KV_HEADS, HEAD_DIM)
        v_win = v_win.reshape(batch, WINDOW, N_KV_HEADS, HEAD_DIM)
    return _outproj(o, x, mod, w_o, layer, tm), k_win, v_win


def _hgrn_layer(x, mod, gain_mix, w_in, lb_logits, o_gain, w_o, s0, layer, batch, seq):
    rows = batch * seq
    tm = _row_tile(rows)
    h = _norm_rows(x, gain_mix, mod, tm)
    a, state = _hgrn_scan(h, w_in, lb_logits, o_gain, s0, layer, batch, seq)
    return _outproj(a, x, mod, w_o, layer, tm), state


def kernel(x_prompt, x_sample, cache_k_win, cache_v_win, state_hgrn, c_prompt, c_sample, norm_gain, w_ada, b_ada,
           attn_w_qkv, attn_q_gain, attn_k_gain, attn_sinks, attn_w_o, rec_w_in, rec_lb_logits, rec_o_gain,
           rec_w_o, mlp_w_up, mlp_w_down):
    bp, lp, _ = x_prompt.shape
    bs, ls, _ = x_sample.shape
    n_c = bp + bs
    c_all = jnp.concatenate([c_prompt, c_sample, jnp.zeros((-n_c % 16, D_MODEL), F32)], axis=0)
    mods = _ada_mods(c_all, w_ada, b_ada)

    xp = x_prompt.reshape(bp * lp, D_MODEL)
    xs = x_sample.reshape(bs * ls, D_MODEL)
    lb_logits = rec_lb_logits.astype(F32)
    w_qkv, w_ao = attn_w_qkv.astype(BF16), attn_w_o.astype(BF16)
    w_in, w_ro = rec_w_in.astype(BF16), rec_w_o.astype(BF16)
    w_up, w_down = _column_blocks(mlp_w_up, MLP_UP_BLOCK), _column_blocks(mlp_w_down, MLP_DOWN_BLOCK)
    kwp, vwp, kws, vws, sp, ss = [], [], [], [], [], []
    for i in range(DEPTH):
        j = i // N_MIXERS
        mod_p = _Mod(mods[i, :bp], lp, per_row=False)
        mod_s = _Mod(mods[i, bp:n_c], ls, per_row=True)
        gain_mix = norm_gain[i, 0].reshape(1, D_MODEL)
        gain_mlp = norm_gain[i, 1].reshape(1, D_MODEL)
        if i % N_MIXERS == 0:
            aw = (w_qkv, attn_q_gain[j], attn_k_gain[j], attn_sinks[j], w_ao, j)
            xp, kp, vp = _attn_layer(xp, mod_p, gain_mix, *aw, bp, lp)
            xs, kn, vn = _attn_layer(xs, mod_s, gain_mix, *aw, bs, ls, cache=(cache_k_win, cache_v_win))
            kwp.append(kp); vwp.append(vp); kws.append(kn); vws.append(vn)
        else:
            rw = (w_in, lb_logits, rec_o_gain[j], w_ro)
            xp, s_p = _hgrn_layer(xp, mod_p, gain_mix, *rw, None, j, bp, lp)
            xs, s_s = _hgrn_layer(xs, mod_s, gain_mix, *rw, state_hgrn, j, bs, ls)
            sp.append(s_p); ss.append(s_s)
        xp = _mlp(xp, gain_mlp, mod_p, w_up, w_down, i, _row_tile(bp * lp))
        xs = _mlp(xs, gain_mlp, mod_s, w_up, w_down, i, _row_tile(bs * ls))
    return (xp.reshape(bp, lp, D_MODEL), xs.reshape(bs, ls, D_MODEL),
            jnp.stack(kwp), jnp.stack(vwp), jnp.stack(kws), jnp.stack(vws), jnp.stack(sp), jnp.stack(ss))
```

```python
import functools

import numpy as np
import jax
import jax.numpy as jnp
from jax import lax
from jax.experimental import pallas as pl
from jax.experimental.pallas import tpu as pltpu

F32 = jnp.float32
BF16 = jnp.bfloat16

D_MODEL = 2048
DEPTH = 4
N_MIXERS = 2
PAST_LEN = 16384
HEAD_DIM = 64
N_Q_HEADS = D_MODEL // HEAD_DIM
N_KV_HEADS = N_Q_HEADS // 8
GQA_GROUP = N_Q_HEADS // N_KV_HEADS
KV_WIDTH = N_KV_HEADS * HEAD_DIM
WINDOW = 128
ROT_DIM = HEAD_DIM // 4
ROPE_THETA = 500000.0
REC_HEADS = 16
REC_DK = 128
REC_DV = D_MODEL // REC_HEADS
D_FF = 4 * D_MODEL
EPS = 1e-6

LANES = 128
MIB = 1024 * 1024
SCAN_ROWS = 128
SCAN_HEADS = 8
SCAN_STATES = 64


def _compiler_params(semantics, vmem_mib):
    return pltpu.CompilerParams(dimension_semantics=semantics, vmem_limit_bytes=vmem_mib * MIB)


def _sigmoid(x):
    e = jnp.exp(-jnp.abs(x))
    r = 1.0 / (1.0 + e)
    return jnp.where(x >= 0, r, e * r)


def _silu(x):
    return x * _sigmoid(x)


def _modnorm(x, gain, scale, shift):
    var = jnp.mean(x * x, axis=-1, keepdims=True)
    y = x * lax.rsqrt(var + EPS) * gain
    return y * (1.0 + scale) + shift


def _split_bf16(x, parts):
    out = []
    for _ in range(parts - 1):
        hi = x.astype(BF16)
        out.append(hi)
        x = x - hi.astype(F32)
    out.append(x.astype(BF16))
    return out


class _Mod:
    def __init__(self, mod, rows_per_group, per_row):
        self.rows_per_group = rows_per_group
        self.per_row = per_row
        parts = jnp.split(mod, 6, axis=-1)
        if per_row:
            self.parts = [jnp.repeat(p, rows_per_group, axis=0)[None] for p in parts]
        else:
            self.parts = [p[:, None, :] for p in parts]

    def spec(self, tm, tn=D_MODEL, col=lambda *_: 0):
        if self.per_row:
            return pl.BlockSpec((1, tm, tn), lambda i, *rest: (0, i, col(i, *rest)))
        assert self.rows_per_group % tm == 0
        per = self.rows_per_group // tm
        return pl.BlockSpec((1, 1, tn), lambda i, *rest: (i // per, 0, col(i, *rest)))


def _ada_kernel(c_ref, w_ref, b_ref, o_ref):
    s = _silu(c_ref[...]).astype(BF16)
    o_ref[0] = jnp.dot(s, w_ref[0].astype(BF16), preferred_element_type=F32) + b_ref[0]


def _ada_mods(c_all, w_ada, b_ada):
    rows = c_all.shape[0]
    tn = 1024
    n = 6 * D_MODEL
    return pl.pallas_call(
        _ada_kernel,
        grid=(DEPTH, n // tn),
        in_specs=[
            pl.BlockSpec((rows, D_MODEL), lambda l, j: (0, 0)),
            pl.BlockSpec((1, D_MODEL, tn), lambda l, j: (l, 0, j)),
            pl.BlockSpec((1, 1, tn), lambda l, j: (l, 0, j)),
        ],
        out_specs=pl.BlockSpec((1, rows, tn), lambda l, j: (l, 0, j)),
        out_shape=jax.ShapeDtypeStruct((DEPTH, rows, n), F32),
        compiler_params=_compiler_params(("parallel", "parallel"), 40),
        name="ada_mods",
    )(c_all, w_ada, b_ada.reshape(DEPTH, 1, n))


def _norm_rows_kernel(x_ref, gain_ref, sc_ref, sh_ref, o_ref):
    o_ref[...] = _modnorm(x_ref[...], gain_ref[...], sc_ref[0], sh_ref[0]).astype(o_ref.dtype)


def _norm_rows(x, gain, mod, tm):
    rows = x.shape[0]
    return pl.pallas_call(
        _norm_rows_kernel,
        grid=(rows // tm,),
        in_specs=[
            pl.BlockSpec((tm, D_MODEL), lambda i: (i, 0)),
            pl.BlockSpec((1, D_MODEL), lambda i: (0, 0)),
            mod.spec(tm),
            mod.spec(tm),
        ],
        out_specs=pl.BlockSpec((tm, D_MODEL), lambda i: (i, 0)),
        out_shape=jax.ShapeDtypeStruct((rows, D_MODEL), BF16),
        compiler_params=_compiler_params(("parallel",), 32),
        name="norm_rows",
    )(x, gain, mod.parts[1], mod.parts[0])


def _qkv_kernel(x_ref, gain_ref, sc_ref, sh_ref, w_ref, qg_ref, kg_ref, cos_ref, s1_ref, s2_ref,
                q_ref, k_ref, v_ref):
    h = _modnorm(x_ref[...], gain_ref[...], sc_ref[0], sh_ref[0]).astype(BF16)
    acc = jnp.dot(h, w_ref[...], preferred_element_type=F32)
    cos, s1, s2 = cos_ref[...], s1_ref[...], s2_ref[...]
    qk_width = D_MODEL + KV_WIDTH
    first_head = lax.broadcasted_iota(jnp.int32, (x_ref.shape[0], LANES), 1) < HEAD_DIM
    for c in range(qk_width // LANES):
        xc = acc[:, c * LANES:(c + 1) * LANES]
        x2 = xc * xc
        ssq = jnp.where(first_head,
                        jnp.sum(jnp.where(first_head, x2, 0.0), axis=-1, keepdims=True),
                        jnp.sum(jnp.where(first_head, 0.0, x2), axis=-1, keepdims=True))
        is_q = c * LANES < D_MODEL
        y = xc * lax.rsqrt(ssq * (1.0 / HEAD_DIM) + EPS) * (qg_ref[...] if is_q else kg_ref[...])
        r = y * cos + pltpu.roll(y, LANES - ROT_DIM // 2, 1) * s1 + pltpu.roll(y, ROT_DIM // 2, 1) * s2
        if is_q:
            q_ref[:, c * LANES:(c + 1) * LANES] = (r * (HEAD_DIM ** -0.5)).astype(q_ref.dtype)
        else:
            k_ref[:, c * LANES - D_MODEL:(c + 1) * LANES - D_MODEL] = r
    v_ref[...] = acc[:, qk_width:]


def _rope_tables(pos):
    half = ROT_DIM // 2
    inv = ROPE_THETA ** (-jnp.arange(half, dtype=F32) / half)
    ang = pos.astype(F32)[:, None] * inv[None, :]
    cos, sin = jnp.cos(ang), jnp.sin(ang)
    n = pos.shape[0]
    pad = jnp.zeros((n, HEAD_DIM - ROT_DIM), F32)
    zero = jnp.zeros((n, half), F32)
    c = jnp.concatenate([cos, cos, pad + 1.0], axis=1)
    s1 = jnp.concatenate([-sin, zero, pad], axis=1)
    s2 = jnp.concatenate([zero, sin, pad], axis=1)
    return tuple(jnp.tile(t, (1, LANES // HEAD_DIM)) for t in (c, s1, s2))


def _qkv(x, gain, mod, w, layer, q_gain, k_gain, tables, table_rows, tm, q_dtype):
    rows = x.shape[0]
    n = w.shape[-1]
    per = table_rows // tm
    tspec = pl.BlockSpec((tm, LANES), lambda i: (i % per, 0))
    vspec = pl.BlockSpec((1, LANES), lambda i: (0, 0))
    reps = LANES // HEAD_DIM
    return pl.pallas_call(
        _qkv_kernel,
        grid=(rows // tm,),
        in_specs=[
            pl.BlockSpec((tm, D_MODEL), lambda i: (i, 0)),
            pl.BlockSpec((1, D_MODEL), lambda i: (0, 0)),
            mod.spec(tm),
            mod.spec(tm),
            pl.BlockSpec((None, D_MODEL, n), lambda i: (layer, 0, 0)),
            vspec, vspec, tspec, tspec, tspec,
        ],
        out_specs=[
            pl.BlockSpec((tm, D_MODEL), lambda i: (i, 0)),
            pl.BlockSpec((tm, KV_WIDTH), lambda i: (i, 0)),
            pl.BlockSpec((tm, KV_WIDTH), lambda i: (i, 0)),
        ],
        out_shape=[
            jax.ShapeDtypeStruct((rows, D_MODEL), q_dtype),
            jax.ShapeDtypeStruct((rows, KV_WIDTH), F32),
            jax.ShapeDtypeStruct((rows, KV_WIDTH), F32),
        ],
        compiler_params=_compiler_params(("parallel",), 48),
        name="qkv",
    )(x, gain, mod.parts[1], mod.parts[0], w,
      jnp.tile(q_gain.reshape(1, HEAD_DIM), (1, reps)), jnp.tile(k_gain.reshape(1, HEAD_DIM), (1, reps)),
      *tables)


def _softmax_with_sink(s, sink):
    m = jnp.maximum(jnp.max(s, axis=-1, keepdims=True), sink)
    p = jnp.exp(s - m)
    denom = jnp.sum(p, axis=-1, keepdims=True) + jnp.exp(sink - m)
    return p, denom


def _attn_prompt_kernel(sink_ref, q_ref, kp_ref, kc_ref, vp_ref, vc_ref, o_ref):
    j = pl.program_id(1)
    kk = jnp.concatenate([kp_ref[...], kc_ref[...]], axis=0).astype(BF16)
    vt = jnp.concatenate([vp_ref[...], vc_ref[...]], axis=0).T.astype(BF16)
    s = lax.broadcasted_iota(jnp.int32, (2 * WINDOW, WINDOW), 0)
    t = lax.broadcasted_iota(jnp.int32, (2 * WINDOW, WINDOW), 1)
    ok = (s > t) & (s <= t + WINDOW) & (s >= jnp.where(j > 0, 0, WINDOW))
    bias = jnp.where(ok, 0.0, -jnp.inf)
    zk = jnp.zeros((2 * WINDOW, HEAD_DIM), BF16)
    zv = jnp.zeros((HEAD_DIM, 2 * WINDOW), BF16)
    for g in range(N_KV_HEADS):
        kg = kk[:, g * HEAD_DIM:(g + 1) * HEAD_DIM]
        vg = vt[g * HEAD_DIM:(g + 1) * HEAD_DIM]
        k2 = jnp.concatenate([jnp.concatenate([kg, zk], axis=1), jnp.concatenate([zk, kg], axis=1)], axis=0)
        v2 = jnp.concatenate([jnp.concatenate([vg, zv], axis=1), jnp.concatenate([zv, vg], axis=1)], axis=0)
        for hp in range(GQA_GROUP // 2):
            h0 = g * GQA_GROUP + 2 * hp
            lanes = slice(h0 * HEAD_DIM, (h0 + 2) * HEAD_DIM)
            st = lax.dot_general(k2, q_ref[:, lanes], (((1,), (1,)), ((), ())), preferred_element_type=F32)
            probs = []
            for i in range(2):
                sh = st[i * 2 * WINDOW:(i + 1) * 2 * WINDOW] + bias
                sink = sink_ref[h0 + i]
                m = jnp.maximum(jnp.max(sh, axis=0, keepdims=True), sink)
                p = jnp.exp(sh - m)
                denom = jnp.sum(p, axis=0, keepdims=True) + jnp.exp(sink - m)
                probs.append((p * (1.0 / denom)).astype(BF16))
            ot = jnp.dot(v2, jnp.concatenate(probs, axis=0), preferred_element_type=F32)
            o_ref[:, lanes] = ot.T.astype(o_ref.dtype)


def _attn_prompt(q, k, v, sinks, batch, seq):
    nb = seq // WINDOW
    rows = batch * seq
    cur = lambda b, j: (b * nb + j, 0)
    prev = lambda b, j: (b * nb + jnp.maximum(j - 1, 0), 0)
    kv_block = (WINDOW, KV_WIDTH)
    return pl.pallas_call(
        _attn_prompt_kernel,
        grid=(batch, nb),
        in_specs=[
            pl.BlockSpec(memory_space=pltpu.SMEM),
            pl.BlockSpec((WINDOW, D_MODEL), cur),
            pl.BlockSpec(kv_block, prev), pl.BlockSpec(kv_block, cur),
            pl.BlockSpec(kv_block, prev), pl.BlockSpec(kv_block, cur),
        ],
        out_specs=pl.BlockSpec((WINDOW, D_MODEL), cur),
        out_shape=jax.ShapeDtypeStruct((rows, D_MODEL), BF16),
        compiler_params=_compiler_params(("parallel", "parallel"), 32),
        name="attn_prompt",
    )(sinks, q, k, k, v, v)


def _attn_sample_kernel(sink_ref, q_ref, k_ref, v_ref, ck_ref, cv_ref, o_ref, nk_ref, nv_ref, *, steps):
    kc, vc = ck_ref[0], cv_ref[0]
    kn, vn = k_ref[...], v_ref[...]
    nk_ref[0, :WINDOW - steps] = kc[steps:]
    nk_ref[0, WINDOW - steps:] = kn
    nv_ref[0, :WINDOW - steps] = vc[steps:]
    nv_ref[0, WINDOW - steps:] = vn
    zpad = jnp.zeros((WINDOW - steps, KV_WIDTH), F32)
    kk = jnp.concatenate([kc, kn, zpad], axis=0).astype(BF16)
    vv = jnp.concatenate([vc, vn, zpad], axis=0).astype(BF16)
    rows = GQA_GROUP * steps
    assert steps & (steps - 1) == 0
    t = lax.broadcasted_iota(jnp.int32, (rows, 2 * WINDOW), 0) & (steps - 1)
    s = lax.broadcasted_iota(jnp.int32, (rows, 2 * WINDOW), 1)
    ok = ((s < WINDOW) & (s > t)) | ((s >= WINDOW) & (s - WINDOW <= t))
    q = q_ref[...]
    for g in range(N_KV_HEADS):
        heads = range(g * GQA_GROUP, (g + 1) * GQA_GROUP)
        qg = jnp.concatenate([q[:, h * HEAD_DIM:(h + 1) * HEAD_DIM] for h in heads], axis=0).astype(BF16)
        sc = lax.dot_general(qg, kk[:, g * HEAD_DIM:(g + 1) * HEAD_DIM], (((1,), (1,)), ((), ())),
                             preferred_element_type=F32)
        p, denom = _softmax_with_sink(jnp.where(ok, sc, -jnp.inf), sink_ref[g])
        og = jnp.dot(p.astype(BF16), vv[:, g * HEAD_DIM:(g + 1) * HEAD_DIM], preferred_element_type=F32) / denom
        for i, h in enumerate(heads):
            o_ref[:, h * HEAD_DIM:(h + 1) * HEAD_DIM] = og[i * steps:(i + 1) * steps]


def _attn_sample(q, k, v, cache_k, cache_v, layer, sinks, batch, steps):
    rows = batch * steps
    sink_rows = jnp.repeat(sinks.reshape(N_KV_HEADS, GQA_GROUP), steps, axis=1)[..., None]
    row = lambda b: (b, 0)
    win = lambda b: (b, 0, 0)
    cache_block = (1, WINDOW, KV_WIDTH)
    return pl.pallas_call(
        functools.partial(_attn_sample_kernel, steps=steps),
        grid=(batch,),
        in_specs=[
            pl.BlockSpec((N_KV_HEADS, GQA_GROUP * steps, 1), lambda b: (0, 0, 0)),
            pl.BlockSpec((steps, D_MODEL), row),
            pl.BlockSpec((steps, KV_WIDTH), row), pl.BlockSpec((steps, KV_WIDTH), row),
            pl.BlockSpec((None,) + cache_block, lambda b: (layer, b, 0, 0)),
            pl.BlockSpec((None,) + cache_block, lambda b: (layer, b, 0, 0)),
        ],
        out_specs=[
            pl.BlockSpec((steps, D_MODEL), row),
            pl.BlockSpec(cache_block, win), pl.BlockSpec(cache_block, win),
        ],
        out_shape=[
            jax.ShapeDtypeStruct((rows, D_MODEL), F32),
            jax.ShapeDtypeStruct((batch, WINDOW, KV_WIDTH), F32),
            jax.ShapeDtypeStruct((batch, WINDOW, KV_WIDTH), F32),
        ],
        compiler_params=_compiler_params(("parallel",), 32),
        name="attn_sample",
    )(sink_rows, q, k, v, cache_k, cache_v)


SUBLANES = 8


def _levels_from_matrix(levels):
    return [b for b in levels if 2 < b <= SUBLANES]


def _scan_matrix(seq_rows, levels, with_suffix):
    n = SCAN_ROWS
    t = np.arange(n)[:, None]
    r = np.arange(n)[None, :]
    same = (t // seq_rows) == (r // seq_rows)
    mats = [same & (r <= t)] + ([same & (r > t)] if with_suffix else [])
    for b in _levels_from_matrix(levels):
        mid = (t // b) * b + b // 2
        upper = (t & (b // 2)) != 0
        mats.append(np.where(upper, (r >= mid) & (r <= t), (r > t) & (r < mid)))
    return jnp.asarray(np.concatenate(mats, axis=0), dtype=BF16)


def _scan_kernel(h_ref, wq_ref, wf_ref, wi_ref, wg_ref, lbl_ref, og_ref, mat_ref, *rest, layer, n_seq, levels):
    s0_ref = rest[0] if len(rest) == 4 else None
    a_ref, st_ref, g_scr = rest[-3:]
    n = SCAN_ROWS
    heads = st_ref.shape[1]
    width = heads * LANES
    seq_rows = n // n_seq
    chunk = pl.program_id(2)

    @pl.when(chunk == 0)
    def _():
        for s in range(n_seq):
            for h in range(heads):
                st_ref[s, h] = jnp.zeros((REC_DV, REC_DK), F32) if s0_ref is None else s0_ref[s, h].T

    logits = lbl_ref[...]
    pexp = jnp.exp(logits - jnp.max(logits, axis=0, keepdims=True))
    psum = jnp.sum(pexp, axis=0, keepdims=True)
    lb = jnp.zeros_like(psum)
    for i in range(1, layer + 1):
        lb = lb + pexp[i:i + 1] / psum

    hb = h_ref[...]
    qs = _silu(jnp.dot(hb, wq_ref[...], preferred_element_type=F32))
    fz = jnp.dot(hb, wf_ref[...], preferred_element_type=F32)
    e = jnp.exp(-jnp.abs(fz))
    r = 1.0 / (1.0 + e)
    pos = fz >= 0
    f = lb + (1.0 - lb) * jnp.where(pos, r, e * r)
    logf = jnp.log(f)
    kin = (1.0 - lb) * jnp.where(pos, e * r, r)
    val = jnp.dot(hb, wi_ref[...], preferred_element_type=F32)
    gate = _silu(jnp.dot(hb, wg_ref[...], preferred_element_type=F32))

    mat = mat_ref[...]
    sums = None
    for part in _split_bf16(logf, 2):
        d = jnp.dot(mat, part, preferred_element_type=F32)
        sums = d if sums is None else sums + d
    gcum = sums[:n]
    g_scr[...] = gcum
    if n_seq == 1:
        gtail = g_scr[n - 1:n, :] - gcum
        block = 1
    else:
        gtail = sums[n:2 * n]
        block = 2
    level_sums = {b: sums[(block + i) * n:(block + i + 1) * n] for i, b in enumerate(_levels_from_matrix(levels))}

    t_row = lax.broadcasted_iota(jnp.int32, (n, width), 0)
    q_lv, k_lv = [], []
    for b in levels:
        half = b // 2
        if b > SUBLANES:
            zero = jnp.zeros((half, width), F32)
            qparts, kparts = [], []
            for lo in range(0, n, b):
                mid, hi = lo + half, lo + b
                ref = g_scr[mid - 1:mid, :]
                kparts += [kin[lo:mid] * jnp.exp(ref - gcum[lo:mid]), zero]
                qparts += [zero, qs[mid:hi] * jnp.exp(gcum[mid:hi] - ref)]
            qt, kt = jnp.concatenate(qparts, axis=0), jnp.concatenate(kparts, axis=0)
        else:
            upper = (t_row & half) != 0
            if b == 2:
                qt, kt = jnp.where(upper, qs * f, 0.0), jnp.where(upper, 0.0, kin)
            else:
                decay = jnp.exp(level_sums[b])
                qt, kt = jnp.where(upper, qs * decay, 0.0), jnp.where(upper, 0.0, kin * decay)
        q_lv.append(qt.astype(BF16))
        k_lv.append(kt.astype(BF16))
    qe_all = qs * jnp.exp(gcum)
    kt_all = kin * jnp.exp(gtail)

    t_lane = lax.broadcasted_iota(jnp.int32, (n, LANES), 0)
    tt = lax.broadcasted_iota(jnp.int32, (n, n), 0)
    ss = lax.broadcasted_iota(jnp.int32, (n, n), 1)
    contract_lanes = (((1,), (1,)), ((), ()))
    contract_rows = (((0,), (0,)), ((), ()))

    for h in range(heads):
        sl = slice(h * LANES, (h + 1) * LANES)
        vh = val[:, sl]
        vb = vh.astype(BF16)
        a = None
        for l, b in enumerate(levels):
            al = lax.dot_general(q_lv[l][:, sl], k_lv[l][:, sl], contract_lanes, preferred_element_type=F32)
            if b == n:
                a = al
            else:
                a = jnp.where((tt ^ ss) < b, al, 0.0 if a is None else a)
        o = jnp.dot(a.astype(BF16), vb, preferred_element_type=F32)
        o = o + jnp.sum(qs[:, sl] * kin[:, sl], axis=-1, keepdims=True) * vh
        qe, kt = qe_all[:, sl], kt_all[:, sl]
        for s in range(n_seq):
            last = (s + 1) * seq_rows - 1
            if n_seq == 1:
                qe_s, kt_s = qe, kt
            else:
                inside = (t_lane >= s * seq_rows) & (t_lane <= last)
                qe_s, kt_s = jnp.where(inside, qe, 0.0), jnp.where(inside, kt, 0.0)
            st = st_ref[s, h]
            o = o + lax.dot_general(qe_s.astype(BF16), st.astype(BF16), contract_lanes,
                                    preferred_element_type=F32)
            st_ref[s, h] = st * jnp.exp(g_scr[last:last + 1, sl]) + lax.dot_general(
                vb, kt_s.astype(BF16), contract_rows, preferred_element_type=F32)
        rs = lax.rsqrt(jnp.mean(o * o, axis=-1, keepdims=True) + EPS)
        a_ref[:, sl] = (o * rs * og_ref[:, sl] * gate[:, sl]).astype(a_ref.dtype)

    @pl.when(chunk == pl.num_programs(2) - 1)
    def _():
        for s in range(n_seq):
            for h in range(heads):
                st_ref[s, h] = st_ref[s, h].T


def _hgrn_scan(h, w_in, lb_logits, o_gain, s0, layer, n_state, seq_rows):
    rows = h.shape[0]
    n = SCAN_ROWS
    n_seq = max(n // seq_rows, 1)
    chunks = max(seq_rows // n, 1)
    levels = [min(seq_rows, n) >> l for l in range(int(np.log2(min(seq_rows, n))))]
    mat = _scan_matrix(min(seq_rows, n), levels, with_suffix=n_seq > 1)
    heads = min(SCAN_HEADS, max(1, SCAN_STATES // n_seq))
    hg = REC_HEADS // heads
    wb = heads * LANES
    sect = lambda k: pl.BlockSpec((None, D_MODEL, wb), lambda g, b, c: (layer, 0, k * hg + g))
    head_vec = lambda r: pl.BlockSpec((r, wb), lambda g, b, c: (0, g))
    state_block = (n_seq, heads, REC_DK, REC_DV)
    state = pl.BlockSpec(state_block, lambda g, b, c: (b, g, 0, 0))
    state_in = [] if s0 is None else [pl.BlockSpec((None,) + state_block, lambda g, b, c: (layer, b, g, 0, 0))]
    return pl.pallas_call(
        functools.partial(_scan_kernel, layer=layer, n_seq=n_seq, levels=levels),
        grid=(hg, n_state // n_seq, chunks),
        in_specs=[
            pl.BlockSpec((n, D_MODEL), lambda g, b, c: (b * chunks + c, 0)),
            sect(0), sect(1), sect(2), sect(3),
            head_vec(lb_logits.shape[0]), head_vec(1),
            pl.BlockSpec(mat.shape, lambda g, b, c: (0, 0)),
        ] + state_in,
        out_specs=[pl.BlockSpec((n, wb), lambda g, b, c: (b * chunks + c, g)), state],
        out_shape=[
            jax.ShapeDtypeStruct((rows, D_MODEL), BF16),
            jax.ShapeDtypeStruct((n_state, REC_HEADS, REC_DK, REC_DV), F32),
        ],
        scratch_shapes=[pltpu.VMEM((n, wb), F32)],
        compiler_params=_compiler_params(("parallel", "parallel", "arbitrary"), 48),
        name="hgrn_scan",
    )(h, w_in, w_in, w_in, w_in, lb_logits, o_gain.reshape(1, D_MODEL), mat, *([] if s0 is None else [s0]))


def _outproj_kernel(a_ref, x_ref, g_ref, w_ref, o_ref):
    y = jnp.dot(a_ref[...].astype(BF16), w_ref[...], preferred_element_type=F32)
    o_ref[...] = x_ref[...] + g_ref[0] * y


def _outproj(a, x, mod, w, layer, tm):
    rows = x.shape[0]
    row = lambda i: (i, 0)
    return pl.pallas_call(
        _outproj_kernel,
        grid=(rows // tm,),
        in_specs=[
            pl.BlockSpec((tm, D_MODEL), row),
            pl.BlockSpec((tm, D_MODEL), row),
            mod.spec(tm),
            pl.BlockSpec((None, D_MODEL, D_MODEL), lambda i: (layer, 0, 0)),
        ],
        out_specs=pl.BlockSpec((tm, D_MODEL), row),
        out_shape=jax.ShapeDtypeStruct((rows, D_MODEL), F32),
        compiler_params=_compiler_params(("parallel",), 48),
        name="outproj",
    )(a, x, mod.parts[2], w)


def _mlp_kernel(x_ref, gain_ref, sc_ref, sh_ref, xo_ref, g_ref, wu_ref, wd_ref, o_ref, h_scr, u_scr, *, n_up, tf):
    j = pl.program_id(1)

    @pl.when(j == 0)
    def _():
        h_scr[...] = _modnorm(x_ref[...], gain_ref[...], sc_ref[0], sh_ref[0]).astype(BF16)

    @pl.when(j < n_up)
    def _():
        u = jnp.maximum(jnp.dot(h_scr[...], wu_ref[...], preferred_element_type=F32), 0.0)
        u2 = (u * u).astype(BF16)
        for c in range(n_up):
            @pl.when(j == c)
            def _():
                u_scr[:, c * tf:(c + 1) * tf] = u2

    @pl.when(j >= n_up)
    def _():
        d = jnp.dot(u_scr[...], wd_ref[...], preferred_element_type=F32)
        o_ref[...] = xo_ref[...] + g_ref[0] * d


MLP_UP_BLOCK = 2048
MLP_DOWN_BLOCK = 512


def _mlp(x, gain, mod, w_up, w_down, layer, tm, tf=MLP_UP_BLOCK, tn=MLP_DOWN_BLOCK):
    rows = x.shape[0]
    n_up, n_down = D_FF // tf, D_MODEL // tn
    row = lambda i, j: (i, 0)
    out_col = lambda i, j: jnp.maximum(j - n_up, 0)
    return pl.pallas_call(
        functools.partial(_mlp_kernel, n_up=n_up, tf=tf),
        grid=(rows // tm, n_up + n_down),
        in_specs=[
            pl.BlockSpec((tm, D_MODEL), row),
            pl.BlockSpec((1, D_MODEL), lambda i, j: (0, 0)),
            mod.spec(tm), mod.spec(tm),
            pl.BlockSpec((tm, tn), lambda i, j: (i, out_col(i, j))),
            mod.spec(tm, tn, out_col),
            pl.BlockSpec((None, D_MODEL, tf), lambda i, j: (layer, 0, jnp.minimum(j, n_up - 1))),
            pl.BlockSpec((None, D_FF, tn), lambda i, j: (layer, 0, out_col(i, j))),
        ],
        out_specs=pl.BlockSpec((tm, tn), lambda i, j: (i, out_col(i, j))),
        out_shape=jax.ShapeDtypeStruct((rows, D_MODEL), F32),
        scratch_shapes=[pltpu.VMEM((tm, D_MODEL), BF16), pltpu.VMEM((tm, D_FF), BF16)],
        compiler_params=_compiler_params(("parallel", "arbitrary"), 60),
        name="mlp",
    )(x, gain, mod.parts[4], mod.parts[3], x, mod.parts[5], w_up, w_down)


def _row_tile(rows):
    return min(rows, 512)


def _attn_layer(x, mod, gain_mix, w_qkv, q_gain, k_gain, sinks, w_o, layer, batch, seq, cache=None):
    rows = batch * seq
    tm = _row_tile(rows)
    if cache is None:
        tables = _rope_tables(jnp.arange(seq, dtype=jnp.int32))
        q, k, v = _qkv(x, gain_mix, mod, w_qkv, layer, q_gain, k_gain, tables, seq, tm, BF16)
        o = _attn_prompt(q, k, v, sinks, batch, seq)
        k_win = k.reshape(batch, seq, N_KV_HEADS, HEAD_DIM)[:, seq - WINDOW:]
        v_win = v.reshape(batch, seq, N_KV_HEADS, HEAD_DIM)[:, seq - WINDOW:]
    else:
        pos = PAST_LEN + jnp.arange(seq, dtype=jnp.int32)
        tables = tuple(jnp.tile(t, (batch, 1)) for t in _rope_tables(pos))
        q, k, v = _qkv(x, gain_mix, mod, w_qkv, layer, q_gain, k_gain, tables, rows, tm, F32)
        ck = cache[0].reshape(-1, batch, WINDOW, KV_WIDTH)
        cv = cache[1].reshape(-1, batch, WINDOW, KV_WIDTH)
        o, k_win, v_win = _attn_sample(q, k, v, ck, cv, layer, sinks, batch, seq)
        k_win = k_win.reshape(batch, WINDOW, N_KV_HEADS, HEAD_DIM)
        v_win = v_win.reshape(batch, WINDOW, N_KV_HEADS, HEAD_DIM)
    return _outproj(o, x, mod, w_o, layer, tm), k_win, v_win


def _hgrn_layer(x, mod, gain_mix, w_in, lb_logits, o_gain, w_o, s0, layer, batch, seq):
    rows = batch * seq
    tm = _row_tile(rows)
    h = _norm_rows(x, gain_mix, mod, tm)
    a, state = _hgrn_scan(h, w_in, lb_logits, o_gain, s0, layer, batch, seq)
    return _outproj(a, x, mod, w_o, layer, tm), state


def kernel(x_prompt, x_sample, cache_k_win, cache_v_win, state_hgrn, c_prompt, c_sample, norm_gain, w_ada, b_ada,
           attn_w_qkv, attn_q_gain, attn_k_gain, attn_sinks, attn_w_o, rec_w_in, rec_lb_logits, rec_o_gain,
           rec_w_o, mlp_w_up, mlp_w_down):
    bp, lp, _ = x_prompt.shape
    bs, ls, _ = x_sample.shape
    n_c = bp + bs
    c_all = jnp.concatenate([c_prompt, c_sample, jnp.zeros((-n_c % 16, D_MODEL), F32)], axis=0)
    mods = _ada_mods(c_all, w_ada, b_ada)

    xp = x_prompt.reshape(bp * lp, D_MODEL)
    xs = x_sample.reshape(bs * ls, D_MODEL)
    lb_logits = rec_lb_logits.astype(F32)
    w_qkv, w_ao = attn_w_qkv.astype(BF16), attn_w_o.astype(BF16)
    w_in, w_ro = rec_w_in.astype(BF16), rec_w_o.astype(BF16)
    w_up, w_down = mlp_w_up.astype(BF16), mlp_w_down.astype(BF16)
    kwp, vwp, kws, vws, sp, ss = [], [], [], [], [], []
    for i in range(DEPTH):
        j = i // N_MIXERS
        mod_p = _Mod(mods[i, :bp], lp, per_row=False)
        mod_s = _Mod(mods[i, bp:n_c], ls, per_row=True)
        gain_mix = norm_gain[i, 0].reshape(1, D_MODEL)
        gain_mlp = norm_gain[i, 1].reshape(1, D_MODEL)
        if i % N_MIXERS == 0:
            aw = (w_qkv, attn_q_gain[j], attn_k_gain[j], attn_sinks[j], w_ao, j)
            xp, kp, vp = _attn_layer(xp, mod_p, gain_mix, *aw, bp, lp)
            xs, kn, vn = _attn_layer(xs, mod_s, gain_mix, *aw, bs, ls, cache=(cache_k_win, cache_v_win))
            kwp.append(kp); vwp.append(vp); kws.append(kn); vws.append(vn)
        else:
            rw = (w_in, lb_logits, rec_o_gain[j], w_ro)
            xp, s_p = _hgrn_layer(xp, mod_p, gain_mix, *rw, None, j, bp, lp)
            xs, s_s = _hgrn_layer(xs, mod_s, gain_mix, *rw, state_hgrn, j, bs, ls)
            sp.append(s_p); ss.append(s_s)
        xp = _mlp(xp, gain_mlp, mod_p, w_up, w_down, i, _row_tile(bp * lp))
        xs = _mlp(xs, gain_mlp, mod_s, w_up, w_down, i, _row_tile(bs * ls))
    return (xp.reshape(bp, lp, D_MODEL), xs.reshape(bs, ls, D_MODEL),
            jnp.stack(kwp), jnp.stack(vwp), jnp.stack(kws), jnp.stack(vws), jnp.stack(sp), jnp.stack(ss))
```

```python
import functools

import numpy as np
import jax
import jax.numpy as jnp
from jax import lax
from jax.experimental import pallas as pl
from jax.experimental.pallas import tpu as pltpu

F32 = jnp.float32
BF16 = jnp.bfloat16

D_MODEL = 2048
DEPTH = 4
N_MIXERS = 2
PAST_LEN = 16384
HEAD_DIM = 64
N_Q_HEADS = D_MODEL // HEAD_DIM
N_KV_HEADS = N_Q_HEADS // 8
GQA_GROUP = N_Q_HEADS // N_KV_HEADS
KV_WIDTH = N_KV_HEADS * HEAD_DIM
WINDOW = 128
ROT_DIM = HEAD_DIM // 4
ROPE_THETA = 500000.0
REC_HEADS = 16
REC_DK = 128
REC_DV = D_MODEL // REC_HEADS
D_FF = 4 * D_MODEL
EPS = 1e-6

LANES = 128
BF16_SUBLANES = 16
MIB = 1024 * 1024
SCAN_ROWS = 128
SCAN_HEADS = 8
SCAN_STATES = 64


def _compiler_params(semantics, vmem_mib):
    return pltpu.CompilerParams(dimension_semantics=semantics, vmem_limit_bytes=vmem_mib * MIB)


def _sigmoid(x):
    e = jnp.exp(-jnp.abs(x))
    r = 1.0 / (1.0 + e)
    return jnp.where(x >= 0, r, e * r)


def _silu(x):
    return x * _sigmoid(x)


def _modnorm(x, gain, scale, shift):
    var = jnp.mean(x * x, axis=-1, keepdims=True)
    y = x * lax.rsqrt(var + EPS) * gain
    return y * (1.0 + scale) + shift


def _split_bf16(x, parts):
    out = []
    for _ in range(parts - 1):
        hi = x.astype(BF16)
        out.append(hi)
        x = x - hi.astype(F32)
    out.append(x.astype(BF16))
    return out


SHIFT1, SCALE1, GATE1, SHIFT2, SCALE2, GATE2 = range(6)


class _Mod:
    def __init__(self, table, layer, rows_per_group, per_row):
        self.table, self.layer, self.rows_per_group, self.per_row = table, layer, rows_per_group, per_row

    def spec(self, chunk, tm, tn=D_MODEL, col=lambda *_: 0):
        layer, first = self.layer, chunk * (D_MODEL // tn)
        if self.per_row:
            return pl.BlockSpec((None, 1, tm, tn), lambda i, *rest: (layer, 0, i, first + col(i, *rest)))
        assert self.rows_per_group % tm == 0
        per = self.rows_per_group // tm
        return pl.BlockSpec((None, 1, 1, tn), lambda i, *rest: (layer, i // per, 0, first + col(i, *rest)))


class _SideCast:
    def __init__(self, stacks, layer, n_steps, step_of):
        self.stacks = stacks
        self.in_specs, self.out_specs, self.out_shape = [], [], []
        for w in stacks:
            _, rows, cols = w.shape
            assert rows % (n_steps * BF16_SUBLANES) == 0
            block = rows // n_steps
            self.in_specs.append(pl.BlockSpec((None, block, cols), lambda *g: (layer, step_of(*g), 0)))
            self.out_specs.append(pl.BlockSpec((block, cols), lambda *g: (step_of(*g), 0)))
            self.out_shape.append(jax.ShapeDtypeStruct((rows, cols), BF16))

    @staticmethod
    def run(src_refs, dst_refs):
        for src, dst in zip(src_refs, dst_refs):
            dst[...] = src[...].astype(BF16)


def _ada_kernel(c_ref, w_ref, b_ref, o_ref):
    s = _silu(c_ref[...]).astype(BF16)
    o_ref[0] = jnp.dot(s, w_ref[0].astype(BF16), preferred_element_type=F32) + b_ref[0]


def _ada_mods(c_all, w_ada, b_ada):
    rows = c_all.shape[0]
    tn = 1024
    n = 6 * D_MODEL
    return pl.pallas_call(
        _ada_kernel,
        grid=(DEPTH, n // tn),
        in_specs=[
            pl.BlockSpec((rows, D_MODEL), lambda l, j: (0, 0)),
            pl.BlockSpec((1, D_MODEL, tn), lambda l, j: (l, 0, j)),
            pl.BlockSpec((1, 1, tn), lambda l, j: (l, 0, j)),
        ],
        out_specs=pl.BlockSpec((1, rows, tn), lambda l, j: (l, 0, j)),
        out_shape=jax.ShapeDtypeStruct((DEPTH, rows, n), F32),
        compiler_params=_compiler_params(("parallel", "parallel"), 40),
        name="ada_mods",
    )(c_all, w_ada, b_ada.reshape(DEPTH, 1, n))


def _norm_rows_kernel(x_ref, gain_ref, sc_ref, sh_ref, o_ref):
    o_ref[...] = _modnorm(x_ref[...], gain_ref[...], sc_ref[0], sh_ref[0]).astype(o_ref.dtype)


def _norm_rows(x, gain, mod, tm):
    rows = x.shape[0]
    return pl.pallas_call(
        _norm_rows_kernel,
        grid=(rows // tm,),
        in_specs=[
            pl.BlockSpec((tm, D_MODEL), lambda i: (i, 0)),
            pl.BlockSpec((1, D_MODEL), lambda i: (0, 0)),
            mod.spec(SCALE1, tm),
            mod.spec(SHIFT1, tm),
        ],
        out_specs=pl.BlockSpec((tm, D_MODEL), lambda i: (i, 0)),
        out_shape=jax.ShapeDtypeStruct((rows, D_MODEL), BF16),
        compiler_params=_compiler_params(("parallel",), 32),
        name="norm_rows",
    )(x, gain, mod.table, mod.table)


def _qkv_kernel(x_ref, gain_ref, sc_ref, sh_ref, w_ref, qg_ref, kg_ref, cos_ref, s1_ref, s2_ref,
                q_ref, k_ref, v_ref):
    h = _modnorm(x_ref[...], gain_ref[...], sc_ref[0], sh_ref[0]).astype(BF16)
    acc = jnp.dot(h, w_ref[...], preferred_element_type=F32)
    cos, s1, s2 = cos_ref[...], s1_ref[...], s2_ref[...]
    qk_width = D_MODEL + KV_WIDTH
    first_head = lax.broadcasted_iota(jnp.int32, (x_ref.shape[0], LANES), 1) < HEAD_DIM
    for c in range(qk_width // LANES):
        xc = acc[:, c * LANES:(c + 1) * LANES]
        x2 = xc * xc
        ssq = jnp.where(first_head,
                        jnp.sum(jnp.where(first_head, x2, 0.0), axis=-1, keepdims=True),
                        jnp.sum(jnp.where(first_head, 0.0, x2), axis=-1, keepdims=True))
        is_q = c * LANES < D_MODEL
        y = xc * lax.rsqrt(ssq * (1.0 / HEAD_DIM) + EPS) * (qg_ref[...] if is_q else kg_ref[...])
        r = y * cos + pltpu.roll(y, LANES - ROT_DIM // 2, 1) * s1 + pltpu.roll(y, ROT_DIM // 2, 1) * s2
        if is_q:
            q_ref[:, c * LANES:(c + 1) * LANES] = (r * (HEAD_DIM ** -0.5)).astype(q_ref.dtype)
        else:
            k_ref[:, c * LANES - D_MODEL:(c + 1) * LANES - D_MODEL] = r
    v_ref[...] = acc[:, qk_width:]


def _rope_tables(pos):
    half = ROT_DIM // 2
    inv = ROPE_THETA ** (-jnp.arange(half, dtype=F32) / half)
    ang = pos.astype(F32)[:, None] * inv[None, :]
    cos, sin = jnp.cos(ang), jnp.sin(ang)
    n = pos.shape[0]
    pad = jnp.zeros((n, HEAD_DIM - ROT_DIM), F32)
    zero = jnp.zeros((n, half), F32)
    c = jnp.concatenate([cos, cos, pad + 1.0], axis=1)
    s1 = jnp.concatenate([-sin, zero, pad], axis=1)
    s2 = jnp.concatenate([zero, sin, pad], axis=1)
    return tuple(jnp.tile(t, (1, LANES // HEAD_DIM)) for t in (c, s1, s2))


def _qkv(x, gain, mod, w, layer, q_gain, k_gain, tables, table_rows, tm, q_dtype):
    rows = x.shape[0]
    n = w.shape[-1]
    per = table_rows // tm
    tspec = pl.BlockSpec((tm, LANES), lambda i: (i % per, 0))
    vspec = pl.BlockSpec((1, LANES), lambda i: (0, 0))
    reps = LANES // HEAD_DIM
    return pl.pallas_call(
        _qkv_kernel,
        grid=(rows // tm,),
        in_specs=[
            pl.BlockSpec((tm, D_MODEL), lambda i: (i, 0)),
            pl.BlockSpec((1, D_MODEL), lambda i: (0, 0)),
            mod.spec(SCALE1, tm),
            mod.spec(SHIFT1, tm),
            pl.BlockSpec((None, D_MODEL, n), lambda i: (layer, 0, 0)),
            vspec, vspec, tspec, tspec, tspec,
        ],
        out_specs=[
            pl.BlockSpec((tm, D_MODEL), lambda i: (i, 0)),
            pl.BlockSpec((tm, KV_WIDTH), lambda i: (i, 0)),
            pl.BlockSpec((tm, KV_WIDTH), lambda i: (i, 0)),
        ],
        out_shape=[
            jax.ShapeDtypeStruct((rows, D_MODEL), q_dtype),
            jax.ShapeDtypeStruct((rows, KV_WIDTH), F32),
            jax.ShapeDtypeStruct((rows, KV_WIDTH), F32),
        ],
        compiler_params=_compiler_params(("parallel",), 48),
        name="qkv",
    )(x, gain, mod.table, mod.table, w,
      jnp.tile(q_gain.reshape(1, HEAD_DIM), (1, reps)), jnp.tile(k_gain.reshape(1, HEAD_DIM), (1, reps)),
      *tables)


def _softmax_with_sink(s, sink):
    m = jnp.maximum(jnp.max(s, axis=-1, keepdims=True), sink)
    p = jnp.exp(s - m)
    denom = jnp.sum(p, axis=-1, keepdims=True) + jnp.exp(sink - m)
    return p, denom


def _attn_prompt_kernel(sink_ref, q_ref, kp_ref, kc_ref, vp_ref, vc_ref, *rest):
    n_cast = len(rest) // 2
    o_ref = rest[n_cast]
    _SideCast.run(rest[:n_cast], rest[n_cast + 1:])
    j = pl.program_id(1)
    kk = jnp.concatenate([kp_ref[...], kc_ref[...]], axis=0).astype(BF16)
    vt = jnp.concatenate([vp_ref[...], vc_ref[...]], axis=0).T.astype(BF16)
    s = lax.broadcasted_iota(jnp.int32, (2 * WINDOW, WINDOW), 0)
    t = lax.broadcasted_iota(jnp.int32, (2 * WINDOW, WINDOW), 1)
    ok = (s > t) & (s <= t + WINDOW) & (s >= jnp.where(j > 0, 0, WINDOW))
    bias = jnp.where(ok, 0.0, -jnp.inf)
    zk = jnp.zeros((2 * WINDOW, HEAD_DIM), BF16)
    zv = jnp.zeros((HEAD_DIM, 2 * WINDOW), BF16)
    for g in range(N_KV_HEADS):
        kg = kk[:, g * HEAD_DIM:(g + 1) * HEAD_DIM]
        vg = vt[g * HEAD_DIM:(g + 1) * HEAD_DIM]
        k2 = jnp.concatenate([jnp.concatenate([kg, zk], axis=1), jnp.concatenate([zk, kg], axis=1)], axis=0)
        v2 = jnp.concatenate([jnp.concatenate([vg, zv], axis=1), jnp.concatenate([zv, vg], axis=1)], axis=0)
        for hp in range(GQA_GROUP // 2):
            h0 = g * GQA_GROUP + 2 * hp
            lanes = slice(h0 * HEAD_DIM, (h0 + 2) * HEAD_DIM)
            st = lax.dot_general(k2, q_ref[:, lanes], (((1,), (1,)), ((), ())), preferred_element_type=F32)
            probs = []
            for i in range(2):
                sh = st[i * 2 * WINDOW:(i + 1) * 2 * WINDOW] + bias
                sink = sink_ref[h0 + i]
                m = jnp.maximum(jnp.max(sh, axis=0, keepdims=True), sink)
                p = jnp.exp(sh - m)
                denom = jnp.sum(p, axis=0, keepdims=True) + jnp.exp(sink - m)
                probs.append((p * (1.0 / denom)).astype(BF16))
            ot = jnp.dot(v2, jnp.concatenate(probs, axis=0), preferred_element_type=F32)
            o_ref[:, lanes] = ot.T.astype(o_ref.dtype)


def _attn_prompt(q, k, v, sinks, batch, seq, cast_stacks, cast_layer):
    nb = seq // WINDOW
    rows = batch * seq
    cur = lambda b, j: (b * nb + j, 0)
    prev = lambda b, j: (b * nb + jnp.maximum(j - 1, 0), 0)
    kv_block = (WINDOW, KV_WIDTH)
    cast = _SideCast(cast_stacks, cast_layer, batch * nb, lambda b, j: b * nb + j)
    o, *casted = pl.pallas_call(
        _attn_prompt_kernel,
        grid=(batch, nb),
        in_specs=[
            pl.BlockSpec(memory_space=pltpu.SMEM),
            pl.BlockSpec((WINDOW, D_MODEL), cur),
            pl.BlockSpec(kv_block, prev), pl.BlockSpec(kv_block, cur),
            pl.BlockSpec(kv_block, prev), pl.BlockSpec(kv_block, cur),
        ] + cast.in_specs,
        out_specs=[pl.BlockSpec((WINDOW, D_MODEL), cur)] + cast.out_specs,
        out_shape=[jax.ShapeDtypeStruct((rows, D_MODEL), BF16)] + cast.out_shape,
        compiler_params=_compiler_params(("parallel", "parallel"), 32),
        name="attn_prompt",
    )(sinks, q, k, k, v, v, *cast.stacks)
    return o, casted


def _attn_sample_kernel(sink_ref, q_ref, k_ref, v_ref, ck_ref, cv_ref, o_ref, nk_ref, nv_ref, *, steps):
    kc, vc = ck_ref[0], cv_ref[0]
    kn, vn = k_ref[...], v_ref[...]
    nk_ref[0, :WINDOW - steps] = kc[steps:]
    nk_ref[0, WINDOW - steps:] = kn
    nv_ref[0, :WINDOW - steps] = vc[steps:]
    nv_ref[0, WINDOW - steps:] = vn
    zpad = jnp.zeros((WINDOW - steps, KV_WIDTH), F32)
    kk = jnp.concatenate([kc, kn, zpad], axis=0).astype(BF16)
    vv = jnp.concatenate([vc, vn, zpad], axis=0).astype(BF16)
    rows = GQA_GROUP * steps
    assert steps & (steps - 1) == 0
    t = lax.broadcasted_iota(jnp.int32, (rows, 2 * WINDOW), 0) & (steps - 1)
    s = lax.broadcasted_iota(jnp.int32, (rows, 2 * WINDOW), 1)
    ok = ((s < WINDOW) & (s > t)) | ((s >= WINDOW) & (s - WINDOW <= t))
    q = q_ref[...]
    for g in range(N_KV_HEADS):
        heads = range(g * GQA_GROUP, (g + 1) * GQA_GROUP)
        qg = jnp.concatenate([q[:, h * HEAD_DIM:(h + 1) * HEAD_DIM] for h in heads], axis=0).astype(BF16)
        sc = lax.dot_general(qg, kk[:, g * HEAD_DIM:(g + 1) * HEAD_DIM], (((1,), (1,)), ((), ())),
                             preferred_element_type=F32)
        p, denom = _softmax_with_sink(jnp.where(ok, sc, -jnp.inf), sink_ref[g])
        og = jnp.dot(p.astype(BF16), vv[:, g * HEAD_DIM:(g + 1) * HEAD_DIM], preferred_element_type=F32) / denom
        for i, h in enumerate(heads):
            o_ref[:, h * HEAD_DIM:(h + 1) * HEAD_DIM] = og[i * steps:(i + 1) * steps]


def _attn_sample(q, k, v, cache_k, cache_v, layer, sinks, batch, steps):
    rows = batch * steps
    sink_rows = jnp.repeat(sinks.reshape(N_KV_HEADS, GQA_GROUP), steps, axis=1)[..., None]
    row = lambda b: (b, 0)
    win = lambda b: (b, 0, 0)
    cache_block = (1, WINDOW, KV_WIDTH)
    return pl.pallas_call(
        functools.partial(_attn_sample_kernel, steps=steps),
        grid=(batch,),
        in_specs=[
            pl.BlockSpec((N_KV_HEADS, GQA_GROUP * steps, 1), lambda b: (0, 0, 0)),
            pl.BlockSpec((steps, D_MODEL), row),
            pl.BlockSpec((steps, KV_WIDTH), row), pl.BlockSpec((steps, KV_WIDTH), row),
            pl.BlockSpec((None,) + cache_block, lambda b: (layer, b, 0, 0)),
            pl.BlockSpec((None,) + cache_block, lambda b: (layer, b, 0, 0)),
        ],
        out_specs=[
            pl.BlockSpec((steps, D_MODEL), row),
            pl.BlockSpec(cache_block, win), pl.BlockSpec(cache_block, win),
        ],
        out_shape=[
            jax.ShapeDtypeStruct((rows, D_MODEL), F32),
            jax.ShapeDtypeStruct((batch, WINDOW, KV_WIDTH), F32),
            jax.ShapeDtypeStruct((batch, WINDOW, KV_WIDTH), F32),
        ],
        compiler_params=_compiler_params(("parallel",), 32),
        name="attn_sample",
    )(sink_rows, q, k, v, cache_k, cache_v)


SUBLANES = 8


def _levels_from_matrix(levels):
    return [b for b in levels if 2 < b <= SUBLANES]


def _scan_matrix(seq_rows, levels, with_suffix):
    n = SCAN_ROWS
    t = np.arange(n)[:, None]
    r = np.arange(n)[None, :]
    same = (t // seq_rows) == (r // seq_rows)
    mats = [same & (r <= t)] + ([same & (r > t)] if with_suffix else [])
    for b in _levels_from_matrix(levels):
        mid = (t // b) * b + b // 2
        upper = (t & (b // 2)) != 0
        mats.append(np.where(upper, (r >= mid) & (r <= t), (r > t) & (r < mid)))
    return jnp.asarray(np.concatenate(mats, axis=0), dtype=BF16)


def _scan_kernel(h_ref, wq_ref, wf_ref, wi_ref, wg_ref, lbl_ref, og_ref, mat_ref, *rest,
                 layer, n_seq, levels, has_s0, n_cast):
    rest = list(rest)
    s0_ref = rest.pop(0) if has_s0 else None
    cast_src, (a_ref, st_ref), cast_dst, g_scr = rest[:n_cast], rest[n_cast:n_cast + 2], rest[n_cast + 2:-1], rest[-1]
    _SideCast.run(cast_src, cast_dst)
    n = SCAN_ROWS
    heads = st_ref.shape[1]
    width = heads * LANES
    seq_rows = n // n_seq
    chunk = pl.program_id(2)

    @pl.when(chunk == 0)
    def _():
        for s in range(n_seq):
            for h in range(heads):
                st_ref[s, h] = jnp.zeros((REC_DV, REC_DK), F32) if s0_ref is None else s0_ref[s, h].T

    logits = lbl_ref[...]
    pexp = jnp.exp(logits - jnp.max(logits, axis=0, keepdims=True))
    psum = jnp.sum(pexp, axis=0, keepdims=True)
    lb = jnp.zeros_like(psum)
    for i in range(1, layer + 1):
        lb = lb + pexp[i:i + 1] / psum

    hb = h_ref[...]
    qs = _silu(jnp.dot(hb, wq_ref[...], preferred_element_type=F32))
    fz = jnp.dot(hb, wf_ref[...], preferred_element_type=F32)
    e = jnp.exp(-jnp.abs(fz))
    r = 1.0 / (1.0 + e)
    pos = fz >= 0
    f = lb + (1.0 - lb) * jnp.where(pos, r, e * r)
    logf = jnp.log(f)
    kin = (1.0 - lb) * jnp.where(pos, e * r, r)
    val = jnp.dot(hb, wi_ref[...], preferred_element_type=F32)
    gate = _silu(jnp.dot(hb, wg_ref[...], preferred_element_type=F32))

    mat = mat_ref[...]
    sums = None
    for part in _split_bf16(logf, 2):
        d = jnp.dot(mat, part, preferred_element_type=F32)
        sums = d if sums is None else sums + d
    gcum = sums[:n]
    g_scr[...] = gcum
    if n_seq == 1:
        gtail = g_scr[n - 1:n, :] - gcum
        block = 1
    else:
        gtail = sums[n:2 * n]
        block = 2
    level_sums = {b: sums[(block + i) * n:(block + i + 1) * n] for i, b in enumerate(_levels_from_matrix(levels))}

    t_row = lax.broadcasted_iota(jnp.int32, (n, width), 0)
    q_lv, k_lv = [], []
    for b in levels:
        half = b // 2
        if b > SUBLANES:
            zero = jnp.zeros((half, width), F32)
            qparts, kparts = [], []
            for lo in range(0, n, b):
                mid, hi = lo + half, lo + b
                ref = g_scr[mid - 1:mid, :]
                kparts += [kin[lo:mid] * jnp.exp(ref - gcum[lo:mid]), zero]
                qparts += [zero, qs[mid:hi] * jnp.exp(gcum[mid:hi] - ref)]
            qt, kt = jnp.concatenate(qparts, axis=0), jnp.concatenate(kparts, axis=0)
        else:
            upper = (t_row & half) != 0
            if b == 2:
                qt, kt = jnp.where(upper, qs * f, 0.0), jnp.where(upper, 0.0, kin)
            else:
                decay = jnp.exp(level_sums[b])
                qt, kt = jnp.where(upper, qs * decay, 0.0), jnp.where(upper, 0.0, kin * decay)
        q_lv.append(qt.astype(BF16))
        k_lv.append(kt.astype(BF16))
    qe_all = qs * jnp.exp(gcum)
    kt_all = kin * jnp.exp(gtail)

    t_lane = lax.broadcasted_iota(jnp.int32, (n, LANES), 0)
    tt = lax.broadcasted_iota(jnp.int32, (n, n), 0)
    ss = lax.broadcasted_iota(jnp.int32, (n, n), 1)
    contract_lanes = (((1,), (1,)), ((), ()))
    contract_rows = (((0,), (0,)), ((), ()))

    for h in range(heads):
        sl = slice(h * LANES, (h + 1) * LANES)
        vh = val[:, sl]
        vb = vh.astype(BF16)
        a = None
        for l, b in enumerate(levels):
            al = lax.dot_general(q_lv[l][:, sl], k_lv[l][:, sl], contract_lanes, preferred_element_type=F32)
            if b == n:
                a = al
            else:
                a = jnp.where((tt ^ ss) < b, al, 0.0 if a is None else a)
        o = jnp.dot(a.astype(BF16), vb, preferred_element_type=F32)
        o = o + jnp.sum(qs[:, sl] * kin[:, sl], axis=-1, keepdims=True) * vh
        qe, kt = qe_all[:, sl], kt_all[:, sl]
        for s in range(n_seq):
            last = (s + 1) * seq_rows - 1
            if n_seq == 1:
                qe_s, kt_s = qe, kt
            else:
                inside = (t_lane >= s * seq_rows) & (t_lane <= last)
                qe_s, kt_s = jnp.where(inside, qe, 0.0), jnp.where(inside, kt, 0.0)
            st = st_ref[s, h]
            o = o + lax.dot_general(qe_s.astype(BF16), st.astype(BF16), contract_lanes,
                                    preferred_element_type=F32)
            st_ref[s, h] = st * jnp.exp(g_scr[last:last + 1, sl]) + lax.dot_general(
                vb, kt_s.astype(BF16), contract_rows, preferred_element_type=F32)
        rs = lax.rsqrt(jnp.mean(o * o, axis=-1, keepdims=True) + EPS)
        a_ref[:, sl] = (o * rs * og_ref[:, sl] * gate[:, sl]).astype(a_ref.dtype)

    @pl.when(chunk == pl.num_programs(2) - 1)
    def _():
        for s in range(n_seq):
            for h in range(heads):
                st_ref[s, h] = st_ref[s, h].T


def _hgrn_scan(h, w_in, lb_logits, o_gain, s0, layer, n_state, seq_rows, cast_stacks=(), cast_layer=0):
    rows = h.shape[0]
    n = SCAN_ROWS
    n_seq = max(n // seq_rows, 1)
    chunks = max(seq_rows // n, 1)
    levels = [min(seq_rows, n) >> l for l in range(int(np.log2(min(seq_rows, n))))]
    mat = _scan_matrix(min(seq_rows, n), levels, with_suffix=n_seq > 1)
    heads = min(SCAN_HEADS, max(1, SCAN_STATES // n_seq))
    hg = REC_HEADS // heads
    wb = heads * LANES
    sect = lambda k: pl.BlockSpec((None, D_MODEL, wb), lambda g, b, c: (layer, 0, k * hg + g))
    head_vec = lambda r: pl.BlockSpec((r, wb), lambda g, b, c: (0, g))
    state_block = (n_seq, heads, REC_DK, REC_DV)
    state = pl.BlockSpec(state_block, lambda g, b, c: (b, g, 0, 0))
    state_in = [] if s0 is None else [pl.BlockSpec((None,) + state_block, lambda g, b, c: (layer, b, g, 0, 0))]
    groups = n_state // n_seq
    cast = _SideCast(cast_stacks, cast_layer, hg * groups * chunks, lambda g, b, c: (g * groups + b) * chunks + c)
    a, state_out, *casted = pl.pallas_call(
        functools.partial(_scan_kernel, layer=layer, n_seq=n_seq, levels=levels, has_s0=s0 is not None,
                          n_cast=len(cast_stacks)),
        grid=(hg, groups, chunks),
        in_specs=[
            pl.BlockSpec((n, D_MODEL), lambda g, b, c: (b * chunks + c, 0)),
            sect(0), sect(1), sect(2), sect(3),
            head_vec(lb_logits.shape[0]), head_vec(1),
            pl.BlockSpec(mat.shape, lambda g, b, c: (0, 0)),
        ] + state_in + cast.in_specs,
        out_specs=[pl.BlockSpec((n, wb), lambda g, b, c: (b * chunks + c, g)), state] + cast.out_specs,
        out_shape=[
            jax.ShapeDtypeStruct((rows, D_MODEL), BF16),
            jax.ShapeDtypeStruct((n_state, REC_HEADS, REC_DK, REC_DV), F32),
        ] + cast.out_shape,
        scratch_shapes=[pltpu.VMEM((n, wb), F32)],
        compiler_params=_compiler_params(("parallel", "parallel", "arbitrary"), 52),
        name="hgrn_scan",
    )(h, w_in, w_in, w_in, w_in, lb_logits, o_gain.reshape(1, D_MODEL), mat, *([] if s0 is None else [s0]),
      *cast.stacks)
    return a, state_out, casted


def _outproj_kernel(a_ref, x_ref, g_ref, w_ref, o_ref):
    y = jnp.dot(a_ref[...].astype(BF16), w_ref[...], preferred_element_type=F32)
    o_ref[...] = x_ref[...] + g_ref[0] * y


def _outproj(a, x, mod, w, layer, tm):
    rows = x.shape[0]
    row = lambda i: (i, 0)
    return pl.pallas_call(
        _outproj_kernel,
        grid=(rows // tm,),
        in_specs=[
            pl.BlockSpec((tm, D_MODEL), row),
            pl.BlockSpec((tm, D_MODEL), row),
            mod.spec(GATE1, tm),
            pl.BlockSpec((None, D_MODEL, D_MODEL), lambda i: (layer, 0, 0)),
        ],
        out_specs=pl.BlockSpec((tm, D_MODEL), row),
        out_shape=jax.ShapeDtypeStruct((rows, D_MODEL), F32),
        compiler_params=_compiler_params(("parallel",), 48),
        name="outproj",
    )(a, x, mod.table, w)


def _mlp_kernel(x_ref, gain_ref, sc_ref, sh_ref, xo_ref, g_ref, wu_ref, wd_ref, o_ref, h_scr, u_scr, *, n_up, tf):
    j = pl.program_id(1)

    @pl.when(j == 0)
    def _():
        h_scr[...] = _modnorm(x_ref[...], gain_ref[...], sc_ref[0], sh_ref[0]).astype(BF16)

    @pl.when(j < n_up)
    def _():
        u = jnp.maximum(jnp.dot(h_scr[...], wu_ref[...], preferred_element_type=F32), 0.0)
        u2 = (u * u).astype(BF16)
        for c in range(n_up):
            @pl.when(j == c)
            def _():
                u_scr[:, c * tf:(c + 1) * tf] = u2

    @pl.when(j >= n_up)
    def _():
        d = jnp.dot(u_scr[...], wd_ref[...], preferred_element_type=F32)
        o_ref[...] = xo_ref[...] + g_ref[0] * d


MLP_UP_BLOCK = 2048
MLP_DOWN_BLOCK = 512


def _mlp(x, gain, mod, w_up, w_down, tm, tf=MLP_UP_BLOCK, tn=MLP_DOWN_BLOCK):
    rows = x.shape[0]
    n_up, n_down = D_FF // tf, D_MODEL // tn
    row = lambda i, j: (i, 0)
    out_col = lambda i, j: jnp.maximum(j - n_up, 0)
    return pl.pallas_call(
        functools.partial(_mlp_kernel, n_up=n_up, tf=tf),
        grid=(rows // tm, n_up + n_down),
        in_specs=[
            pl.BlockSpec((tm, D_MODEL), row),
            pl.BlockSpec((1, D_MODEL), lambda i, j: (0, 0)),
            mod.spec(SCALE2, tm), mod.spec(SHIFT2, tm),
            pl.BlockSpec((tm, tn), lambda i, j: (i, out_col(i, j))),
            mod.spec(GATE2, tm, tn, out_col),
            pl.BlockSpec((D_MODEL, tf), lambda i, j: (0, jnp.minimum(j, n_up - 1))),
            pl.BlockSpec((D_FF, tn), lambda i, j: (0, out_col(i, j))),
        ],
        out_specs=pl.BlockSpec((tm, tn), lambda i, j: (i, out_col(i, j))),
        out_shape=jax.ShapeDtypeStruct((rows, D_MODEL), F32),
        scratch_shapes=[pltpu.VMEM((tm, D_MODEL), BF16), pltpu.VMEM((tm, D_FF), BF16)],
        compiler_params=_compiler_params(("parallel", "arbitrary"), 60),
        name="mlp",
    )(x, gain, mod.table, mod.table, x, mod.table, w_up, w_down)


def _row_tile(rows):
    return min(rows, 512)


def _attn_layer(x, mod, gain_mix, w_qkv, q_gain, k_gain, sinks, w_o, layer, batch, seq, cache=None, cast=((), 0)):
    rows = batch * seq
    tm = _row_tile(rows)
    casted = []
    if cache is None:
        tables = _rope_tables(jnp.arange(seq, dtype=jnp.int32))
        q, k, v = _qkv(x, gain_mix, mod, w_qkv, layer, q_gain, k_gain, tables, seq, tm, BF16)
        o, casted = _attn_prompt(q, k, v, sinks, batch, seq, *cast)
        window = lambda t: t.reshape(batch, seq, KV_WIDTH)[:, seq - WINDOW:].reshape(
            batch, WINDOW, N_KV_HEADS, HEAD_DIM)
        k_win, v_win = window(k), window(v)
    else:
        pos = PAST_LEN + jnp.arange(seq, dtype=jnp.int32)
        tables = tuple(jnp.tile(t, (batch, 1)) for t in _rope_tables(pos))
        q, k, v = _qkv(x, gain_mix, mod, w_qkv, layer, q_gain, k_gain, tables, rows, tm, F32)
        ck = cache[0].reshape(-1, batch, WINDOW, KV_WIDTH)
        cv = cache[1].reshape(-1, batch, WINDOW, KV_WIDTH)
        o, k_win, v_win = _attn_sample(q, k, v, ck, cv, layer, sinks, batch, seq)
        k_win = k_win.reshape(batch, WINDOW, N_KV_HEADS, HEAD_DIM)
        v_win = v_win.reshape(batch, WINDOW, N_KV_HEADS, HEAD_DIM)
    return _outproj(o, x, mod, w_o, layer, tm), k_win, v_win, casted


def _hgrn_layer(x, mod, gain_mix, w_in, lb_logits, o_gain, w_o, s0, layer, batch, seq, cast=((), 0)):
    rows = batch * seq
    tm = _row_tile(rows)
    h = _norm_rows(x, gain_mix, mod, tm)
    a, state, casted = _hgrn_scan(h, w_in, lb_logits, o_gain, s0, layer, batch, seq, *cast)
    return _outproj(a, x, mod, w_o, layer, tm), state, casted


def kernel(x_prompt, x_sample, cache_k_win, cache_v_win, state_hgrn, c_prompt, c_sample, norm_gain, w_ada, b_ada,
           attn_w_qkv, attn_q_gain, attn_k_gain, attn_sinks, attn_w_o, rec_w_in, rec_lb_logits, rec_o_gain,
           rec_w_o, mlp_w_up, mlp_w_down):
    bp, lp, _ = x_prompt.shape
    bs, ls, _ = x_sample.shape
    n_c = bp + bs
    c_all = jnp.concatenate([c_prompt, c_sample, jnp.zeros((-n_c % 16, D_MODEL), F32)], axis=0)
    mods = _ada_mods(c_all, w_ada, b_ada)

    xp = x_prompt.reshape(bp * lp, D_MODEL)
    xs = x_sample.reshape(bs * ls, D_MODEL)
    lb_logits = rec_lb_logits.astype(F32)
    w_qkv, w_ao = attn_w_qkv.astype(BF16), attn_w_o.astype(BF16)
    w_in, w_ro = rec_w_in.astype(BF16), rec_w_o.astype(BF16)
    table_p = mods[:, :bp].reshape(DEPTH, bp, 1, 6 * D_MODEL)
    table_s = jnp.repeat(mods[:, bp:n_c], ls, axis=1).reshape(DEPTH, 1, bs * ls, 6 * D_MODEL)
    kwp, vwp, kws, vws, sp, ss = [], [], [], [], [], []
    for i in range(DEPTH):
        j = i // N_MIXERS
        mod_p = _Mod(table_p, i, lp, per_row=False)
        mod_s = _Mod(table_s, i, ls, per_row=True)
        gain_mix = norm_gain[i, 0].reshape(1, D_MODEL)
        gain_mlp = norm_gain[i, 1].reshape(1, D_MODEL)
        mlp_cast = ((mlp_w_up, mlp_w_down), i)
        if i % N_MIXERS == 0:
            aw = (w_qkv, attn_q_gain[j], attn_k_gain[j], attn_sinks[j], w_ao, j)
            xp, kp, vp, (w_up, w_down) = _attn_layer(xp, mod_p, gain_mix, *aw, bp, lp, cast=mlp_cast)
            xs, kn, vn, _ = _attn_layer(xs, mod_s, gain_mix, *aw, bs, ls, cache=(cache_k_win, cache_v_win))
            kwp.append(kp); vwp.append(vp); kws.append(kn); vws.append(vn)
        else:
            rw = (w_in, lb_logits, rec_o_gain[j], w_ro)
            xp, s_p, (w_up, w_down) = _hgrn_layer(xp, mod_p, gain_mix, *rw, None, j, bp, lp, cast=mlp_cast)
            xs, s_s, _ = _hgrn_layer(xs, mod_s, gain_mix, *rw, state_hgrn, j, bs, ls)
            sp.append(s_p); ss.append(s_s)
        xp = _mlp(xp, gain_mlp, mod_p, w_up, w_down, _row_tile(bp * lp))
        xs = _mlp(xs, gain_mlp, mod_s, w_up, w_down, _row_tile(bs * ls))
    return (xp.reshape(bp, lp, D_MODEL), xs.reshape(bs, ls, D_MODEL),
            jnp.stack(kwp), jnp.stack(vwp), jnp.stack(kws), jnp.stack(vws), jnp.stack(sp), jnp.stack(ss))
```

```python
import functools

import numpy as np
import jax
import jax.numpy as jnp
from jax import lax
from jax.experimental import pallas as pl
from jax.experimental.pallas import tpu as pltpu

F32 = jnp.float32
BF16 = jnp.bfloat16

D_MODEL = 2048
DEPTH = 4
N_MIXERS = 2
PAST_LEN = 16384
HEAD_DIM = 64
N_Q_HEADS = D_MODEL // HEAD_DIM
N_KV_HEADS = N_Q_HEADS // 8
GQA_GROUP = N_Q_HEADS // N_KV_HEADS
KV_WIDTH = N_KV_HEADS * HEAD_DIM
WINDOW = 128
ROT_DIM = HEAD_DIM // 4
ROPE_THETA = 500000.0
REC_HEADS = 16
REC_DK = 128
REC_DV = D_MODEL // REC_HEADS
D_FF = 4 * D_MODEL
EPS = 1e-6

LANES = 128
BF16_SUBLANES = 16
MIB = 1024 * 1024
SCAN_ROWS = 128
SCAN_HEADS = 8
SCAN_STATES = 64


def _compiler_params(semantics, vmem_mib):
    return pltpu.CompilerParams(dimension_semantics=semantics, vmem_limit_bytes=vmem_mib * MIB)


def _silu(x):
    half = 0.5 * x
    return half + half * jnp.tanh(half)


def _modnorm(x, gain, scale, shift):
    var = jnp.mean(x * x, axis=-1, keepdims=True)
    y = x * lax.rsqrt(var + EPS) * gain
    return y * (1.0 + scale) + shift


def _split_bf16(x, parts):
    out = []
    for _ in range(parts - 1):
        hi = x.astype(BF16)
        out.append(hi)
        x = x - hi.astype(F32)
    out.append(x.astype(BF16))
    return out


SHIFT1, SCALE1, GATE1, SHIFT2, SCALE2, GATE2 = range(6)


class _Mod:
    def __init__(self, table, layer, rows_per_group, per_row):
        self.table, self.layer, self.rows_per_group, self.per_row = table, layer, rows_per_group, per_row

    def spec(self, chunk, tm, tn=D_MODEL, col=lambda *_: 0):
        layer, first = self.layer, chunk * (D_MODEL // tn)
        if self.per_row:
            return pl.BlockSpec((None, 1, tm, tn), lambda i, *rest: (layer, 0, i, first + col(i, *rest)))
        assert self.rows_per_group % tm == 0
        per = self.rows_per_group // tm
        return pl.BlockSpec((None, 1, 1, tn), lambda i, *rest: (layer, i // per, 0, first + col(i, *rest)))


class _SideCast:
    def __init__(self, items, n_steps, step_of):
        self.stacks = [w for w, _ in items]
        self.in_specs, self.out_specs, self.out_shape = [], [], []
        for w, layer in items:
            _, rows, cols = w.shape
            assert rows % (n_steps * BF16_SUBLANES) == 0
            block = rows // n_steps
            self.in_specs.append(pl.BlockSpec((None, block, cols), lambda *g, layer=layer: (layer, step_of(*g), 0)))
            self.out_specs.append(pl.BlockSpec((block, cols), lambda *g: (step_of(*g), 0)))
            self.out_shape.append(jax.ShapeDtypeStruct((rows, cols), BF16))

    @staticmethod
    def run(src_refs, dst_refs):
        for src, dst in zip(src_refs, dst_refs):
            dst[...] = src[...].astype(BF16)


def _ada_kernel(c_ref, w_ref, b_ref, o_ref):
    s = _silu(c_ref[...]).astype(BF16)
    o_ref[0] = jnp.dot(s, w_ref[0].astype(BF16), preferred_element_type=F32) + b_ref[0]


def _ada_mods(c_all, w_ada, b_ada):
    rows = c_all.shape[0]
    tn = 1024
    n = 6 * D_MODEL
    return pl.pallas_call(
        _ada_kernel,
        grid=(DEPTH, n // tn),
        in_specs=[
            pl.BlockSpec((rows, D_MODEL), lambda l, j: (0, 0)),
            pl.BlockSpec((1, D_MODEL, tn), lambda l, j: (l, 0, j)),
            pl.BlockSpec((1, 1, tn), lambda l, j: (l, 0, j)),
        ],
        out_specs=pl.BlockSpec((1, rows, tn), lambda l, j: (l, 0, j)),
        out_shape=jax.ShapeDtypeStruct((DEPTH, rows, n), F32),
        compiler_params=_compiler_params(("parallel", "parallel"), 40),
        name="ada_mods",
    )(c_all, w_ada, b_ada.reshape(DEPTH, 1, n))


def _norm_rows_kernel(x_ref, gain_ref, sc_ref, sh_ref, o_ref):
    o_ref[...] = _modnorm(x_ref[...], gain_ref[...], sc_ref[0], sh_ref[0]).astype(o_ref.dtype)


def _norm_rows(x, gain, mod, tm):
    rows = x.shape[0]
    return pl.pallas_call(
        _norm_rows_kernel,
        grid=(rows // tm,),
        in_specs=[
            pl.BlockSpec((tm, D_MODEL), lambda i: (i, 0)),
            pl.BlockSpec((1, D_MODEL), lambda i: (0, 0)),
            mod.spec(SCALE1, tm),
            mod.spec(SHIFT1, tm),
        ],
        out_specs=pl.BlockSpec((tm, D_MODEL), lambda i: (i, 0)),
        out_shape=jax.ShapeDtypeStruct((rows, D_MODEL), BF16),
        compiler_params=_compiler_params(("parallel",), 32),
        name="norm_rows",
    )(x, gain, mod.table, mod.table)


def _qkv_kernel(x_ref, gain_ref, sc_ref, sh_ref, w_ref, qg_ref, kg_ref, cos_ref, s1_ref, s2_ref, *rest):
    n_cast = (len(rest) - 3) // 2
    q_ref, k_ref, v_ref = rest[n_cast:n_cast + 3]
    _SideCast.run(rest[:n_cast], rest[n_cast + 3:])
    h = _modnorm(x_ref[...], gain_ref[...], sc_ref[0], sh_ref[0]).astype(BF16)
    acc = jnp.dot(h, w_ref[...], preferred_element_type=F32)
    cos, s1, s2 = cos_ref[...], s1_ref[...], s2_ref[...]
    qk_width = D_MODEL + KV_WIDTH
    first_head = lax.broadcasted_iota(jnp.int32, (x_ref.shape[0], LANES), 1) < HEAD_DIM
    for c in range(qk_width // LANES):
        xc = acc[:, c * LANES:(c + 1) * LANES]
        x2 = xc * xc
        ssq = jnp.where(first_head,
                        jnp.sum(jnp.where(first_head, x2, 0.0), axis=-1, keepdims=True),
                        jnp.sum(jnp.where(first_head, 0.0, x2), axis=-1, keepdims=True))
        is_q = c * LANES < D_MODEL
        y = xc * lax.rsqrt(ssq * (1.0 / HEAD_DIM) + EPS) * (qg_ref[...] if is_q else kg_ref[...])
        r = y * cos + pltpu.roll(y, LANES - ROT_DIM // 2, 1) * s1 + pltpu.roll(y, ROT_DIM // 2, 1) * s2
        if is_q:
            q_ref[:, c * LANES:(c + 1) * LANES] = (r * (HEAD_DIM ** -0.5)).astype(q_ref.dtype)
        else:
            k_ref[:, c * LANES - D_MODEL:(c + 1) * LANES - D_MODEL] = r
    v_ref[...] = acc[:, qk_width:]


def _rope_tables(pos):
    half = ROT_DIM // 2
    inv = ROPE_THETA ** (-jnp.arange(half, dtype=F32) / half)
    ang = pos.astype(F32)[:, None] * inv[None, :]
    cos, sin = jnp.cos(ang), jnp.sin(ang)
    n = pos.shape[0]
    pad = jnp.zeros((n, HEAD_DIM - ROT_DIM), F32)
    zero = jnp.zeros((n, half), F32)
    c = jnp.concatenate([cos, cos, pad + 1.0], axis=1)
    s1 = jnp.concatenate([-sin, zero, pad], axis=1)
    s2 = jnp.concatenate([zero, sin, pad], axis=1)
    return tuple(jnp.tile(t, (1, LANES // HEAD_DIM)) for t in (c, s1, s2))


def _qkv(x, gain, mod, w, q_gain, k_gain, tables, table_rows, tm, q_dtype, cast_items=()):
    rows = x.shape[0]
    n = w.shape[-1]
    per = table_rows // tm
    tspec = pl.BlockSpec((tm, LANES), lambda i: (i % per, 0))
    vspec = pl.BlockSpec((1, LANES), lambda i: (0, 0))
    reps = LANES // HEAD_DIM
    cast = _SideCast(cast_items, rows // tm, lambda i: i)
    q, k, v, *casted = pl.pallas_call(
        _qkv_kernel,
        grid=(rows // tm,),
        in_specs=[
            pl.BlockSpec((tm, D_MODEL), lambda i: (i, 0)),
            pl.BlockSpec((1, D_MODEL), lambda i: (0, 0)),
            mod.spec(SCALE1, tm),
            mod.spec(SHIFT1, tm),
            pl.BlockSpec((D_MODEL, n), lambda i: (0, 0)),
            vspec, vspec, tspec, tspec, tspec,
        ] + cast.in_specs,
        out_specs=[
            pl.BlockSpec((tm, D_MODEL), lambda i: (i, 0)),
            pl.BlockSpec((tm, KV_WIDTH), lambda i: (i, 0)),
            pl.BlockSpec((tm, KV_WIDTH), lambda i: (i, 0)),
        ] + cast.out_specs,
        out_shape=[
            jax.ShapeDtypeStruct((rows, D_MODEL), q_dtype),
            jax.ShapeDtypeStruct((rows, KV_WIDTH), F32),
            jax.ShapeDtypeStruct((rows, KV_WIDTH), F32),
        ] + cast.out_shape,
        compiler_params=_compiler_params(("parallel",), 56),
        name="qkv",
    )(x, gain, mod.table, mod.table, w,
      jnp.tile(q_gain.reshape(1, HEAD_DIM), (1, reps)), jnp.tile(k_gain.reshape(1, HEAD_DIM), (1, reps)),
      *tables, *cast.stacks)
    return q, k, v, casted


def _softmax_with_sink(s, sink):
    m = jnp.maximum(jnp.max(s, axis=-1, keepdims=True), sink)
    p = jnp.exp(s - m)
    denom = jnp.sum(p, axis=-1, keepdims=True) + jnp.exp(sink - m)
    return p, denom


def _attn_prompt_kernel(sink_ref, q_ref, kp_ref, kc_ref, vp_ref, vc_ref, *rest):
    n_cast = len(rest) // 2
    o_ref = rest[n_cast]
    _SideCast.run(rest[:n_cast], rest[n_cast + 1:])
    j = pl.program_id(1)
    kk = jnp.concatenate([kp_ref[...], kc_ref[...]], axis=0).astype(BF16)
    vt = jnp.concatenate([vp_ref[...], vc_ref[...]], axis=0).T.astype(BF16)
    s = lax.broadcasted_iota(jnp.int32, (2 * WINDOW, WINDOW), 0)
    t = lax.broadcasted_iota(jnp.int32, (2 * WINDOW, WINDOW), 1)
    ok = (s > t) & (s <= t + WINDOW) & (s >= jnp.where(j > 0, 0, WINDOW))
    bias = jnp.where(ok, 0.0, -jnp.inf)
    zk = jnp.zeros((2 * WINDOW, HEAD_DIM), BF16)
    zv = jnp.zeros((HEAD_DIM, 2 * WINDOW), BF16)
    for g in range(N_KV_HEADS):
        kg = kk[:, g * HEAD_DIM:(g + 1) * HEAD_DIM]
        vg = vt[g * HEAD_DIM:(g + 1) * HEAD_DIM]
        k2 = jnp.concatenate([jnp.concatenate([kg, zk], axis=1), jnp.concatenate([zk, kg], axis=1)], axis=0)
        v2 = jnp.concatenate([jnp.concatenate([vg, zv], axis=1), jnp.concatenate([zv, vg], axis=1)], axis=0)
        for hp in range(GQA_GROUP // 2):
            h0 = g * GQA_GROUP + 2 * hp
            lanes = slice(h0 * HEAD_DIM, (h0 + 2) * HEAD_DIM)
            st = lax.dot_general(k2, q_ref[:, lanes], (((1,), (1,)), ((), ())), preferred_element_type=F32)
            probs = []
            for i in range(2):
                sh = st[i * 2 * WINDOW:(i + 1) * 2 * WINDOW] + bias
                sink = sink_ref[h0 + i]
                m = jnp.maximum(jnp.max(sh, axis=0, keepdims=True), sink)
                p = jnp.exp(sh - m)
                denom = jnp.sum(p, axis=0, keepdims=True) + jnp.exp(sink - m)
                probs.append((p * (1.0 / denom)).astype(BF16))
            ot = jnp.dot(v2, jnp.concatenate(probs, axis=0), preferred_element_type=F32)
            o_ref[:, lanes] = ot.T.astype(o_ref.dtype)


def _attn_prompt(q, k, v, sinks, batch, seq, cast_items=()):
    nb = seq // WINDOW
    rows = batch * seq
    cur = lambda b, j: (b * nb + j, 0)
    prev = lambda b, j: (b * nb + jnp.maximum(j - 1, 0), 0)
    kv_block = (WINDOW, KV_WIDTH)
    cast = _SideCast(cast_items, batch * nb, lambda b, j: b * nb + j)
    o, *casted = pl.pallas_call(
        _attn_prompt_kernel,
        grid=(batch, nb),
        in_specs=[
            pl.BlockSpec(memory_space=pltpu.SMEM),
            pl.BlockSpec((WINDOW, D_MODEL), cur),
            pl.BlockSpec(kv_block, prev), pl.BlockSpec(kv_block, cur),
            pl.BlockSpec(kv_block, prev), pl.BlockSpec(kv_block, cur),
        ] + cast.in_specs,
        out_specs=[pl.BlockSpec((WINDOW, D_MODEL), cur)] + cast.out_specs,
        out_shape=[jax.ShapeDtypeStruct((rows, D_MODEL), BF16)] + cast.out_shape,
        compiler_params=_compiler_params(("parallel", "parallel"), 40),
        name="attn_prompt",
    )(sinks, q, k, k, v, v, *cast.stacks)
    return o, casted


SAMPLE_SEQS = 4


def _attn_sample_kernel(sink_ref, q_ref, k_ref, v_ref, ck_ref, cv_ref, o_ref, nk_ref, nv_ref, *, steps):
    rows = GQA_GROUP * steps
    assert steps & (steps - 1) == 0
    t = lax.broadcasted_iota(jnp.int32, (rows, 2 * WINDOW), 0) & (steps - 1)
    s = lax.broadcasted_iota(jnp.int32, (rows, 2 * WINDOW), 1)
    ok = ((s < WINDOW) & (s > t)) | ((s >= WINDOW) & (s - WINDOW <= t))
    zpad = jnp.zeros((WINDOW - steps, KV_WIDTH), F32)
    for b in range(ck_ref.shape[0]):
        new = slice(b * steps, (b + 1) * steps)
        kc, vc = ck_ref[b], cv_ref[b]
        kn, vn = k_ref[new], v_ref[new]
        nk_ref[b, :WINDOW - steps] = kc[steps:]
        nk_ref[b, WINDOW - steps:] = kn
        nv_ref[b, :WINDOW - steps] = vc[steps:]
        nv_ref[b, WINDOW - steps:] = vn
        kk = jnp.concatenate([kc, kn, zpad], axis=0).astype(BF16)
        vv = jnp.concatenate([vc, vn, zpad], axis=0).astype(BF16)
        q = q_ref[new]
        for g in range(N_KV_HEADS):
            heads = range(g * GQA_GROUP, (g + 1) * GQA_GROUP)
            qg = jnp.concatenate([q[:, h * HEAD_DIM:(h + 1) * HEAD_DIM] for h in heads], axis=0).astype(BF16)
            sc = lax.dot_general(qg, kk[:, g * HEAD_DIM:(g + 1) * HEAD_DIM], (((1,), (1,)), ((), ())),
                                 preferred_element_type=F32)
            p, denom = _softmax_with_sink(jnp.where(ok, sc, -jnp.inf), sink_ref[g])
            og = jnp.dot(p.astype(BF16), vv[:, g * HEAD_DIM:(g + 1) * HEAD_DIM],
                         preferred_element_type=F32) / denom
            for i, h in enumerate(heads):
                o_ref[new, h * HEAD_DIM:(h + 1) * HEAD_DIM] = og[i * steps:(i + 1) * steps]


def _attn_sample(q, k, v, cache_k, cache_v, layer, sinks, batch, steps):
    rows = batch * steps
    sink_rows = jnp.repeat(sinks.reshape(N_KV_HEADS, GQA_GROUP), steps, axis=1)[..., None]
    seqs = min(SAMPLE_SEQS, batch)
    assert batch % seqs == 0
    steps_blk = seqs * steps
    row = lambda b: (b, 0)
    win = lambda b: (b, 0, 0)
    cache_block = (seqs, WINDOW, KV_WIDTH)
    return pl.pallas_call(
        functools.partial(_attn_sample_kernel, steps=steps),
        grid=(batch // seqs,),
        in_specs=[
            pl.BlockSpec((N_KV_HEADS, GQA_GROUP * steps, 1), lambda b: (0, 0, 0)),
            pl.BlockSpec((steps_blk, D_MODEL), row),
            pl.BlockSpec((steps_blk, KV_WIDTH), row), pl.BlockSpec((steps_blk, KV_WIDTH), row),
            pl.BlockSpec((None,) + cache_block, lambda b: (layer, b, 0, 0)),
            pl.BlockSpec((None,) + cache_block, lambda b: (layer, b, 0, 0)),
        ],
        out_specs=[
            pl.BlockSpec((steps_blk, D_MODEL), row),
            pl.BlockSpec(cache_block, win), pl.BlockSpec(cache_block, win),
        ],
        out_shape=[
            jax.ShapeDtypeStruct((rows, D_MODEL), F32),
            jax.ShapeDtypeStruct((batch, WINDOW, KV_WIDTH), F32),
            jax.ShapeDtypeStruct((batch, WINDOW, KV_WIDTH), F32),
        ],
        compiler_params=_compiler_params(("parallel",), 32),
        name="attn_sample",
    )(sink_rows, q, k, v, cache_k, cache_v)


SUBLANES = 8


def _levels_from_matrix(levels):
    return [b for b in levels if 2 < b <= SUBLANES]


def _scan_matrix(seq_rows, levels, with_suffix):
    n = SCAN_ROWS
    t = np.arange(n)[:, None]
    r = np.arange(n)[None, :]
    same = (t // seq_rows) == (r // seq_rows)
    mats = [same & (r <= t)] + ([same & (r > t)] if with_suffix else [])
    for b in _levels_from_matrix(levels):
        mid = (t // b) * b + b // 2
        upper = (t & (b // 2)) != 0
        mats.append(np.where(upper, (r >= mid) & (r <= t), (r > t) & (r < mid)))
    return jnp.asarray(np.concatenate(mats, axis=0), dtype=BF16)


def _scan_kernel(h_ref, wq_ref, wf_ref, wi_ref, wg_ref, lbl_ref, og_ref, mat_ref, *rest,
                 layer, n_seq, levels, has_s0, n_cast):
    rest = list(rest)
    s0_ref = rest.pop(0) if has_s0 else None
    cast_src, (a_ref, st_ref), cast_dst, g_scr = rest[:n_cast], rest[n_cast:n_cast + 2], rest[n_cast + 2:-1], rest[-1]
    _SideCast.run(cast_src, cast_dst)
    n = SCAN_ROWS
    heads = st_ref.shape[1]
    width = heads * LANES
    seq_rows = n // n_seq
    chunk = pl.program_id(2)

    @pl.when(chunk == 0)
    def _():
        for s in range(n_seq):
            for h in range(heads):
                st_ref[s, h] = jnp.zeros((REC_DV, REC_DK), F32) if s0_ref is None else s0_ref[s, h].T

    logits = lbl_ref[...]
    pexp = jnp.exp(logits - jnp.max(logits, axis=0, keepdims=True))
    psum = jnp.sum(pexp, axis=0, keepdims=True)
    lb = jnp.zeros_like(psum)
    for i in range(1, layer + 1):
        lb = lb + pexp[i:i + 1] / psum

    hb = h_ref[...]
    qs = _silu(jnp.dot(hb, wq_ref[...], preferred_element_type=F32))
    fz = jnp.dot(hb, wf_ref[...], preferred_element_type=F32)
    e = jnp.exp(-jnp.abs(fz))
    r = 1.0 / (1.0 + e)
    pos = fz >= 0
    f = lb + (1.0 - lb) * jnp.where(pos, r, e * r)
    logf = jnp.log(f)
    kin = (1.0 - lb) * jnp.where(pos, e * r, r)
    val = jnp.dot(hb, wi_ref[...], preferred_element_type=F32)
    gate = _silu(jnp.dot(hb, wg_ref[...], preferred_element_type=F32))

    mat = mat_ref[...]
    sums = None
    for part in _split_bf16(logf, 2):
        d = jnp.dot(mat, part, preferred_element_type=F32)
        sums = d if sums is None else sums + d
    gcum = sums[:n]
    g_scr[...] = gcum
    if n_seq == 1:
        gtail = g_scr[n - 1:n, :] - gcum
        block = 1
    else:
        gtail = sums[n:2 * n]
        block = 2
    level_sums = {b: sums[(block + i) * n:(block + i + 1) * n] for i, b in enumerate(_levels_from_matrix(levels))}

    t_row = lax.broadcasted_iota(jnp.int32, (n, width), 0)
    q_lv, k_lv = [], []
    for b in levels:
        half = b // 2
        if b > SUBLANES:
            zero = jnp.zeros((half, width), F32)
            qparts, kparts = [], []
            for lo in range(0, n, b):
                mid, hi = lo + half, lo + b
                ref = g_scr[mid - 1:mid, :]
                kparts += [kin[lo:mid] * jnp.exp(ref - gcum[lo:mid]), zero]
                qparts += [zero, qs[mid:hi] * jnp.exp(gcum[mid:hi] - ref)]
            qt, kt = jnp.concatenate(qparts, axis=0), jnp.concatenate(kparts, axis=0)
        else:
            upper = (t_row & half) != 0
            if b == 2:
                qt, kt = jnp.where(upper, qs * f, 0.0), jnp.where(upper, 0.0, kin)
            else:
                decay = jnp.exp(level_sums[b])
                qt, kt = jnp.where(upper, qs * decay, 0.0), jnp.where(upper, 0.0, kin * decay)
        q_lv.append(qt.astype(BF16))
        k_lv.append(kt.astype(BF16))
    qe_all = qs * jnp.exp(gcum)
    kt_all = kin * jnp.exp(gtail)

    t_lane = lax.broadcasted_iota(jnp.int32, (n, LANES), 0)
    tt = lax.broadcasted_iota(jnp.int32, (n, n), 0)
    ss = lax.broadcasted_iota(jnp.int32, (n, n), 1)
    contract_lanes = (((1,), (1,)), ((), ()))
    contract_rows = (((0,), (0,)), ((), ()))

    for h in range(heads):
        sl = slice(h * LANES, (h + 1) * LANES)
        vh = val[:, sl]
        vb = vh.astype(BF16)
        a = None
        for l, b in enumerate(levels):
            al = lax.dot_general(q_lv[l][:, sl], k_lv[l][:, sl], contract_lanes, preferred_element_type=F32)
            if b == n:
                a = al
            else:
                a = jnp.where((tt ^ ss) < b, al, 0.0 if a is None else a)
        o = jnp.dot(a.astype(BF16), vb, preferred_element_type=F32)
        o = o + jnp.sum(qs[:, sl] * kin[:, sl], axis=-1, keepdims=True) * vh
        qe, kt = qe_all[:, sl], kt_all[:, sl]
        for s in range(n_seq):
            last = (s + 1) * seq_rows - 1
            if n_seq == 1:
                qe_s, kt_s = qe, kt
            else:
                inside = (t_lane >= s * seq_rows) & (t_lane <= last)
                qe_s, kt_s = jnp.where(inside, qe, 0.0), jnp.where(inside, kt, 0.0)
            st = st_ref[s, h]
            o = o + lax.dot_general(qe_s.astype(BF16), st.astype(BF16), contract_lanes,
                                    preferred_element_type=F32)
            st_ref[s, h] = st * jnp.exp(g_scr[last:last + 1, sl]) + lax.dot_general(
                vb, kt_s.astype(BF16), contract_rows, preferred_element_type=F32)
        rs = lax.rsqrt(jnp.mean(o * o, axis=-1, keepdims=True) + EPS)
        a_ref[:, sl] = (o * rs * og_ref[:, sl] * gate[:, sl]).astype(a_ref.dtype)

    @pl.when(chunk == pl.num_programs(2) - 1)
    def _():
        for s in range(n_seq):
            for h in range(heads):
                st_ref[s, h] = st_ref[s, h].T


def _hgrn_scan(h, w_in, lb_logits, o_gain, s0, layer, n_state, seq_rows, cast_items=()):
    rows = h.shape[0]
    n = SCAN_ROWS
    n_seq = max(n // seq_rows, 1)
    chunks = max(seq_rows // n, 1)
    levels = [min(seq_rows, n) >> l for l in range(int(np.log2(min(seq_rows, n))))]
    mat = _scan_matrix(min(seq_rows, n), levels, with_suffix=n_seq > 1)
    heads = min(SCAN_HEADS, max(1, SCAN_STATES // n_seq))
    hg = REC_HEADS // heads
    wb = heads * LANES
    sect = lambda k: pl.BlockSpec((D_MODEL, wb), lambda g, b, c: (0, k * hg + g))
    head_vec = lambda r: pl.BlockSpec((r, wb), lambda g, b, c: (0, g))
    state_block = (n_seq, heads, REC_DK, REC_DV)
    state = pl.BlockSpec(state_block, lambda g, b, c: (b, g, 0, 0))
    state_in = [] if s0 is None else [pl.BlockSpec((None,) + state_block, lambda g, b, c: (layer, b, g, 0, 0))]
    groups = n_state // n_seq
    cast = _SideCast(cast_items, hg * groups * chunks, lambda g, b, c: (g * groups + b) * chunks + c)
    a, state_out, *casted = pl.pallas_call(
        functools.partial(_scan_kernel, layer=layer, n_seq=n_seq, levels=levels, has_s0=s0 is not None,
                          n_cast=len(cast_items)),
        grid=(hg, groups, chunks),
        in_specs=[
            pl.BlockSpec((n, D_MODEL), lambda g, b, c: (b * chunks + c, 0)),
            sect(0), sect(1), sect(2), sect(3),
            head_vec(lb_logits.shape[0]), head_vec(1),
            pl.BlockSpec(mat.shape, lambda g, b, c: (0, 0)),
        ] + state_in + cast.in_specs,
        out_specs=[pl.BlockSpec((n, wb), lambda g, b, c: (b * chunks + c, g)), state] + cast.out_specs,
        out_shape=[
            jax.ShapeDtypeStruct((rows, D_MODEL), BF16),
            jax.ShapeDtypeStruct((n_state, REC_HEADS, REC_DK, REC_DV), F32),
        ] + cast.out_shape,
        scratch_shapes=[pltpu.VMEM((n, wb), F32)],
        compiler_params=_compiler_params(("parallel", "parallel", "arbitrary"), 52),
        name="hgrn_scan",
    )(h, w_in, w_in, w_in, w_in, lb_logits, o_gain.reshape(1, D_MODEL), mat, *([] if s0 is None else [s0]),
      *cast.stacks)
    return a, state_out, casted


def _outproj_kernel(a_ref, x_ref, g_ref, w_ref, o_ref):
    y = jnp.dot(a_ref[...].astype(BF16), w_ref[...], preferred_element_type=F32)
    o_ref[...] = x_ref[...] + g_ref[0] * y


def _outproj(a, x, mod, w, tm):
    rows = x.shape[0]
    row = lambda i: (i, 0)
    return pl.pallas_call(
        _outproj_kernel,
        grid=(rows // tm,),
        in_specs=[
            pl.BlockSpec((tm, D_MODEL), row),
            pl.BlockSpec((tm, D_MODEL), row),
            mod.spec(GATE1, tm),
            pl.BlockSpec((D_MODEL, D_MODEL), lambda i: (0, 0)),
        ],
        out_specs=pl.BlockSpec((tm, D_MODEL), row),
        out_shape=jax.ShapeDtypeStruct((rows, D_MODEL), F32),
        compiler_params=_compiler_params(("parallel",), 48),
        name="outproj",
    )(a, x, mod.table, w)


def _mlp_kernel(x_ref, gain_ref, sc_ref, sh_ref, xo_ref, g_ref, wu_ref, wd_ref, o_ref, h_scr, u_scr, *, n_up, tf):
    j = pl.program_id(1)

    @pl.when(j == 0)
    def _():
        h_scr[...] = _modnorm(x_ref[...], gain_ref[...], sc_ref[0], sh_ref[0]).astype(BF16)

    @pl.when(j < n_up)
    def _():
        u = jnp.maximum(jnp.dot(h_scr[...], wu_ref[...], preferred_element_type=F32), 0.0)
        u2 = (u * u).astype(BF16)
        for c in range(n_up):
            @pl.when(j == c)
            def _():
                u_scr[:, c * tf:(c + 1) * tf] = u2

    @pl.when(j >= n_up)
    def _():
        d = jnp.dot(u_scr[...], wd_ref[...], preferred_element_type=F32)
        o_ref[...] = xo_ref[...] + g_ref[0] * d


MLP_UP_BLOCK = 2048
MLP_DOWN_BLOCK = 512


def _mlp(x, gain, mod, w_up, w_down, tm, tf=MLP_UP_BLOCK, tn=MLP_DOWN_BLOCK):
    rows = x.shape[0]
    n_up, n_down = D_FF // tf, D_MODEL // tn
    row = lambda i, j: (i, 0)
    out_col = lambda i, j: jnp.maximum(j - n_up, 0)
    return pl.pallas_call(
        functools.partial(_mlp_kernel, n_up=n_up, tf=tf),
        grid=(rows // tm, n_up + n_down),
        in_specs=[
            pl.BlockSpec((tm, D_MODEL), row),
            pl.BlockSpec((1, D_MODEL), lambda i, j: (0, 0)),
            mod.spec(SCALE2, tm), mod.spec(SHIFT2, tm),
            pl.BlockSpec((tm, tn), lambda i, j: (i, out_col(i, j))),
            mod.spec(GATE2, tm, tn, out_col),
            pl.BlockSpec((D_MODEL, tf), lambda i, j: (0, jnp.minimum(j, n_up - 1))),
            pl.BlockSpec((D_FF, tn), lambda i, j: (0, out_col(i, j))),
        ],
        out_specs=pl.BlockSpec((tm, tn), lambda i, j: (i, out_col(i, j))),
        out_shape=jax.ShapeDtypeStruct((rows, D_MODEL), F32),
        scratch_shapes=[pltpu.VMEM((tm, D_MODEL), BF16), pltpu.VMEM((tm, D_FF), BF16)],
        compiler_params=_compiler_params(("parallel", "arbitrary"), 60),
        name="mlp",
    )(x, gain, mod.table, mod.table, x, mod.table, w_up, w_down)


def _row_tile(rows):
    return min(rows, 512)


def _attn_prompt_layer(x, mod, gain_mix, w_qkv, q_gain, k_gain, sinks, w_o_f32, layer, batch, seq,
                       qkv_cast, attn_cast):
    tm = _row_tile(batch * seq)
    tables = _rope_tables(jnp.arange(seq, dtype=jnp.int32))
    q, k, v, qkv_casted = _qkv(x, gain_mix, mod, w_qkv, q_gain, k_gain, tables, seq, tm, BF16, qkv_cast)
    o, (w_o, *attn_casted) = _attn_prompt(q, k, v, sinks, batch, seq, [(w_o_f32, layer)] + list(attn_cast))
    window = lambda t: t.reshape(batch, seq, KV_WIDTH)[:, seq - WINDOW:].reshape(batch, WINDOW, N_KV_HEADS, HEAD_DIM)
    return _outproj(o, x, mod, w_o, tm), window(k), window(v), w_o, qkv_casted, attn_casted


def _attn_sample_layer(x, mod, gain_mix, w_qkv, q_gain, k_gain, sinks, w_o, layer, batch, seq, cache_k, cache_v):
    rows = batch * seq
    tm = _row_tile(rows)
    pos = PAST_LEN + jnp.arange(seq, dtype=jnp.int32)
    tables = tuple(jnp.tile(t, (batch, 1)) for t in _rope_tables(pos))
    q, k, v, _ = _qkv(x, gain_mix, mod, w_qkv, q_gain, k_gain, tables, rows, tm, F32)
    ck = cache_k.reshape(-1, batch, WINDOW, KV_WIDTH)
    cv = cache_v.reshape(-1, batch, WINDOW, KV_WIDTH)
    o, k_win, v_win = _attn_sample(q, k, v, ck, cv, layer, sinks, batch, seq)
    k_win = k_win.reshape(batch, WINDOW, N_KV_HEADS, HEAD_DIM)
    v_win = v_win.reshape(batch, WINDOW, N_KV_HEADS, HEAD_DIM)
    return _outproj(o, x, mod, w_o, tm), k_win, v_win


def _hgrn_layer(x, mod, gain_mix, w_in, lb_logits, o_gain, w_o, s0, layer, batch, seq, cast_items=()):
    tm = _row_tile(batch * seq)
    h = _norm_rows(x, gain_mix, mod, tm)
    a, state, casted = _hgrn_scan(h, w_in, lb_logits, o_gain, s0, layer, batch, seq, cast_items)
    if w_o is None:
        w_o, *casted = casted
    return _outproj(a, x, mod, w_o, tm), state, w_o, casted


def kernel(x_prompt, x_sample, cache_k_win, cache_v_win, state_hgrn, c_prompt, c_sample, norm_gain, w_ada, b_ada,
           attn_w_qkv, attn_q_gain, attn_k_gain, attn_sinks, attn_w_o, rec_w_in, rec_lb_logits, rec_o_gain,
           rec_w_o, mlp_w_up, mlp_w_down):
    bp, lp, _ = x_prompt.shape
    bs, ls, _ = x_sample.shape
    n_c = bp + bs
    c_all = jnp.concatenate([c_prompt, c_sample, jnp.zeros((-n_c % 16, D_MODEL), F32)], axis=0)
    mods = _ada_mods(c_all, w_ada, b_ada)

    xp = x_prompt.reshape(bp * lp, D_MODEL)
    xs = x_sample.reshape(bs * ls, D_MODEL)
    lb_logits = rec_lb_logits.astype(F32)
    table_p = mods[:, :bp].reshape(DEPTH, bp, 1, 6 * D_MODEL)
    table_s = jnp.repeat(mods[:, bp:n_c], ls, axis=1).reshape(DEPTH, 1, bs * ls, 6 * D_MODEL)
    w_qkv = attn_w_qkv[0].astype(BF16)
    kwp, vwp, kws, vws, sp, ss = [], [], [], [], [], []
    for i in range(DEPTH):
        j = i // N_MIXERS
        mod_p = _Mod(table_p, i, lp, per_row=False)
        mod_s = _Mod(table_s, i, ls, per_row=True)
        gain_mix = norm_gain[i, 0].reshape(1, D_MODEL)
        gain_mlp = norm_gain[i, 1].reshape(1, D_MODEL)
        mlp_cast = [(mlp_w_up, i), (mlp_w_down, i)]
        if i % N_MIXERS == 0:
            aw = (w_qkv, attn_q_gain[j], attn_k_gain[j], attn_sinks[j])
            xp, kp, vp, w_o, (w_in,), (w_up, w_down) = _attn_prompt_layer(
                xp, mod_p, gain_mix, *aw, attn_w_o, j, bp, lp, [(rec_w_in, j)], mlp_cast)
            xs, kn, vn = _attn_sample_layer(xs, mod_s, gain_mix, *aw, w_o, j, bs, ls, cache_k_win, cache_v_win)
            kwp.append(kp); vwp.append(vp); kws.append(kn); vws.append(vn)
        else:
            next_qkv = [(attn_w_qkv, j + 1)] if j + 1 < attn_w_qkv.shape[0] else []
            rw = (w_in, lb_logits, rec_o_gain[j])
            xp, s_p, w_o, (w_up, w_down, *w_next) = _hgrn_layer(
                xp, mod_p, gain_mix, *rw, None, None, j, bp, lp, [(rec_w_o, j)] + mlp_cast + next_qkv)
            xs, s_s, _, _ = _hgrn_layer(xs, mod_s, gain_mix, *rw, w_o, state_hgrn, j, bs, ls)
            sp.append(s_p); ss.append(s_s)
            if w_next:
                w_qkv = w_next[0]
        xp = _mlp(xp, gain_mlp, mod_p, w_up, w_down, _row_tile(bp * lp))
        xs = _mlp(xs, gain_mlp, mod_s, w_up, w_down, _row_tile(bs * ls))
    return (xp.reshape(bp, lp, D_MODEL), xs.reshape(bs, ls, D_MODEL),
            jnp.stack(kwp), jnp.stack(vwp), jnp.stack(kws), jnp.stack(vws), jnp.stack(sp), jnp.stack(ss))
```

```python
import functools

import numpy as np
import jax
import jax.numpy as jnp
from jax import lax
from jax.experimental import pallas as pl
from jax.experimental.pallas import tpu as pltpu

F32 = jnp.float32
BF16 = jnp.bfloat16

D_MODEL = 2048
DEPTH = 4
N_MIXERS = 2
PAST_LEN = 16384
HEAD_DIM = 64
N_Q_HEADS = D_MODEL // HEAD_DIM
N_KV_HEADS = N_Q_HEADS // 8
GQA_GROUP = N_Q_HEADS // N_KV_HEADS
KV_WIDTH = N_KV_HEADS * HEAD_DIM
WINDOW = 128
ROT_DIM = HEAD_DIM // 4
ROPE_THETA = 500000.0
REC_HEADS = 16
REC_DK = 128
REC_DV = D_MODEL // REC_HEADS
D_FF = 4 * D_MODEL
EPS = 1e-6

LANES = 128
BF16_SUBLANES = 16
MIB = 1024 * 1024
SCAN_ROWS = 128
SCAN_CHUNKS = 2
SCAN_HEADS = 8
SCAN_STATES = 64


def _compiler_params(semantics, vmem_mib):
    return pltpu.CompilerParams(dimension_semantics=semantics, vmem_limit_bytes=vmem_mib * MIB)


def _silu(x):
    half = 0.5 * x
    return half + half * jnp.tanh(half)


def _modnorm(x, gain, scale, shift):
    var = jnp.mean(x * x, axis=-1, keepdims=True)
    y = x * lax.rsqrt(var + EPS) * gain
    return y * (1.0 + scale) + shift


def _split_bf16(x, parts):
    out = []
    for _ in range(parts - 1):
        hi = x.astype(BF16)
        out.append(hi)
        x = x - hi.astype(F32)
    out.append(x.astype(BF16))
    return out


SHIFT1, SCALE1, GATE1, SHIFT2, SCALE2, GATE2 = range(6)


class _Mod:
    def __init__(self, table, layer, rows_per_group, per_row):
        self.table, self.layer, self.rows_per_group, self.per_row = table, layer, rows_per_group, per_row

    def spec(self, chunk, tm, tn=D_MODEL, col=lambda *_: 0):
        layer, first = self.layer, chunk * (D_MODEL // tn)
        if self.per_row:
            return pl.BlockSpec((None, 1, tm, tn), lambda i, *rest: (layer, 0, i, first + col(i, *rest)))
        assert self.rows_per_group % tm == 0
        per = self.rows_per_group // tm
        return pl.BlockSpec((None, 1, 1, tn), lambda i, *rest: (layer, i // per, 0, first + col(i, *rest)))


class _SideCast:
    def __init__(self, items, n_steps, step_of):
        self.stacks = [w for w, _ in items]
        self.in_specs, self.out_specs, self.out_shape = [], [], []
        for w, layer in items:
            _, rows, cols = w.shape
            assert rows % (n_steps * BF16_SUBLANES) == 0
            block = rows // n_steps
            self.in_specs.append(pl.BlockSpec((None, block, cols), lambda *g, layer=layer: (layer, step_of(*g), 0)))
            self.out_specs.append(pl.BlockSpec((block, cols), lambda *g: (step_of(*g), 0)))
            self.out_shape.append(jax.ShapeDtypeStruct((rows, cols), BF16))

    @staticmethod
    def run(src_refs, dst_refs):
        for src, dst in zip(src_refs, dst_refs):
            dst[...] = src[...].astype(BF16)


def _ada_kernel(c_ref, w_ref, b_ref, o_ref):
    s = _silu(c_ref[...]).astype(BF16)
    o_ref[0] = jnp.dot(s, w_ref[0].astype(BF16), preferred_element_type=F32) + b_ref[0]


def _ada_mods(c_all, w_ada, b_ada):
    rows = c_all.shape[0]
    tn = 1024
    n = 6 * D_MODEL
    return pl.pallas_call(
        _ada_kernel,
        grid=(DEPTH, n // tn),
        in_specs=[
            pl.BlockSpec((rows, D_MODEL), lambda l, j: (0, 0)),
            pl.BlockSpec((1, D_MODEL, tn), lambda l, j: (l, 0, j)),
            pl.BlockSpec((1, 1, tn), lambda l, j: (l, 0, j)),
        ],
        out_specs=pl.BlockSpec((1, rows, tn), lambda l, j: (l, 0, j)),
        out_shape=jax.ShapeDtypeStruct((DEPTH, rows, n), F32),
        compiler_params=_compiler_params(("parallel", "parallel"), 40),
        name="ada_mods",
    )(c_all, w_ada, b_ada.reshape(DEPTH, 1, n))


def _norm_rows_kernel(x_ref, gain_ref, sc_ref, sh_ref, o_ref):
    o_ref[...] = _modnorm(x_ref[...], gain_ref[...], sc_ref[0], sh_ref[0]).astype(o_ref.dtype)


def _norm_rows(x, gain, mod, tm):
    rows = x.shape[0]
    return pl.pallas_call(
        _norm_rows_kernel,
        grid=(rows // tm,),
        in_specs=[
            pl.BlockSpec((tm, D_MODEL), lambda i: (i, 0)),
            pl.BlockSpec((1, D_MODEL), lambda i: (0, 0)),
            mod.spec(SCALE1, tm),
            mod.spec(SHIFT1, tm),
        ],
        out_specs=pl.BlockSpec((tm, D_MODEL), lambda i: (i, 0)),
        out_shape=jax.ShapeDtypeStruct((rows, D_MODEL), BF16),
        compiler_params=_compiler_params(("parallel",), 32),
        name="norm_rows",
    )(x, gain, mod.table, mod.table)


def _qkv_kernel(x_ref, gain_ref, sc_ref, sh_ref, w_ref, qg_ref, kg_ref, cos_ref, s1_ref, s2_ref, *rest):
    n_cast = (len(rest) - 3) // 2
    q_ref, k_ref, v_ref = rest[n_cast:n_cast + 3]
    _SideCast.run(rest[:n_cast], rest[n_cast + 3:])
    h = _modnorm(x_ref[...], gain_ref[...], sc_ref[0], sh_ref[0]).astype(BF16)
    acc = jnp.dot(h, w_ref[...], preferred_element_type=F32)
    cos, s1, s2 = cos_ref[...], s1_ref[...], s2_ref[...]
    qk_width = D_MODEL + KV_WIDTH
    first_head = lax.broadcasted_iota(jnp.int32, (x_ref.shape[0], LANES), 1) < HEAD_DIM
    for c in range(qk_width // LANES):
        xc = acc[:, c * LANES:(c + 1) * LANES]
        x2 = xc * xc
        ssq = jnp.where(first_head,
                        jnp.sum(jnp.where(first_head, x2, 0.0), axis=-1, keepdims=True),
                        jnp.sum(jnp.where(first_head, 0.0, x2), axis=-1, keepdims=True))
        is_q = c * LANES < D_MODEL
        y = xc * lax.rsqrt(ssq * (1.0 / HEAD_DIM) + EPS) * (qg_ref[...] if is_q else kg_ref[...])
        r = y * cos + pltpu.roll(y, LANES - ROT_DIM // 2, 1) * s1 + pltpu.roll(y, ROT_DIM // 2, 1) * s2
        if is_q:
            q_ref[:, c * LANES:(c + 1) * LANES] = (r * (HEAD_DIM ** -0.5)).astype(q_ref.dtype)
        else:
            k_ref[:, c * LANES - D_MODEL:(c + 1) * LANES - D_MODEL] = r
    v_ref[...] = acc[:, qk_width:]


def _rope_tables(pos):
    half = ROT_DIM // 2
    inv = ROPE_THETA ** (-jnp.arange(half, dtype=F32) / half)
    ang = pos.astype(F32)[:, None] * inv[None, :]
    cos, sin = jnp.cos(ang), jnp.sin(ang)
    n = pos.shape[0]
    pad = jnp.zeros((n, HEAD_DIM - ROT_DIM), F32)
    zero = jnp.zeros((n, half), F32)
    c = jnp.concatenate([cos, cos, pad + 1.0], axis=1)
    s1 = jnp.concatenate([-sin, zero, pad], axis=1)
    s2 = jnp.concatenate([zero, sin, pad], axis=1)
    return tuple(jnp.tile(t, (1, LANES // HEAD_DIM)) for t in (c, s1, s2))


def _qkv(x, gain, mod, w, q_gain, k_gain, tables, table_rows, tm, q_dtype, cast_items=()):
    rows = x.shape[0]
    n = w.shape[-1]
    per = table_rows // tm
    tspec = pl.BlockSpec((tm, LANES), lambda i: (i % per, 0))
    vspec = pl.BlockSpec((1, LANES), lambda i: (0, 0))
    reps = LANES // HEAD_DIM
    cast = _SideCast(cast_items, rows // tm, lambda i: i)
    q, k, v, *casted = pl.pallas_call(
        _qkv_kernel,
        grid=(rows // tm,),
        in_specs=[
            pl.BlockSpec((tm, D_MODEL), lambda i: (i, 0)),
            pl.BlockSpec((1, D_MODEL), lambda i: (0, 0)),
            mod.spec(SCALE1, tm),
            mod.spec(SHIFT1, tm),
            pl.BlockSpec((D_MODEL, n), lambda i: (0, 0)),
            vspec, vspec, tspec, tspec, tspec,
        ] + cast.in_specs,
        out_specs=[
            pl.BlockSpec((tm, D_MODEL), lambda i: (i, 0)),
            pl.BlockSpec((tm, KV_WIDTH), lambda i: (i, 0)),
            pl.BlockSpec((tm, KV_WIDTH), lambda i: (i, 0)),
        ] + cast.out_specs,
        out_shape=[
            jax.ShapeDtypeStruct((rows, D_MODEL), q_dtype),
            jax.ShapeDtypeStruct((rows, KV_WIDTH), F32),
            jax.ShapeDtypeStruct((rows, KV_WIDTH), F32),
        ] + cast.out_shape,
        compiler_params=_compiler_params(("parallel",), 56),
        name="qkv",
    )(x, gain, mod.table, mod.table, w,
      jnp.tile(q_gain.reshape(1, HEAD_DIM), (1, reps)), jnp.tile(k_gain.reshape(1, HEAD_DIM), (1, reps)),
      *tables, *cast.stacks)
    return q, k, v, casted


def _softmax_with_sink(s, sink):
    m = jnp.maximum(jnp.max(s, axis=-1, keepdims=True), sink)
    p = jnp.exp(s - m)
    denom = jnp.sum(p, axis=-1, keepdims=True) + jnp.exp(sink - m)
    return p, denom


def _attn_prompt_kernel(sink_ref, q_ref, kp_ref, kc_ref, vp_ref, vc_ref, *rest):
    n_cast = len(rest) // 2
    o_ref = rest[n_cast]
    _SideCast.run(rest[:n_cast], rest[n_cast + 1:])
    j = pl.program_id(1)
    kk = jnp.concatenate([kp_ref[...], kc_ref[...]], axis=0).astype(BF16)
    vt = jnp.concatenate([vp_ref[...], vc_ref[...]], axis=0).T.astype(BF16)
    s = lax.broadcasted_iota(jnp.int32, (2 * WINDOW, WINDOW), 0)
    t = lax.broadcasted_iota(jnp.int32, (2 * WINDOW, WINDOW), 1)
    ok = (s > t) & (s <= t + WINDOW) & (s >= jnp.where(j > 0, 0, WINDOW))
    bias = jnp.where(ok, 0.0, -jnp.inf)
    zk = jnp.zeros((2 * WINDOW, HEAD_DIM), BF16)
    zv = jnp.zeros((HEAD_DIM, 2 * WINDOW), BF16)
    for g in range(N_KV_HEADS):
        kg = kk[:, g * HEAD_DIM:(g + 1) * HEAD_DIM]
        vg = vt[g * HEAD_DIM:(g + 1) * HEAD_DIM]
        k2 = jnp.concatenate([jnp.concatenate([kg, zk], axis=1), jnp.concatenate([zk, kg], axis=1)], axis=0)
        v2 = jnp.concatenate([jnp.concatenate([vg, zv], axis=1), jnp.concatenate([zv, vg], axis=1)], axis=0)
        for hp in range(GQA_GROUP // 2):
            h0 = g * GQA_GROUP + 2 * hp
            lanes = slice(h0 * HEAD_DIM, (h0 + 2) * HEAD_DIM)
            st = lax.dot_general(k2, q_ref[:, lanes], (((1,), (1,)), ((), ())), preferred_element_type=F32)
            probs = []
            for i in range(2):
                sh = st[i * 2 * WINDOW:(i + 1) * 2 * WINDOW] + bias
                sink = sink_ref[h0 + i]
                m = jnp.maximum(jnp.max(sh, axis=0, keepdims=True), sink)
                p = jnp.exp(sh - m)
                denom = jnp.sum(p, axis=0, keepdims=True) + jnp.exp(sink - m)
                probs.append((p * (1.0 / denom)).astype(BF16))
            ot = jnp.dot(v2, jnp.concatenate(probs, axis=0), preferred_element_type=F32)
            o_ref[:, lanes] = ot.T.astype(o_ref.dtype)


def _attn_prompt(q, k, v, sinks, batch, seq, cast_items=()):
    nb = seq // WINDOW
    rows = batch * seq
    cur = lambda b, j: (b * nb + j, 0)
    prev = lambda b, j: (b * nb + jnp.maximum(j - 1, 0), 0)
    kv_block = (WINDOW, KV_WIDTH)
    cast = _SideCast(cast_items, batch * nb, lambda b, j: b * nb + j)
    o, *casted = pl.pallas_call(
        _attn_prompt_kernel,
        grid=(batch, nb),
        in_specs=[
            pl.BlockSpec(memory_space=pltpu.SMEM),
            pl.BlockSpec((WINDOW, D_MODEL), cur),
            pl.BlockSpec(kv_block, prev), pl.BlockSpec(kv_block, cur),
            pl.BlockSpec(kv_block, prev), pl.BlockSpec(kv_block, cur),
        ] + cast.in_specs,
        out_specs=[pl.BlockSpec((WINDOW, D_MODEL), cur)] + cast.out_specs,
        out_shape=[jax.ShapeDtypeStruct((rows, D_MODEL), BF16)] + cast.out_shape,
        compiler_params=_compiler_params(("parallel", "parallel"), 40),
        name="attn_prompt",
    )(sinks, q, k, k, v, v, *cast.stacks)
    return o, casted


SAMPLE_SEQS = 4


def _attn_sample_kernel(sink_ref, q_ref, k_ref, v_ref, ck_ref, cv_ref, o_ref, nk_ref, nv_ref, *, steps):
    rows = GQA_GROUP * steps
    assert steps & (steps - 1) == 0
    t = lax.broadcasted_iota(jnp.int32, (rows, 2 * WINDOW), 0) & (steps - 1)
    s = lax.broadcasted_iota(jnp.int32, (rows, 2 * WINDOW), 1)
    ok = ((s < WINDOW) & (s > t)) | ((s >= WINDOW) & (s - WINDOW <= t))
    zpad = jnp.zeros((WINDOW - steps, KV_WIDTH), F32)
    for b in range(ck_ref.shape[0]):
        new = slice(b * steps, (b + 1) * steps)
        kc, vc = ck_ref[b], cv_ref[b]
        kn, vn = k_ref[new], v_ref[new]
        nk_ref[b, :WINDOW - steps] = kc[steps:]
        nk_ref[b, WINDOW - steps:] = kn
        nv_ref[b, :WINDOW - steps] = vc[steps:]
        nv_ref[b, WINDOW - steps:] = vn
        kk = jnp.concatenate([kc, kn, zpad], axis=0).astype(BF16)
        vv = jnp.concatenate([vc, vn, zpad], axis=0).astype(BF16)
        q = q_ref[new]
        for g in range(N_KV_HEADS):
            heads = range(g * GQA_GROUP, (g + 1) * GQA_GROUP)
            qg = jnp.concatenate([q[:, h * HEAD_DIM:(h + 1) * HEAD_DIM] for h in heads], axis=0).astype(BF16)
            sc = lax.dot_general(qg, kk[:, g * HEAD_DIM:(g + 1) * HEAD_DIM], (((1,), (1,)), ((), ())),
                                 preferred_element_type=F32)
            p, denom = _softmax_with_sink(jnp.where(ok, sc, -jnp.inf), sink_ref[g])
            og = jnp.dot(p.astype(BF16), vv[:, g * HEAD_DIM:(g + 1) * HEAD_DIM],
                         preferred_element_type=F32) / denom
            for i, h in enumerate(heads):
                o_ref[new, h * HEAD_DIM:(h + 1) * HEAD_DIM] = og[i * steps:(i + 1) * steps]


def _attn_sample(q, k, v, cache_k, cache_v, layer, sinks, batch, steps):
    rows = batch * steps
    sink_rows = jnp.repeat(sinks.reshape(N_KV_HEADS, GQA_GROUP), steps, axis=1)[..., None]
    seqs = min(SAMPLE_SEQS, batch)
    assert batch % seqs == 0
    steps_blk = seqs * steps
    row = lambda b: (b, 0)
    win = lambda b: (b, 0, 0)
    cache_block = (seqs, WINDOW, KV_WIDTH)
    return pl.pallas_call(
        functools.partial(_attn_sample_kernel, steps=steps),
        grid=(batch // seqs,),
        in_specs=[
            pl.BlockSpec((N_KV_HEADS, GQA_GROUP * steps, 1), lambda b: (0, 0, 0)),
            pl.BlockSpec((steps_blk, D_MODEL), row),
            pl.BlockSpec((steps_blk, KV_WIDTH), row), pl.BlockSpec((steps_blk, KV_WIDTH), row),
            pl.BlockSpec((None,) + cache_block, lambda b: (layer, b, 0, 0)),
            pl.BlockSpec((None,) + cache_block, lambda b: (layer, b, 0, 0)),
        ],
        out_specs=[
            pl.BlockSpec((steps_blk, D_MODEL), row),
            pl.BlockSpec(cache_block, win), pl.BlockSpec(cache_block, win),
        ],
        out_shape=[
            jax.ShapeDtypeStruct((rows, D_MODEL), F32),
            jax.ShapeDtypeStruct((batch, WINDOW, KV_WIDTH), F32),
            jax.ShapeDtypeStruct((batch, WINDOW, KV_WIDTH), F32),
        ],
        compiler_params=_compiler_params(("parallel",), 32),
        name="attn_sample",
    )(sink_rows, q, k, v, cache_k, cache_v)


SUBLANES = 8


def _levels_from_matrix(levels):
    return [b for b in levels if 2 < b <= SUBLANES]


def _scan_matrix(seq_rows, levels, with_suffix):
    n = SCAN_ROWS
    t = np.arange(n)[:, None]
    r = np.arange(n)[None, :]
    same = (t // seq_rows) == (r // seq_rows)
    mats = [same & (r <= t)] + ([same & (r > t)] if with_suffix else [])
    for b in _levels_from_matrix(levels):
        mid = (t // b) * b + b // 2
        upper = (t & (b // 2)) != 0
        mats.append(np.where(upper, (r >= mid) & (r <= t), (r > t) & (r < mid)))
    return jnp.asarray(np.concatenate(mats, axis=0), dtype=BF16)


def _scan_kernel(h_ref, wq_ref, wf_ref, wi_ref, wg_ref, lbl_ref, og_ref, mat_ref, *rest,
                 layer, n_seq, levels, has_s0, n_cast):
    rest = list(rest)
    s0_ref = rest.pop(0) if has_s0 else None
    cast_src, (a_ref, st_ref), cast_dst, g_scr = rest[:n_cast], rest[n_cast:n_cast + 2], rest[n_cast + 2:-1], rest[-1]
    _SideCast.run(cast_src, cast_dst)
    n = SCAN_ROWS
    heads = st_ref.shape[1]
    width = heads * LANES
    seq_rows = n // n_seq
    chunk = pl.program_id(2)

    @pl.when(chunk == 0)
    def _():
        for s in range(n_seq):
            for h in range(heads):
                st_ref[s, h] = jnp.zeros((REC_DV, REC_DK), F32) if s0_ref is None else s0_ref[s, h].T

    logits = lbl_ref[...]
    pexp = jnp.exp(logits - jnp.max(logits, axis=0, keepdims=True))
    psum = jnp.sum(pexp, axis=0, keepdims=True)
    lb = jnp.zeros_like(psum)
    for i in range(1, layer + 1):
        lb = lb + pexp[i:i + 1] / psum

    hb = h_ref[...]
    project = lambda w_ref: jnp.dot(hb, w_ref[...], preferred_element_type=F32)
    qs_all = _silu(project(wq_ref))
    fz = project(wf_ref)
    e = jnp.exp(-jnp.abs(fz))
    r = 1.0 / (1.0 + e)
    pos = fz >= 0
    f_all = lb + (1.0 - lb) * jnp.where(pos, r, e * r)
    logf_all = jnp.log(f_all)
    kin_all = (1.0 - lb) * jnp.where(pos, e * r, r)
    val_all = project(wi_ref)
    gate_all = _silu(project(wg_ref))
    mat = mat_ref[...]
    t_row = lax.broadcasted_iota(jnp.int32, (n, width), 0)
    t_lane = lax.broadcasted_iota(jnp.int32, (n, LANES), 0)
    tt = lax.broadcasted_iota(jnp.int32, (n, n), 0)
    ss = lax.broadcasted_iota(jnp.int32, (n, n), 1)
    same_block = {b: (tt ^ ss) < b for b in levels if b < n}
    for u in range(g_scr.shape[0]):
        rows = slice(u * n, (u + 1) * n)
        _scan_chunk(qs_all[rows], f_all[rows], logf_all[rows], kin_all[rows], val_all[rows], gate_all[rows], mat,
                    og_ref, g_scr.at[u], st_ref, a_ref.at[rows], t_row, t_lane, same_block, n_seq, levels)

    @pl.when(chunk == pl.num_programs(2) - 1)
    def _():
        for s in range(n_seq):
            for h in range(heads):
                st_ref[s, h] = st_ref[s, h].T


def _scan_chunk(qs, f, logf, kin, val, gate, mat, og_ref, g_scr, st_ref, a_ref, t_row, t_lane, same_block,
                n_seq, levels):
    n = SCAN_ROWS
    heads = st_ref.shape[1]
    width = heads * LANES
    seq_rows = n // n_seq
    sums = None
    for part in _split_bf16(logf, 2):
        d = jnp.dot(mat, part, preferred_element_type=F32)
        sums = d if sums is None else sums + d
    gcum = sums[:n]
    g_scr[...] = gcum
    if n_seq == 1:
        gtail = g_scr[n - 1:n, :] - gcum
        block = 1
    else:
        gtail = sums[n:2 * n]
        block = 2
    level_sums = {b: sums[(block + i) * n:(block + i + 1) * n] for i, b in enumerate(_levels_from_matrix(levels))}

    q_lv, k_lv = [], []
    for b in levels:
        half = b // 2
        if b > SUBLANES:
            zero = jnp.zeros((half, width), F32)
            qparts, kparts = [], []
            for lo in range(0, n, b):
                mid, hi = lo + half, lo + b
                ref = g_scr[mid - 1:mid, :]
                kparts += [kin[lo:mid] * jnp.exp(ref - gcum[lo:mid]), zero]
                qparts += [zero, qs[mid:hi] * jnp.exp(gcum[mid:hi] - ref)]
            qt, kt = jnp.concatenate(qparts, axis=0), jnp.concatenate(kparts, axis=0)
        else:
            upper = (t_row & half) != 0
            if b == 2:
                qt, kt = jnp.where(upper, qs * f, 0.0), jnp.where(upper, 0.0, kin)
            else:
                decay = jnp.exp(level_sums[b])
                qt, kt = jnp.where(upper, qs * decay, 0.0), jnp.where(upper, 0.0, kin * decay)
        q_lv.append(qt.astype(BF16))
        k_lv.append(kt.astype(BF16))
    qe_all = qs * jnp.exp(gcum)
    kt_all = kin * jnp.exp(gtail)
    contract_lanes = (((1,), (1,)), ((), ()))
    contract_rows = (((0,), (0,)), ((), ()))

    for h in range(heads):
        sl = slice(h * LANES, (h + 1) * LANES)
        vh = val[:, sl]
        vb = vh.astype(BF16)
        a = None
        for l, b in enumerate(levels):
            al = lax.dot_general(q_lv[l][:, sl], k_lv[l][:, sl], contract_lanes, preferred_element_type=F32)
            if b == n:
                a = al
            else:
                a = jnp.where(same_block[b], al, 0.0 if a is None else a)
        o = jnp.dot(a.astype(BF16), vb, preferred_element_type=F32)
        o = o + jnp.sum(qs[:, sl] * kin[:, sl], axis=-1, keepdims=True) * vh
        qe, kt = qe_all[:, sl], kt_all[:, sl]
        for s in range(n_seq):
            last = (s + 1) * seq_rows - 1
            if n_seq == 1:
                qe_s, kt_s = qe, kt
            else:
                inside = (t_lane >= s * seq_rows) & (t_lane <= last)
                qe_s, kt_s = jnp.where(inside, qe, 0.0), jnp.where(inside, kt, 0.0)
            st = st_ref[s, h]
            o = o + lax.dot_general(qe_s.astype(BF16), st.astype(BF16), contract_lanes,
                                    preferred_element_type=F32)
            st_ref[s, h] = st * jnp.exp(g_scr[last:last + 1, sl]) + lax.dot_general(
                vb, kt_s.astype(BF16), contract_rows, preferred_element_type=F32)
        rs = lax.rsqrt(jnp.mean(o * o, axis=-1, keepdims=True) + EPS)
        a_ref[:, sl] = (o * rs * og_ref[:, sl] * gate[:, sl]).astype(a_ref.dtype)


def _hgrn_scan(h, w_in, lb_logits, o_gain, s0, layer, n_state, seq_rows, cast_items=()):
    rows = h.shape[0]
    n = SCAN_ROWS
    n_seq = max(n // seq_rows, 1)
    sub = SCAN_CHUNKS if seq_rows % (SCAN_CHUNKS * n) == 0 else 1
    chunks = max(seq_rows // (sub * n), 1)
    levels = [min(seq_rows, n) >> l for l in range(int(np.log2(min(seq_rows, n))))]
    mat = _scan_matrix(min(seq_rows, n), levels, with_suffix=n_seq > 1)
    heads = min(SCAN_HEADS, max(1, SCAN_STATES // n_seq))
    hg = REC_HEADS // heads
    wb = heads * LANES
    sect = lambda k: pl.BlockSpec((D_MODEL, wb), lambda g, b, c: (0, k * hg + g))
    head_vec = lambda r: pl.BlockSpec((r, wb), lambda g, b, c: (0, g))
    state_block = (n_seq, heads, REC_DK, REC_DV)
    state = pl.BlockSpec(state_block, lambda g, b, c: (b, g, 0, 0))
    state_in = [] if s0 is None else [pl.BlockSpec((None,) + state_block, lambda g, b, c: (layer, b, g, 0, 0))]
    groups = n_state // n_seq
    cast = _SideCast(cast_items, hg * groups * chunks, lambda g, b, c: (g * groups + b) * chunks + c)
    a, state_out, *casted = pl.pallas_call(
        functools.partial(_scan_kernel, layer=layer, n_seq=n_seq, levels=levels, has_s0=s0 is not None,
                          n_cast=len(cast_items)),
        grid=(hg, groups, chunks),
        in_specs=[
            pl.BlockSpec((sub * n, D_MODEL), lambda g, b, c: (b * chunks + c, 0)),
            sect(0), sect(1), sect(2), sect(3),
            head_vec(lb_logits.shape[0]), head_vec(1),
            pl.BlockSpec(mat.shape, lambda g, b, c: (0, 0)),
        ] + state_in + cast.in_specs,
        out_specs=[pl.BlockSpec((sub * n, wb), lambda g, b, c: (b * chunks + c, g)), state] + cast.out_specs,
        out_shape=[
            jax.ShapeDtypeStruct((rows, D_MODEL), BF16),
            jax.ShapeDtypeStruct((n_state, REC_HEADS, REC_DK, REC_DV), F32),
        ] + cast.out_shape,
        scratch_shapes=[pltpu.VMEM((sub, n, wb), F32)],
        compiler_params=_compiler_params(("parallel", "parallel", "arbitrary"), 56),
        name="hgrn_scan",
    )(h, w_in, w_in, w_in, w_in, lb_logits, o_gain.reshape(1, D_MODEL), mat, *([] if s0 is None else [s0]),
      *cast.stacks)
    return a, state_out, casted


def _outproj_kernel(a_ref, x_ref, g_ref, w_ref, o_ref):
    y = jnp.dot(a_ref[...].astype(BF16), w_ref[...], preferred_element_type=F32)
    o_ref[...] = x_ref[...] + g_ref[0] * y


def _outproj(a, x, mod, w, tm):
    rows = x.shape[0]
    row = lambda i: (i, 0)
    return pl.pallas_call(
        _outproj_kernel,
        grid=(rows // tm,),
        in_specs=[
            pl.BlockSpec((tm, D_MODEL), row),
            pl.BlockSpec((tm, D_MODEL), row),
            mod.spec(GATE1, tm),
            pl.BlockSpec((D_MODEL, D_MODEL), lambda i: (0, 0)),
        ],
        out_specs=pl.BlockSpec((tm, D_MODEL), row),
        out_shape=jax.ShapeDtypeStruct((rows, D_MODEL), F32),
        compiler_params=_compiler_params(("parallel",), 48),
        name="outproj",
    )(a, x, mod.table, w)


def _mlp_kernel(x_ref, gain_ref, sc_ref, sh_ref, xo_ref, g_ref, wu_ref, wd_ref, o_ref, h_scr, u_scr, *, n_up, tf):
    j = pl.program_id(1)

    @pl.when(j == 0)
    def _():
        h_scr[...] = _modnorm(x_ref[...], gain_ref[...], sc_ref[0], sh_ref[0]).astype(BF16)

    @pl.when(j < n_up)
    def _():
        u = jnp.maximum(jnp.dot(h_scr[...], wu_ref[...], preferred_element_type=F32), 0.0)
        u2 = (u * u).astype(BF16)
        for c in range(n_up):
            @pl.when(j == c)
            def _():
                u_scr[:, c * tf:(c + 1) * tf] = u2

    @pl.when(j >= n_up)
    def _():
        d = jnp.dot(u_scr[...], wd_ref[...], preferred_element_type=F32)
        o_ref[...] = xo_ref[...] + g_ref[0] * d


MLP_UP_BLOCK = 2048
MLP_DOWN_BLOCK = 512


def _mlp(x, gain, mod, w_up, w_down, tm, tf=MLP_UP_BLOCK, tn=MLP_DOWN_BLOCK):
    rows = x.shape[0]
    n_up, n_down = D_FF // tf, D_MODEL // tn
    row = lambda i, j: (i, 0)
    out_col = lambda i, j: jnp.maximum(j - n_up, 0)
    return pl.pallas_call(
        functools.partial(_mlp_kernel, n_up=n_up, tf=tf),
        grid=(rows // tm, n_up + n_down),
        in_specs=[
            pl.BlockSpec((tm, D_MODEL), row),
            pl.BlockSpec((1, D_MODEL), lambda i, j: (0, 0)),
            mod.spec(SCALE2, tm), mod.spec(SHIFT2, tm),
            pl.BlockSpec((tm, tn), lambda i, j: (i, out_col(i, j))),
            mod.spec(GATE2, tm, tn, out_col),
            pl.BlockSpec((D_MODEL, tf), lambda i, j: (0, jnp.minimum(j, n_up - 1))),
            pl.BlockSpec((D_FF, tn), lambda i, j: (0, out_col(i, j))),
        ],
        out_specs=pl.BlockSpec((tm, tn), lambda i, j: (i, out_col(i, j))),
        out_shape=jax.ShapeDtypeStruct((rows, D_MODEL), F32),
        scratch_shapes=[pltpu.VMEM((tm, D_MODEL), BF16), pltpu.VMEM((tm, D_FF), BF16)],
        compiler_params=_compiler_params(("parallel", "arbitrary"), 60),
        name="mlp",
    )(x, gain, mod.table, mod.table, x, mod.table, w_up, w_down)


def _row_tile(rows):
    return min(rows, 512)


def _attn_prompt_layer(x, mod, gain_mix, w_qkv, q_gain, k_gain, sinks, w_o_f32, layer, batch, seq,
                       qkv_cast, attn_cast):
    tm = _row_tile(batch * seq)
    tables = _rope_tables(jnp.arange(seq, dtype=jnp.int32))
    q, k, v, qkv_casted = _qkv(x, gain_mix, mod, w_qkv, q_gain, k_gain, tables, seq, tm, BF16, qkv_cast)
    o, (w_o, *attn_casted) = _attn_prompt(q, k, v, sinks, batch, seq, [(w_o_f32, layer)] + list(attn_cast))
    window = lambda t: t.reshape(batch, seq, KV_WIDTH)[:, seq - WINDOW:].reshape(batch, WINDOW, N_KV_HEADS, HEAD_DIM)
    return _outproj(o, x, mod, w_o, tm), window(k), window(v), w_o, qkv_casted, attn_casted


def _attn_sample_layer(x, mod, gain_mix, w_qkv, q_gain, k_gain, sinks, w_o, layer, batch, seq, cache_k, cache_v):
    rows = batch * seq
    tm = _row_tile(rows)
    pos = PAST_LEN + jnp.arange(seq, dtype=jnp.int32)
    tables = tuple(jnp.tile(t, (batch, 1)) for t in _rope_tables(pos))
    q, k, v, _ = _qkv(x, gain_mix, mod, w_qkv, q_gain, k_gain, tables, rows, tm, F32)
    ck = cache_k.reshape(-1, batch, WINDOW, KV_WIDTH)
    cv = cache_v.reshape(-1, batch, WINDOW, KV_WIDTH)
    o, k_win, v_win = _attn_sample(q, k, v, ck, cv, layer, sinks, batch, seq)
    k_win = k_win.reshape(batch, WINDOW, N_KV_HEADS, HEAD_DIM)
    v_win = v_win.reshape(batch, WINDOW, N_KV_HEADS, HEAD_DIM)
    return _outproj(o, x, mod, w_o, tm), k_win, v_win


def _hgrn_layer(x, mod, gain_mix, w_in, lb_logits, o_gain, w_o, s0, layer, batch, seq, cast_items=()):
    tm = _row_tile(batch * seq)
    h = _norm_rows(x, gain_mix, mod, tm)
    a, state, casted = _hgrn_scan(h, w_in, lb_logits, o_gain, s0, layer, batch, seq, cast_items)
    if w_o is None:
        w_o, *casted = casted
    return _outproj(a, x, mod, w_o, tm), state, w_o, casted


def kernel(x_prompt, x_sample, cache_k_win, cache_v_win, state_hgrn, c_prompt, c_sample, norm_gain, w_ada, b_ada,
           attn_w_qkv, attn_q_gain, attn_k_gain, attn_sinks, attn_w_o, rec_w_in, rec_lb_logits, rec_o_gain,
           rec_w_o, mlp_w_up, mlp_w_down):
    bp, lp, _ = x_prompt.shape
    bs, ls, _ = x_sample.shape
    n_c = bp + bs
    c_all = jnp.concatenate([c_prompt, c_sample, jnp.zeros((-n_c % 16, D_MODEL), F32)], axis=0)
    mods = _ada_mods(c_all, w_ada, b_ada)

    xp = x_prompt.reshape(bp * lp, D_MODEL)
    xs = x_sample.reshape(bs * ls, D_MODEL)
    lb_logits = rec_lb_logits.astype(F32)
    table_p = mods[:, :bp].reshape(DEPTH, bp, 1, 6 * D_MODEL)
    table_s = jnp.repeat(mods[:, bp:n_c], ls, axis=1).reshape(DEPTH, 1, bs * ls, 6 * D_MODEL)
    w_qkv = attn_w_qkv[0].astype(BF16)
    kwp, vwp, kws, vws, sp, ss = [], [], [], [], [], []
    for i in range(DEPTH):
        j = i // N_MIXERS
        mod_p = _Mod(table_p, i, lp, per_row=False)
        mod_s = _Mod(table_s, i, ls, per_row=True)
        gain_mix = norm_gain[i, 0].reshape(1, D_MODEL)
        gain_mlp = norm_gain[i, 1].reshape(1, D_MODEL)
        mlp_cast = [(mlp_w_up, i), (mlp_w_down, i)]
        if i % N_MIXERS == 0:
            aw = (w_qkv, attn_q_gain[j], attn_k_gain[j], attn_sinks[j])
            xp, kp, vp, w_o, (w_in,), (w_up, w_down) = _attn_prompt_layer(
                xp, mod_p, gain_mix, *aw, attn_w_o, j, bp, lp, [(rec_w_in, j)], mlp_cast)
            xs, kn, vn = _attn_sample_layer(xs, mod_s, gain_mix, *aw, w_o, j, bs, ls, cache_k_win, cache_v_win)
            kwp.append(kp); vwp.append(vp); kws.append(kn); vws.append(vn)
        else:
            next_qkv = [(attn_w_qkv, j + 1)] if j + 1 < attn_w_qkv.shape[0] else []
            rw = (w_in, lb_logits, rec_o_gain[j])
            xp, s_p, w_o, (w_up, w_down, *w_next) = _hgrn_layer(
                xp, mod_p, gain_mix, *rw, None, None, j, bp, lp, [(rec_w_o, j)] + mlp_cast + next_qkv)
            xs, s_s, _, _ = _hgrn_layer(xs, mod_s, gain_mix, *rw, w_o, state_hgrn, j, bs, ls)
            sp.append(s_p); ss.append(s_s)
            if w_next:
                w_qkv = w_next[0]
        xp = _mlp(xp, gain_mlp, mod_p, w_up, w_down, _row_tile(bp * lp))
        xs = _mlp(xs, gain_mlp, mod_s, w_up, w_down, _row_tile(bs * ls))
    return (xp.reshape(bp, lp, D_MODEL), xs.reshape(bs, ls, D_MODEL),
            jnp.stack(kwp), jnp.stack(vwp), jnp.stack(kws), jnp.stack(vws), jnp.stack(sp), jnp.stack(ss))
```

```python
import functools

import numpy as np
import jax
import jax.numpy as jnp
from jax import lax
from jax.experimental import pallas as pl
from jax.experimental.pallas import tpu as pltpu

F32 = jnp.float32
BF16 = jnp.bfloat16

D_MODEL = 2048
DEPTH = 4
N_MIXERS = 2
PAST_LEN = 16384
HEAD_DIM = 64
N_Q_HEADS = D_MODEL // HEAD_DIM
N_KV_HEADS = N_Q_HEADS // 8
GQA_GROUP = N_Q_HEADS // N_KV_HEADS
KV_WIDTH = N_KV_HEADS * HEAD_DIM
WINDOW = 128
ROT_DIM = HEAD_DIM // 4
ROPE_THETA = 500000.0
REC_HEADS = 16
REC_DK = 128
REC_DV = D_MODEL // REC_HEADS
D_FF = 4 * D_MODEL
EPS = 1e-6

LANES = 128
BF16_SUBLANES = 16
MIB = 1024 * 1024
SCAN_ROWS = 128
SCAN_CHUNKS = 2
SCAN_HEADS = 8
SCAN_STATES = 64


def _compiler_params(semantics, vmem_mib):
    return pltpu.CompilerParams(dimension_semantics=semantics, vmem_limit_bytes=vmem_mib * MIB)


def _silu(x):
    half = 0.5 * x
    return half + half * jnp.tanh(half)


def _modnorm(x, gain, scale, shift):
    var = jnp.mean(x * x, axis=-1, keepdims=True)
    y = x * lax.rsqrt(var + EPS) * gain
    return y * (1.0 + scale) + shift


def _split_bf16(x, parts):
    out = []
    for _ in range(parts - 1):
        hi = x.astype(BF16)
        out.append(hi)
        x = x - hi.astype(F32)
    out.append(x.astype(BF16))
    return out


SHIFT1, SCALE1, GATE1, SHIFT2, SCALE2, GATE2 = range(6)


class _Mod:
    def __init__(self, table, layer, rows_per_group, per_row):
        self.table, self.layer, self.rows_per_group, self.per_row = table, layer, rows_per_group, per_row

    def spec(self, chunk, tm, tn=D_MODEL, col=lambda *_: 0):
        layer, first = self.layer, chunk * (D_MODEL // tn)
        if self.per_row:
            return pl.BlockSpec((None, 1, tm, tn), lambda i, *rest: (layer, 0, i, first + col(i, *rest)))
        assert self.rows_per_group % tm == 0
        per = self.rows_per_group // tm
        return pl.BlockSpec((None, 1, 1, tn), lambda i, *rest: (layer, i // per, 0, first + col(i, *rest)))


class _SideCast:
    def __init__(self, items, n_steps, step_of):
        self.stacks = [item[0] for item in items]
        self.in_specs, self.out_specs, self.out_shape = [], [], []
        for w, layer, *width in items:
            _, rows, cols = w.shape
            assert rows % (n_steps * BF16_SUBLANES) == 0
            block = rows // n_steps
            self.in_specs.append(pl.BlockSpec((None, block, cols), lambda *g, layer=layer: (layer, step_of(*g), 0)))
            if width:
                n_blocks = cols // width[0]
                self.out_specs.append(pl.BlockSpec((n_blocks, block, width[0]), lambda *g: (0, step_of(*g), 0)))
                self.out_shape.append(jax.ShapeDtypeStruct((n_blocks, rows, width[0]), BF16))
            else:
                self.out_specs.append(pl.BlockSpec((block, cols), lambda *g: (step_of(*g), 0)))
                self.out_shape.append(jax.ShapeDtypeStruct((rows, cols), BF16))

    @staticmethod
    def run(src_refs, dst_refs):
        for src, dst in zip(src_refs, dst_refs):
            if len(dst.shape) == 2:
                dst[...] = src[...].astype(BF16)
            else:
                width = dst.shape[2]
                for c in range(dst.shape[0]):
                    dst[c] = src[:, c * width:(c + 1) * width].astype(BF16)


def _ada_kernel(c_ref, w_ref, b_ref, o_ref):
    s = _silu(c_ref[...]).astype(BF16)
    o_ref[0] = jnp.dot(s, w_ref[0].astype(BF16), preferred_element_type=F32) + b_ref[0]


def _ada_mods(c_all, w_ada, b_ada):
    rows = c_all.shape[0]
    tn = 1024
    n = 6 * D_MODEL
    return pl.pallas_call(
        _ada_kernel,
        grid=(DEPTH, n // tn),
        in_specs=[
            pl.BlockSpec((rows, D_MODEL), lambda l, j: (0, 0)),
            pl.BlockSpec((1, D_MODEL, tn), lambda l, j: (l, 0, j)),
            pl.BlockSpec((1, 1, tn), lambda l, j: (l, 0, j)),
        ],
        out_specs=pl.BlockSpec((1, rows, tn), lambda l, j: (l, 0, j)),
        out_shape=jax.ShapeDtypeStruct((DEPTH, rows, n), F32),
        compiler_params=_compiler_params(("parallel", "parallel"), 40),
        name="ada_mods",
    )(c_all, w_ada, b_ada.reshape(DEPTH, 1, n))


def _norm_rows_kernel(x_ref, gain_ref, sc_ref, sh_ref, o_ref):
    o_ref[...] = _modnorm(x_ref[...], gain_ref[...], sc_ref[0], sh_ref[0]).astype(o_ref.dtype)


def _norm_rows(x, gain, mod, tm):
    rows = x.shape[0]
    return pl.pallas_call(
        _norm_rows_kernel,
        grid=(rows // tm,),
        in_specs=[
            pl.BlockSpec((tm, D_MODEL), lambda i: (i, 0)),
            pl.BlockSpec((1, D_MODEL), lambda i: (0, 0)),
            mod.spec(SCALE1, tm),
            mod.spec(SHIFT1, tm),
        ],
        out_specs=pl.BlockSpec((tm, D_MODEL), lambda i: (i, 0)),
        out_shape=jax.ShapeDtypeStruct((rows, D_MODEL), BF16),
        compiler_params=_compiler_params(("parallel",), 32),
        name="norm_rows",
    )(x, gain, mod.table, mod.table)


def _qkv_kernel(x_ref, gain_ref, sc_ref, sh_ref, w_ref, qg_ref, kg_ref, cos_ref, s1_ref, s2_ref, *rest):
    n_cast = (len(rest) - 3) // 2
    q_ref, k_ref, v_ref = rest[n_cast:n_cast + 3]
    _SideCast.run(rest[:n_cast], rest[n_cast + 3:])
    h = _modnorm(x_ref[...], gain_ref[...], sc_ref[0], sh_ref[0]).astype(BF16)
    acc = jnp.dot(h, w_ref[...], preferred_element_type=F32)
    cos, s1, s2 = cos_ref[...], s1_ref[...], s2_ref[...]
    qk_width = D_MODEL + KV_WIDTH
    first_head = lax.broadcasted_iota(jnp.int32, (x_ref.shape[0], LANES), 1) < HEAD_DIM
    for c in range(qk_width // LANES):
        xc = acc[:, c * LANES:(c + 1) * LANES]
        x2 = xc * xc
        ssq = jnp.where(first_head,
                        jnp.sum(jnp.where(first_head, x2, 0.0), axis=-1, keepdims=True),
                        jnp.sum(jnp.where(first_head, 0.0, x2), axis=-1, keepdims=True))
        is_q = c * LANES < D_MODEL
        y = xc * lax.rsqrt(ssq * (1.0 / HEAD_DIM) + EPS) * (qg_ref[...] if is_q else kg_ref[...])
        r = y * cos + pltpu.roll(y, LANES - ROT_DIM // 2, 1) * s1 + pltpu.roll(y, ROT_DIM // 2, 1) * s2
        if is_q:
            q_ref[:, c * LANES:(c + 1) * LANES] = (r * (HEAD_DIM ** -0.5)).astype(q_ref.dtype)
        else:
            k_ref[:, c * LANES - D_MODEL:(c + 1) * LANES - D_MODEL] = r
    v_ref[...] = acc[:, qk_width:]


def _rope_tables(pos):
    half = ROT_DIM // 2
    inv = ROPE_THETA ** (-jnp.arange(half, dtype=F32) / half)
    ang = pos.astype(F32)[:, None] * inv[None, :]
    cos, sin = jnp.cos(ang), jnp.sin(ang)
    n = pos.shape[0]
    pad = jnp.zeros((n, HEAD_DIM - ROT_DIM), F32)
    zero = jnp.zeros((n, half), F32)
    c = jnp.concatenate([cos, cos, pad + 1.0], axis=1)
    s1 = jnp.concatenate([-sin, zero, pad], axis=1)
    s2 = jnp.concatenate([zero, sin, pad], axis=1)
    return tuple(jnp.tile(t, (1, LANES // HEAD_DIM)) for t in (c, s1, s2))


def _qkv(x, gain, mod, w, q_gain, k_gain, tables, table_rows, tm, q_dtype, cast_items=()):
    rows = x.shape[0]
    n = w.shape[-1]
    per = table_rows // tm
    tspec = pl.BlockSpec((tm, LANES), lambda i: (i % per, 0))
    vspec = pl.BlockSpec((1, LANES), lambda i: (0, 0))
    reps = LANES // HEAD_DIM
    cast = _SideCast(cast_items, rows // tm, lambda i: i)
    q, k, v, *casted = pl.pallas_call(
        _qkv_kernel,
        grid=(rows // tm,),
        in_specs=[
            pl.BlockSpec((tm, D_MODEL), lambda i: (i, 0)),
            pl.BlockSpec((1, D_MODEL), lambda i: (0, 0)),
            mod.spec(SCALE1, tm),
            mod.spec(SHIFT1, tm),
            pl.BlockSpec((D_MODEL, n), lambda i: (0, 0)),
            vspec, vspec, tspec, tspec, tspec,
        ] + cast.in_specs,
        out_specs=[
            pl.BlockSpec((tm, D_MODEL), lambda i: (i, 0)),
            pl.BlockSpec((tm, KV_WIDTH), lambda i: (i, 0)),
            pl.BlockSpec((tm, KV_WIDTH), lambda i: (i, 0)),
        ] + cast.out_specs,
        out_shape=[
            jax.ShapeDtypeStruct((rows, D_MODEL), q_dtype),
            jax.ShapeDtypeStruct((rows, KV_WIDTH), F32),
            jax.ShapeDtypeStruct((rows, KV_WIDTH), F32),
        ] + cast.out_shape,
        compiler_params=_compiler_params(("parallel",), 56),
        name="qkv",
    )(x, gain, mod.table, mod.table, w,
      jnp.tile(q_gain.reshape(1, HEAD_DIM), (1, reps)), jnp.tile(k_gain.reshape(1, HEAD_DIM), (1, reps)),
      *tables, *cast.stacks)
    return q, k, v, casted


def _softmax_with_sink(s, sink):
    m = jnp.maximum(jnp.max(s, axis=-1, keepdims=True), sink)
    p = jnp.exp(s - m)
    denom = jnp.sum(p, axis=-1, keepdims=True) + jnp.exp(sink - m)
    return p, denom


def _attn_prompt_kernel(sink_ref, q_ref, kp_ref, kc_ref, vp_ref, vc_ref, *rest):
    n_cast = len(rest) // 2
    o_ref = rest[n_cast]
    _SideCast.run(rest[:n_cast], rest[n_cast + 1:])
    j = pl.program_id(1)
    kk = jnp.concatenate([kp_ref[...], kc_ref[...]], axis=0).astype(BF16)
    vt = jnp.concatenate([vp_ref[...], vc_ref[...]], axis=0).T.astype(BF16)
    s = lax.broadcasted_iota(jnp.int32, (2 * WINDOW, WINDOW), 0)
    t = lax.broadcasted_iota(jnp.int32, (2 * WINDOW, WINDOW), 1)
    ok = (s > t) & (s <= t + WINDOW) & (s >= jnp.where(j > 0, 0, WINDOW))
    bias = jnp.where(ok, 0.0, -jnp.inf)
    zk = jnp.zeros((2 * WINDOW, HEAD_DIM), BF16)
    zv = jnp.zeros((HEAD_DIM, 2 * WINDOW), BF16)
    for g in range(N_KV_HEADS):
        kg = kk[:, g * HEAD_DIM:(g + 1) * HEAD_DIM]
        vg = vt[g * HEAD_DIM:(g + 1) * HEAD_DIM]
        k2 = jnp.concatenate([jnp.concatenate([kg, zk], axis=1), jnp.concatenate([zk, kg], axis=1)], axis=0)
        v2 = jnp.concatenate([jnp.concatenate([vg, zv], axis=1), jnp.concatenate([zv, vg], axis=1)], axis=0)
        for hp in range(GQA_GROUP // 2):
            h0 = g * GQA_GROUP + 2 * hp
            lanes = slice(h0 * HEAD_DIM, (h0 + 2) * HEAD_DIM)
            st = lax.dot_general(k2, q_ref[:, lanes], (((1,), (1,)), ((), ())), preferred_element_type=F32)
            probs = []
            for i in range(2):
                sh = st[i * 2 * WINDOW:(i + 1) * 2 * WINDOW] + bias
                sink = sink_ref[h0 + i]
                m = jnp.maximum(jnp.max(sh, axis=0, keepdims=True), sink)
                p = jnp.exp(sh - m)
                denom = jnp.sum(p, axis=0, keepdims=True) + jnp.exp(sink - m)
                probs.append((p * (1.0 / denom)).astype(BF16))
            ot = jnp.dot(v2, jnp.concatenate(probs, axis=0), preferred_element_type=F32)
            o_ref[:, lanes] = ot.T.astype(o_ref.dtype)


def _attn_prompt(q, k, v, sinks, batch, seq, cast_items=()):
    nb = seq // WINDOW
    rows = batch * seq
    cur = lambda b, j: (b * nb + j, 0)
    prev = lambda b, j: (b * nb + jnp.maximum(j - 1, 0), 0)
    kv_block = (WINDOW, KV_WIDTH)
    cast = _SideCast(cast_items, batch * nb, lambda b, j: b * nb + j)
    o, *casted = pl.pallas_call(
        _attn_prompt_kernel,
        grid=(batch, nb),
        in_specs=[
            pl.BlockSpec(memory_space=pltpu.SMEM),
            pl.BlockSpec((WINDOW, D_MODEL), cur),
            pl.BlockSpec(kv_block, prev), pl.BlockSpec(kv_block, cur),
            pl.BlockSpec(kv_block, prev), pl.BlockSpec(kv_block, cur),
        ] + cast.in_specs,
        out_specs=[pl.BlockSpec((WINDOW, D_MODEL), cur)] + cast.out_specs,
        out_shape=[jax.ShapeDtypeStruct((rows, D_MODEL), BF16)] + cast.out_shape,
        compiler_params=_compiler_params(("parallel", "parallel"), 40),
        name="attn_prompt",
    )(sinks, q, k, k, v, v, *cast.stacks)
    return o, casted


SAMPLE_SEQS = 4


def _attn_sample_kernel(sink_ref, q_ref, k_ref, v_ref, ck_ref, cv_ref, o_ref, nk_ref, nv_ref, *, steps):
    rows = GQA_GROUP * steps
    assert steps & (steps - 1) == 0
    t = lax.broadcasted_iota(jnp.int32, (rows, 2 * WINDOW), 0) & (steps - 1)
    s = lax.broadcasted_iota(jnp.int32, (rows, 2 * WINDOW), 1)
    ok = ((s < WINDOW) & (s > t)) | ((s >= WINDOW) & (s - WINDOW <= t))
    zpad = jnp.zeros((WINDOW - steps, KV_WIDTH), F32)
    for b in range(ck_ref.shape[0]):
        new = slice(b * steps, (b + 1) * steps)
        kc, vc = ck_ref[b], cv_ref[b]
        kn, vn = k_ref[new], v_ref[new]
        nk_ref[b, :WINDOW - steps] = kc[steps:]
        nk_ref[b, WINDOW - steps:] = kn
        nv_ref[b, :WINDOW - steps] = vc[steps:]
        nv_ref[b, WINDOW - steps:] = vn
        kk = jnp.concatenate([kc, kn, zpad], axis=0).astype(BF16)
        vv = jnp.concatenate([vc, vn, zpad], axis=0).astype(BF16)
        q = q_ref[new]
        for g in range(N_KV_HEADS):
            heads = range(g * GQA_GROUP, (g + 1) * GQA_GROUP)
            qg = jnp.concatenate([q[:, h * HEAD_DIM:(h + 1) * HEAD_DIM] for h in heads], axis=0).astype(BF16)
            sc = lax.dot_general(qg, kk[:, g * HEAD_DIM:(g + 1) * HEAD_DIM], (((1,), (1,)), ((), ())),
                                 preferred_element_type=F32)
            p, denom = _softmax_with_sink(jnp.where(ok, sc, -jnp.inf), sink_ref[g])
            og = jnp.dot(p.astype(BF16), vv[:, g * HEAD_DIM:(g + 1) * HEAD_DIM],
                         preferred_element_type=F32) / denom
            for i, h in enumerate(heads):
                o_ref[new, h * HEAD_DIM:(h + 1) * HEAD_DIM] = og[i * steps:(i + 1) * steps]


def _attn_sample(q, k, v, cache_k, cache_v, layer, sinks, batch, steps):
    rows = batch * steps
    sink_rows = jnp.repeat(sinks.reshape(N_KV_HEADS, GQA_GROUP), steps, axis=1)[..., None]
    seqs = min(SAMPLE_SEQS, batch)
    assert batch % seqs == 0
    steps_blk = seqs * steps
    row = lambda b: (b, 0)
    win = lambda b: (b, 0, 0)
    cache_block = (seqs, WINDOW, KV_WIDTH)
    return pl.pallas_call(
        functools.partial(_attn_sample_kernel, steps=steps),
        grid=(batch // seqs,),
        in_specs=[
            pl.BlockSpec((N_KV_HEADS, GQA_GROUP * steps, 1), lambda b: (0, 0, 0)),
            pl.BlockSpec((steps_blk, D_MODEL), row),
            pl.BlockSpec((steps_blk, KV_WIDTH), row), pl.BlockSpec((steps_blk, KV_WIDTH), row),
            pl.BlockSpec((None,) + cache_block, lambda b: (layer, b, 0, 0)),
            pl.BlockSpec((None,) + cache_block, lambda b: (layer, b, 0, 0)),
        ],
        out_specs=[
            pl.BlockSpec((steps_blk, D_MODEL), row),
            pl.BlockSpec(cache_block, win), pl.BlockSpec(cache_block, win),
        ],
        out_shape=[
            jax.ShapeDtypeStruct((rows, D_MODEL), F32),
            jax.ShapeDtypeStruct((batch, WINDOW, KV_WIDTH), F32),
            jax.ShapeDtypeStruct((batch, WINDOW, KV_WIDTH), F32),
        ],
        compiler_params=_compiler_params(("parallel",), 32),
        name="attn_sample",
    )(sink_rows, q, k, v, cache_k, cache_v)


SUBLANES = 8


def _levels_from_matrix(levels):
    return [b for b in levels if 2 < b <= SUBLANES]


def _scan_matrix(seq_rows, levels, with_suffix):
    n = SCAN_ROWS
    t = np.arange(n)[:, None]
    r = np.arange(n)[None, :]
    same = (t // seq_rows) == (r // seq_rows)
    mats = [same & (r <= t)] + ([same & (r > t)] if with_suffix else [])
    for b in _levels_from_matrix(levels):
        mid = (t // b) * b + b // 2
        upper = (t & (b // 2)) != 0
        mats.append(np.where(upper, (r >= mid) & (r <= t), (r > t) & (r < mid)))
    return jnp.asarray(np.concatenate(mats, axis=0), dtype=BF16)


def _scan_kernel(h_ref, wq_ref, wf_ref, wi_ref, wg_ref, lbl_ref, og_ref, mat_ref, *rest,
                 layer, n_seq, levels, has_s0, n_cast):
    rest = list(rest)
    s0_ref = rest.pop(0) if has_s0 else None
    cast_src, (a_ref, st_ref), cast_dst, g_scr = rest[:n_cast], rest[n_cast:n_cast + 2], rest[n_cast + 2:-1], rest[-1]
    _SideCast.run(cast_src, cast_dst)
    n = SCAN_ROWS
    heads = st_ref.shape[1]
    width = heads * LANES
    seq_rows = n // n_seq
    chunk = pl.program_id(2)

    @pl.when(chunk == 0)
    def _():
        for s in range(n_seq):
            for h in range(heads):
                st_ref[s, h] = jnp.zeros((REC_DV, REC_DK), F32) if s0_ref is None else s0_ref[s, h].T

    logits = lbl_ref[...]
    pexp = jnp.exp(logits - jnp.max(logits, axis=0, keepdims=True))
    psum = jnp.sum(pexp, axis=0, keepdims=True)
    lb = jnp.zeros_like(psum)
    for i in range(1, layer + 1):
        lb = lb + pexp[i:i + 1] / psum

    hb = h_ref[...]
    project = lambda w_ref: jnp.dot(hb, w_ref[...], preferred_element_type=F32)
    qs_all = _silu(project(wq_ref))
    fz = project(wf_ref)
    e = jnp.exp(-jnp.abs(fz))
    r = 1.0 / (1.0 + e)
    pos = fz >= 0
    f_all = lb + (1.0 - lb) * jnp.where(pos, r, e * r)
    logf_all = jnp.log(f_all)
    kin_all = (1.0 - lb) * jnp.where(pos, e * r, r)
    val_all = project(wi_ref)
    gate_all = _silu(project(wg_ref))
    mat = mat_ref[...]
    t_row = lax.broadcasted_iota(jnp.int32, (n, width), 0)
    t_lane = lax.broadcasted_iota(jnp.int32, (n, LANES), 0)
    tt = lax.broadcasted_iota(jnp.int32, (n, n), 0)
    ss = lax.broadcasted_iota(jnp.int32, (n, n), 1)
    same_block = {b: (tt ^ ss) < b for b in levels if b < n}
    for u in range(g_scr.shape[0]):
        rows = slice(u * n, (u + 1) * n)
        _scan_chunk(qs_all[rows], f_all[rows], logf_all[rows], kin_all[rows], val_all[rows], gate_all[rows], mat,
                    og_ref, g_scr.at[u], st_ref, a_ref.at[rows], t_row, t_lane, same_block, n_seq, levels)

    @pl.when(chunk == pl.num_programs(2) - 1)
    def _():
        for s in range(n_seq):
            for h in range(heads):
                st_ref[s, h] = st_ref[s, h].T


def _scan_chunk(qs, f, logf, kin, val, gate, mat, og_ref, g_scr, st_ref, a_ref, t_row, t_lane, same_block,
                n_seq, levels):
    n = SCAN_ROWS
    heads = st_ref.shape[1]
    width = heads * LANES
    seq_rows = n // n_seq
    sums = None
    for part in _split_bf16(logf, 2):
        d = jnp.dot(mat, part, preferred_element_type=F32)
        sums = d if sums is None else sums + d
    gcum = sums[:n]
    g_scr[...] = gcum
    if n_seq == 1:
        gtail = g_scr[n - 1:n, :] - gcum
        block = 1
    else:
        gtail = sums[n:2 * n]
        block = 2
    level_sums = {b: sums[(block + i) * n:(block + i + 1) * n] for i, b in enumerate(_levels_from_matrix(levels))}

    q_lv, k_lv = [], []
    for b in levels:
        half = b // 2
        if b > SUBLANES:
            zero = jnp.zeros((half, width), F32)
            qparts, kparts = [], []
            for lo in range(0, n, b):
                mid, hi = lo + half, lo + b
                ref = g_scr[mid - 1:mid, :]
                kparts += [kin[lo:mid] * jnp.exp(ref - gcum[lo:mid]), zero]
                qparts += [zero, qs[mid:hi] * jnp.exp(gcum[mid:hi] - ref)]
            qt, kt = jnp.concatenate(qparts, axis=0), jnp.concatenate(kparts, axis=0)
        else:
            upper = (t_row & half) != 0
            if b == 2:
                qt, kt = jnp.where(upper, qs * f, 0.0), jnp.where(upper, 0.0, kin)
            else:
                decay = jnp.exp(level_sums[b])
                qt, kt = jnp.where(upper, qs * decay, 0.0), jnp.where(upper, 0.0, kin * decay)
        q_lv.append(qt.astype(BF16))
        k_lv.append(kt.astype(BF16))
    qe_all = qs * jnp.exp(gcum)
    kt_all = kin * jnp.exp(gtail)
    contract_lanes = (((1,), (1,)), ((), ()))
    contract_rows = (((0,), (0,)), ((), ()))

    for h in range(heads):
        sl = slice(h * LANES, (h + 1) * LANES)
        vh = val[:, sl]
        vb = vh.astype(BF16)
        a = None
        for l, b in enumerate(levels):
            al = lax.dot_general(q_lv[l][:, sl], k_lv[l][:, sl], contract_lanes, preferred_element_type=F32)
            if b == n:
                a = al
            else:
                a = jnp.where(same_block[b], al, 0.0 if a is None else a)
        o = jnp.dot(a.astype(BF16), vb, preferred_element_type=F32)
        o = o + jnp.sum(qs[:, sl] * kin[:, sl], axis=-1, keepdims=True) * vh
        qe, kt = qe_all[:, sl], kt_all[:, sl]
        for s in range(n_seq):
            last = (s + 1) * seq_rows - 1
            if n_seq == 1:
                qe_s, kt_s = qe, kt
            else:
                inside = (t_lane >= s * seq_rows) & (t_lane <= last)
                qe_s, kt_s = jnp.where(inside, qe, 0.0), jnp.where(inside, kt, 0.0)
            st = st_ref[s, h]
            o = o + lax.dot_general(qe_s.astype(BF16), st.astype(BF16), contract_lanes,
                                    preferred_element_type=F32)
            st_ref[s, h] = st * jnp.exp(g_scr[last:last + 1, sl]) + lax.dot_general(
                vb, kt_s.astype(BF16), contract_rows, preferred_element_type=F32)
        rs = lax.rsqrt(jnp.mean(o * o, axis=-1, keepdims=True) + EPS)
        a_ref[:, sl] = (o * rs * og_ref[:, sl] * gate[:, sl]).astype(a_ref.dtype)


def _hgrn_scan(h, w_in, lb_logits, o_gain, s0, layer, n_state, seq_rows, cast_items=()):
    rows = h.shape[0]
    n = SCAN_ROWS
    n_seq = max(n // seq_rows, 1)
    sub = SCAN_CHUNKS if seq_rows % (SCAN_CHUNKS * n) == 0 else 1
    chunks = max(seq_rows // (sub * n), 1)
    levels = [min(seq_rows, n) >> l for l in range(int(np.log2(min(seq_rows, n))))]
    mat = _scan_matrix(min(seq_rows, n), levels, with_suffix=n_seq > 1)
    heads = min(SCAN_HEADS, max(1, SCAN_STATES // n_seq))
    hg = REC_HEADS // heads
    wb = heads * LANES
    sect = lambda k: pl.BlockSpec((D_MODEL, wb), lambda g, b, c: (0, k * hg + g))
    head_vec = lambda r: pl.BlockSpec((r, wb), lambda g, b, c: (0, g))
    state_block = (n_seq, heads, REC_DK, REC_DV)
    state = pl.BlockSpec(state_block, lambda g, b, c: (b, g, 0, 0))
    state_in = [] if s0 is None else [pl.BlockSpec((None,) + state_block, lambda g, b, c: (layer, b, g, 0, 0))]
    groups = n_state // n_seq
    cast = _SideCast(cast_items, hg * groups * chunks, lambda g, b, c: (g * groups + b) * chunks + c)
    a, state_out, *casted = pl.pallas_call(
        functools.partial(_scan_kernel, layer=layer, n_seq=n_seq, levels=levels, has_s0=s0 is not None,
                          n_cast=len(cast_items)),
        grid=(hg, groups, chunks),
        in_specs=[
            pl.BlockSpec((sub * n, D_MODEL), lambda g, b, c: (b * chunks + c, 0)),
            sect(0), sect(1), sect(2), sect(3),
            head_vec(lb_logits.shape[0]), head_vec(1),
            pl.BlockSpec(mat.shape, lambda g, b, c: (0, 0)),
        ] + state_in + cast.in_specs,
        out_specs=[pl.BlockSpec((sub * n, wb), lambda g, b, c: (b * chunks + c, g)), state] + cast.out_specs,
        out_shape=[
            jax.ShapeDtypeStruct((rows, D_MODEL), BF16),
            jax.ShapeDtypeStruct((n_state, REC_HEADS, REC_DK, REC_DV), F32),
        ] + cast.out_shape,
        scratch_shapes=[pltpu.VMEM((sub, n, wb), F32)],
        compiler_params=_compiler_params(("parallel", "parallel", "arbitrary"), 56),
        name="hgrn_scan",
    )(h, w_in, w_in, w_in, w_in, lb_logits, o_gain.reshape(1, D_MODEL), mat, *([] if s0 is None else [s0]),
      *cast.stacks)
    return a, state_out, casted


def _outproj_kernel(a_ref, x_ref, g_ref, w_ref, o_ref):
    y = jnp.dot(a_ref[...].astype(BF16), w_ref[...], preferred_element_type=F32)
    o_ref[...] = x_ref[...] + g_ref[0] * y


def _outproj(a, x, mod, w, tm):
    rows = x.shape[0]
    row = lambda i: (i, 0)
    return pl.pallas_call(
        _outproj_kernel,
        grid=(rows // tm,),
        in_specs=[
            pl.BlockSpec((tm, D_MODEL), row),
            pl.BlockSpec((tm, D_MODEL), row),
            mod.spec(GATE1, tm),
            pl.BlockSpec((D_MODEL, D_MODEL), lambda i: (0, 0)),
        ],
        out_specs=pl.BlockSpec((tm, D_MODEL), row),
        out_shape=jax.ShapeDtypeStruct((rows, D_MODEL), F32),
        compiler_params=_compiler_params(("parallel",), 48),
        name="outproj",
    )(a, x, mod.table, w)


def _mlp_kernel(x_ref, gain_ref, sc_ref, sh_ref, xo_ref, g_ref, wu_ref, wd_ref, o_ref, h_scr, u_scr, *, n_up, tf):
    j = pl.program_id(1)

    @pl.when(j == 0)
    def _():
        h_scr[...] = _modnorm(x_ref[...], gain_ref[...], sc_ref[0], sh_ref[0]).astype(BF16)

    @pl.when(j < n_up)
    def _():
        u = jnp.maximum(jnp.dot(h_scr[...], wu_ref[...], preferred_element_type=F32), 0.0)
        u2 = (u * u).astype(BF16)
        for c in range(n_up):
            @pl.when(j == c)
            def _():
                u_scr[:, c * tf:(c + 1) * tf] = u2

    @pl.when(j >= n_up)
    def _():
        d = jnp.dot(u_scr[...], wd_ref[...], preferred_element_type=F32)
        o_ref[...] = xo_ref[...] + g_ref[0] * d


MLP_UP_BLOCK = 2048
MLP_DOWN_BLOCK = 512


def _mlp(x, gain, mod, w_up, w_down, tm):
    rows = x.shape[0]
    n_up, _, tf = w_up.shape
    n_down, _, tn = w_down.shape
    row = lambda i, j: (i, 0)
    out_col = lambda i, j: jnp.maximum(j - n_up, 0)
    return pl.pallas_call(
        functools.partial(_mlp_kernel, n_up=n_up, tf=tf),
        grid=(rows // tm, n_up + n_down),
        in_specs=[
            pl.BlockSpec((tm, D_MODEL), row),
            pl.BlockSpec((1, D_MODEL), lambda i, j: (0, 0)),
            mod.spec(SCALE2, tm), mod.spec(SHIFT2, tm),
            pl.BlockSpec((tm, tn), lambda i, j: (i, out_col(i, j))),
            mod.spec(GATE2, tm, tn, out_col),
            pl.BlockSpec((None, D_MODEL, tf), lambda i, j: (jnp.minimum(j, n_up - 1), 0, 0)),
            pl.BlockSpec((None, D_FF, tn), lambda i, j: (out_col(i, j), 0, 0)),
        ],
        out_specs=pl.BlockSpec((tm, tn), lambda i, j: (i, out_col(i, j))),
        out_shape=jax.ShapeDtypeStruct((rows, D_MODEL), F32),
        scratch_shapes=[pltpu.VMEM((tm, D_MODEL), BF16), pltpu.VMEM((tm, D_FF), BF16)],
        compiler_params=_compiler_params(("parallel", "arbitrary"), 60),
        name="mlp",
    )(x, gain, mod.table, mod.table, x, mod.table, w_up, w_down)


def _row_tile(rows):
    return min(rows, 512)


def _attn_prompt_layer(x, mod, gain_mix, w_qkv, q_gain, k_gain, sinks, w_o_f32, layer, batch, seq,
                       qkv_cast, attn_cast):
    tm = _row_tile(batch * seq)
    tables = _rope_tables(jnp.arange(seq, dtype=jnp.int32))
    q, k, v, qkv_casted = _qkv(x, gain_mix, mod, w_qkv, q_gain, k_gain, tables, seq, tm, BF16, qkv_cast)
    o, (w_o, *attn_casted) = _attn_prompt(q, k, v, sinks, batch, seq, [(w_o_f32, layer)] + list(attn_cast))
    window = lambda t: t.reshape(batch, seq, KV_WIDTH)[:, seq - WINDOW:].reshape(batch, WINDOW, N_KV_HEADS, HEAD_DIM)
    return _outproj(o, x, mod, w_o, tm), window(k), window(v), w_o, qkv_casted, attn_casted


def _attn_sample_layer(x, mod, gain_mix, w_qkv, q_gain, k_gain, sinks, w_o, layer, batch, seq, cache_k, cache_v):
    rows = batch * seq
    tm = _row_tile(rows)
    pos = PAST_LEN + jnp.arange(seq, dtype=jnp.int32)
    tables = tuple(jnp.tile(t, (batch, 1)) for t in _rope_tables(pos))
    q, k, v, _ = _qkv(x, gain_mix, mod, w_qkv, q_gain, k_gain, tables, rows, tm, F32)
    ck = cache_k.reshape(-1, batch, WINDOW, KV_WIDTH)
    cv = cache_v.reshape(-1, batch, WINDOW, KV_WIDTH)
    o, k_win, v_win = _attn_sample(q, k, v, ck, cv, layer, sinks, batch, seq)
    k_win = k_win.reshape(batch, WINDOW, N_KV_HEADS, HEAD_DIM)
    v_win = v_win.reshape(batch, WINDOW, N_KV_HEADS, HEAD_DIM)
    return _outproj(o, x, mod, w_o, tm), k_win, v_win


def _hgrn_layer(x, mod, gain_mix, w_in, lb_logits, o_gain, w_o, s0, layer, batch, seq, cast_items=()):
    tm = _row_tile(batch * seq)
    h = _norm_rows(x, gain_mix, mod, tm)
    a, state, casted = _hgrn_scan(h, w_in, lb_logits, o_gain, s0, layer, batch, seq, cast_items)
    if w_o is None:
        w_o, *casted = casted
    return _outproj(a, x, mod, w_o, tm), state, w_o, casted


def kernel(x_prompt, x_sample, cache_k_win, cache_v_win, state_hgrn, c_prompt, c_sample, norm_gain, w_ada, b_ada,
           attn_w_qkv, attn_q_gain, attn_k_gain, attn_sinks, attn_w_o, rec_w_in, rec_lb_logits, rec_o_gain,
           rec_w_o, mlp_w_up, mlp_w_down):
    bp, lp, _ = x_prompt.shape
    bs, ls, _ = x_sample.shape
    n_c = bp + bs
    c_all = jnp.concatenate([c_prompt, c_sample, jnp.zeros((-n_c % 16, D_MODEL), F32)], axis=0)
    mods = _ada_mods(c_all, w_ada, b_ada)

    xp = x_prompt.reshape(bp * lp, D_MODEL)
    xs = x_sample.reshape(bs * ls, D_MODEL)
    lb_logits = rec_lb_logits.astype(F32)
    table_p = mods[:, :bp].reshape(DEPTH, bp, 1, 6 * D_MODEL)
    table_s = jnp.repeat(mods[:, bp:n_c], ls, axis=1).reshape(DEPTH, 1, bs * ls, 6 * D_MODEL)
    w_qkv = attn_w_qkv[0].astype(BF16)
    kwp, vwp, kws, vws, sp, ss = [], [], [], [], [], []
    for i in range(DEPTH):
        j = i // N_MIXERS
        mod_p = _Mod(table_p, i, lp, per_row=False)
        mod_s = _Mod(table_s, i, ls, per_row=True)
        gain_mix = norm_gain[i, 0].reshape(1, D_MODEL)
        gain_mlp = norm_gain[i, 1].reshape(1, D_MODEL)
        mlp_cast = [(mlp_w_up, i, MLP_UP_BLOCK), (mlp_w_down, i, MLP_DOWN_BLOCK)]
        if i % N_MIXERS == 0:
            aw = (w_qkv, attn_q_gain[j], attn_k_gain[j], attn_sinks[j])
            xp, kp, vp, w_o, (w_in,), (w_up, w_down) = _attn_prompt_layer(
                xp, mod_p, gain_mix, *aw, attn_w_o, j, bp, lp, [(rec_w_in, j)], mlp_cast)
            xs, kn, vn = _attn_sample_layer(xs, mod_s, gain_mix, *aw, w_o, j, bs, ls, cache_k_win, cache_v_win)
            kwp.append(kp); vwp.append(vp); kws.append(kn); vws.append(vn)
        else:
            next_qkv = [(attn_w_qkv, j + 1)] if j + 1 < attn_w_qkv.shape[0] else []
            rw = (w_in, lb_logits, rec_o_gain[j])
            xp, s_p, w_o, (w_up, w_down, *w_next) = _hgrn_layer(
                xp, mod_p, gain_mix, *rw, None, None, j, bp, lp, [(rec_w_o, j)] + mlp_cast + next_qkv)
            xs, s_s, _, _ = _hgrn_layer(xs, mod_s, gain_mix, *rw, w_o, state_hgrn, j, bs, ls)
            sp.append(s_p); ss.append(s_s)
            if w_next:
                w_qkv = w_next[0]
        xp = _mlp(xp, gain_mlp, mod_p, w_up, w_down, _row_tile(bp * lp))
        xs = _mlp(xs, gain_mlp, mod_s, w_up, w_down, _row_tile(bs * ls))
    return (xp.reshape(bp, lp, D_MODEL), xs.reshape(bs, ls, D_MODEL),
            jnp.stack(kwp), jnp.stack(vwp), jnp.stack(kws), jnp.stack(vws), jnp.stack(sp), jnp.stack(ss))
```

```python
import functools

import numpy as np
import jax
import jax.numpy as jnp
from jax import lax
from jax.experimental import pallas as pl
from jax.experimental.pallas import tpu as pltpu

F32 = jnp.float32
BF16 = jnp.bfloat16

D_MODEL = 2048
DEPTH = 4
N_MIXERS = 2
PAST_LEN = 16384
HEAD_DIM = 64
N_Q_HEADS = D_MODEL // HEAD_DIM
N_KV_HEADS = N_Q_HEADS // 8
GQA_GROUP = N_Q_HEADS // N_KV_HEADS
KV_WIDTH = N_KV_HEADS * HEAD_DIM
WINDOW = 128
ROT_DIM = HEAD_DIM // 4
ROPE_THETA = 500000.0
REC_HEADS = 16
REC_DK = 128
REC_DV = D_MODEL // REC_HEADS
D_FF = 4 * D_MODEL
EPS = 1e-6

LANES = 128
BF16_SUBLANES = 16
MIB = 1024 * 1024
SCAN_ROWS = 128
SCAN_CHUNKS = 2
SCAN_HEADS = 8
SCAN_STATES = 64


def _compiler_params(semantics, vmem_mib):
    return pltpu.CompilerParams(dimension_semantics=semantics, vmem_limit_bytes=vmem_mib * MIB)


def _silu(x):
    half = 0.5 * x
    return half + half * jnp.tanh(half)


def _modnorm(x, gain, scale, shift):
    var = jnp.mean(x * x, axis=-1, keepdims=True)
    y = x * lax.rsqrt(var + EPS) * gain
    return y * (1.0 + scale) + shift


def _split_bf16(x, parts):
    out = []
    for _ in range(parts - 1):
        hi = x.astype(BF16)
        out.append(hi)
        x = x - hi.astype(F32)
    out.append(x.astype(BF16))
    return out


SHIFT1, SCALE1, GATE1, SHIFT2, SCALE2, GATE2 = range(6)


class _Mod:
    def __init__(self, table, layer, rows_per_group, per_row):
        self.table, self.layer, self.rows_per_group, self.per_row = table, layer, rows_per_group, per_row

    def spec(self, chunk, tm, tn=D_MODEL, col=lambda *_: 0):
        layer, first = self.layer, chunk * (D_MODEL // tn)
        if self.per_row:
            return pl.BlockSpec((None, 1, tm, tn), lambda i, *rest: (layer, 0, i, first + col(i, *rest)))
        assert self.rows_per_group % tm == 0
        per = self.rows_per_group // tm
        return pl.BlockSpec((None, 1, 1, tn), lambda i, *rest: (layer, i // per, 0, first + col(i, *rest)))


class _SideCast:
    def __init__(self, items, n_steps, step_of):
        self.stacks = [w for w, _ in items]
        self.in_specs, self.out_specs, self.out_shape = [], [], []
        for w, layer in items:
            _, rows, cols = w.shape
            assert rows % (n_steps * BF16_SUBLANES) == 0
            block = rows // n_steps
            self.in_specs.append(pl.BlockSpec((None, block, cols), lambda *g, layer=layer: (layer, step_of(*g), 0)))
            self.out_specs.append(pl.BlockSpec((block, cols), lambda *g: (step_of(*g), 0)))
            self.out_shape.append(jax.ShapeDtypeStruct((rows, cols), BF16))

    @staticmethod
    def run(src_refs, dst_refs):
        for src, dst in zip(src_refs, dst_refs):
            dst[...] = src[...].astype(BF16)


def _ada_kernel(c_ref, w_ref, b_ref, o_ref):
    s = _silu(c_ref[...]).astype(BF16)
    o_ref[0] = jnp.dot(s, w_ref[0].astype(BF16), preferred_element_type=F32) + b_ref[0]


def _ada_mods(c_all, w_ada, b_ada):
    rows = c_all.shape[0]
    tn = 1024
    n = 6 * D_MODEL
    return pl.pallas_call(
        _ada_kernel,
        grid=(DEPTH, n // tn),
        in_specs=[
            pl.BlockSpec((rows, D_MODEL), lambda l, j: (0, 0)),
            pl.BlockSpec((1, D_MODEL, tn), lambda l, j: (l, 0, j)),
            pl.BlockSpec((1, 1, tn), lambda l, j: (l, 0, j)),
        ],
        out_specs=pl.BlockSpec((1, rows, tn), lambda l, j: (l, 0, j)),
        out_shape=jax.ShapeDtypeStruct((DEPTH, rows, n), F32),
        compiler_params=_compiler_params(("parallel", "parallel"), 40),
        name="ada_mods",
    )(c_all, w_ada, b_ada.reshape(DEPTH, 1, n))


def _norm_rows_kernel(x_ref, gain_ref, sc_ref, sh_ref, o_ref):
    o_ref[...] = _modnorm(x_ref[...], gain_ref[...], sc_ref[0], sh_ref[0]).astype(o_ref.dtype)


def _norm_rows(x, gain, mod, tm):
    rows = x.shape[0]
    return pl.pallas_call(
        _norm_rows_kernel,
        grid=(rows // tm,),
        in_specs=[
            pl.BlockSpec((tm, D_MODEL), lambda i: (i, 0)),
            pl.BlockSpec((1, D_MODEL), lambda i: (0, 0)),
            mod.spec(SCALE1, tm),
            mod.spec(SHIFT1, tm),
        ],
        out_specs=pl.BlockSpec((tm, D_MODEL), lambda i: (i, 0)),
        out_shape=jax.ShapeDtypeStruct((rows, D_MODEL), BF16),
        compiler_params=_compiler_params(("parallel",), 32),
        name="norm_rows",
    )(x, gain, mod.table, mod.table)


def _qkv_kernel(x_ref, gain_ref, sc_ref, sh_ref, w_ref, qg_ref, kg_ref, cos_ref, s1_ref, s2_ref, *rest):
    n_cast = (len(rest) - 3) // 2
    q_ref, k_ref, v_ref = rest[n_cast:n_cast + 3]
    _SideCast.run(rest[:n_cast], rest[n_cast + 3:])
    h = _modnorm(x_ref[...], gain_ref[...], sc_ref[0], sh_ref[0]).astype(BF16)
    acc = jnp.dot(h, w_ref[...], preferred_element_type=F32)
    cos, s1, s2 = cos_ref[...], s1_ref[...], s2_ref[...]
    qk_width = D_MODEL + KV_WIDTH
    first_head = lax.broadcasted_iota(jnp.int32, (x_ref.shape[0], LANES), 1) < HEAD_DIM
    for c in range(qk_width // LANES):
        xc = acc[:, c * LANES:(c + 1) * LANES]
        x2 = xc * xc
        ssq = jnp.where(first_head,
                        jnp.sum(jnp.where(first_head, x2, 0.0), axis=-1, keepdims=True),
                        jnp.sum(jnp.where(first_head, 0.0, x2), axis=-1, keepdims=True))
        is_q = c * LANES < D_MODEL
        y = xc * lax.rsqrt(ssq * (1.0 / HEAD_DIM) + EPS) * (qg_ref[...] if is_q else kg_ref[...])
        r = y * cos + pltpu.roll(y, LANES - ROT_DIM // 2, 1) * s1 + pltpu.roll(y, ROT_DIM // 2, 1) * s2
        if is_q:
            q_ref[:, c * LANES:(c + 1) * LANES] = (r * (HEAD_DIM ** -0.5)).astype(q_ref.dtype)
        else:
            k_ref[:, c * LANES - D_MODEL:(c + 1) * LANES - D_MODEL] = r
    v_ref[...] = acc[:, qk_width:]


def _rope_tables(pos):
    half = ROT_DIM // 2
    inv = ROPE_THETA ** (-jnp.arange(half, dtype=F32) / half)
    ang = pos.astype(F32)[:, None] * inv[None, :]
    cos, sin = jnp.cos(ang), jnp.sin(ang)
    n = pos.shape[0]
    pad = jnp.zeros((n, HEAD_DIM - ROT_DIM), F32)
    zero = jnp.zeros((n, half), F32)
    c = jnp.concatenate([cos, cos, pad + 1.0], axis=1)
    s1 = jnp.concatenate([-sin, zero, pad], axis=1)
    s2 = jnp.concatenate([zero, sin, pad], axis=1)
    return tuple(jnp.tile(t, (1, LANES // HEAD_DIM)) for t in (c, s1, s2))


def _qkv(x, gain, mod, w, q_gain, k_gain, tables, table_rows, tm, q_dtype, cast_items=()):
    rows = x.shape[0]
    n = w.shape[-1]
    per = table_rows // tm
    tspec = pl.BlockSpec((tm, LANES), lambda i: (i % per, 0))
    vspec = pl.BlockSpec((1, LANES), lambda i: (0, 0))
    reps = LANES // HEAD_DIM
    cast = _SideCast(cast_items, rows // tm, lambda i: i)
    q, k, v, *casted = pl.pallas_call(
        _qkv_kernel,
        grid=(rows // tm,),
        in_specs=[
            pl.BlockSpec((tm, D_MODEL), lambda i: (i, 0)),
            pl.BlockSpec((1, D_MODEL), lambda i: (0, 0)),
            mod.spec(SCALE1, tm),
            mod.spec(SHIFT1, tm),
            pl.BlockSpec((D_MODEL, n), lambda i: (0, 0)),
            vspec, vspec, tspec, tspec, tspec,
        ] + cast.in_specs,
        out_specs=[
            pl.BlockSpec((tm, D_MODEL), lambda i: (i, 0)),
            pl.BlockSpec((tm, KV_WIDTH), lambda i: (i, 0)),
            pl.BlockSpec((tm, KV_WIDTH), lambda i: (i, 0)),
        ] + cast.out_specs,
        out_shape=[
            jax.ShapeDtypeStruct((rows, D_MODEL), q_dtype),
            jax.ShapeDtypeStruct((rows, KV_WIDTH), F32),
            jax.ShapeDtypeStruct((rows, KV_WIDTH), F32),
        ] + cast.out_shape,
        compiler_params=_compiler_params(("parallel",), 56),
        name="qkv",
    )(x, gain, mod.table, mod.table, w,
      jnp.tile(q_gain.reshape(1, HEAD_DIM), (1, reps)), jnp.tile(k_gain.reshape(1, HEAD_DIM), (1, reps)),
      *tables, *cast.stacks)
    return q, k, v, casted


def _softmax_with_sink(s, sink):
    m = jnp.maximum(jnp.max(s, axis=-1, keepdims=True), sink)
    p = jnp.exp(s - m)
    denom = jnp.sum(p, axis=-1, keepdims=True) + jnp.exp(sink - m)
    return p, denom


def _attn_prompt_kernel(sink_ref, q_ref, kp_ref, kc_ref, vp_ref, vc_ref, *rest):
    n_cast = len(rest) // 2
    o_ref = rest[n_cast]
    _SideCast.run(rest[:n_cast], rest[n_cast + 1:])
    j = pl.program_id(1)
    kk = jnp.concatenate([kp_ref[...], kc_ref[...]], axis=0).astype(BF16)
    vt = jnp.concatenate([vp_ref[...], vc_ref[...]], axis=0).T.astype(BF16)
    s = lax.broadcasted_iota(jnp.int32, (2 * WINDOW, WINDOW), 0)
    t = lax.broadcasted_iota(jnp.int32, (2 * WINDOW, WINDOW), 1)
    ok = (s > t) & (s <= t + WINDOW) & (s >= jnp.where(j > 0, 0, WINDOW))
    bias = jnp.where(ok, 0.0, -jnp.inf)
    zk = jnp.zeros((2 * WINDOW, HEAD_DIM), BF16)
    zv = jnp.zeros((HEAD_DIM, 2 * WINDOW), BF16)
    for g in range(N_KV_HEADS):
        kg = kk[:, g * HEAD_DIM:(g + 1) * HEAD_DIM]
        vg = vt[g * HEAD_DIM:(g + 1) * HEAD_DIM]
        k2 = jnp.concatenate([jnp.concatenate([kg, zk], axis=1), jnp.concatenate([zk, kg], axis=1)], axis=0)
        v2 = jnp.concatenate([jnp.concatenate([vg, zv], axis=1), jnp.concatenate([zv, vg], axis=1)], axis=0)
        for hp in range(GQA_GROUP // 2):
            h0 = g * GQA_GROUP + 2 * hp
            lanes = slice(h0 * HEAD_DIM, (h0 + 2) * HEAD_DIM)
            st = lax.dot_general(k2, q_ref[:, lanes], (((1,), (1,)), ((), ())), preferred_element_type=F32)
            probs = []
            for i in range(2):
                sh = st[i * 2 * WINDOW:(i + 1) * 2 * WINDOW] + bias
                sink = sink_ref[h0 + i]
                m = jnp.maximum(jnp.max(sh, axis=0, keepdims=True), sink)
                p = jnp.exp(sh - m)
                denom = jnp.sum(p, axis=0, keepdims=True) + jnp.exp(sink - m)
                probs.append((p * (1.0 / denom)).astype(BF16))
            ot = jnp.dot(v2, jnp.concatenate(probs, axis=0), preferred_element_type=F32)
            o_ref[:, lanes] = ot.T.astype(o_ref.dtype)


def _attn_prompt(q, k, v, sinks, batch, seq, cast_items=()):
    nb = seq // WINDOW
    rows = batch * seq
    cur = lambda b, j: (b * nb + j, 0)
    prev = lambda b, j: (b * nb + jnp.maximum(j - 1, 0), 0)
    kv_block = (WINDOW, KV_WIDTH)
    cast = _SideCast(cast_items, batch * nb, lambda b, j: b * nb + j)
    o, *casted = pl.pallas_call(
        _attn_prompt_kernel,
        grid=(batch, nb),
        in_specs=[
            pl.BlockSpec(memory_space=pltpu.SMEM),
            pl.BlockSpec((WINDOW, D_MODEL), cur),
            pl.BlockSpec(kv_block, prev), pl.BlockSpec(kv_block, cur),
            pl.BlockSpec(kv_block, prev), pl.BlockSpec(kv_block, cur),
        ] + cast.in_specs,
        out_specs=[pl.BlockSpec((WINDOW, D_MODEL), cur)] + cast.out_specs,
        out_shape=[jax.ShapeDtypeStruct((rows, D_MODEL), BF16)] + cast.out_shape,
        compiler_params=_compiler_params(("parallel", "parallel"), 40),
        name="attn_prompt",
    )(sinks, q, k, k, v, v, *cast.stacks)
    return o, casted


SAMPLE_SEQS = 4


def _attn_sample_kernel(sink_ref, q_ref, k_ref, v_ref, ck_ref, cv_ref, o_ref, nk_ref, nv_ref, *, steps):
    rows = GQA_GROUP * steps
    assert steps & (steps - 1) == 0
    t = lax.broadcasted_iota(jnp.int32, (rows, 2 * WINDOW), 0) & (steps - 1)
    s = lax.broadcasted_iota(jnp.int32, (rows, 2 * WINDOW), 1)
    ok = ((s < WINDOW) & (s > t)) | ((s >= WINDOW) & (s - WINDOW <= t))
    zpad = jnp.zeros((WINDOW - steps, KV_WIDTH), F32)
    for b in range(ck_ref.shape[0]):
        new = slice(b * steps, (b + 1) * steps)
        kc, vc = ck_ref[b], cv_ref[b]
        kn, vn = k_ref[new], v_ref[new]
        nk_ref[b, :WINDOW - steps] = kc[steps:]
        nk_ref[b, WINDOW - steps:] = kn
        nv_ref[b, :WINDOW - steps] = vc[steps:]
        nv_ref[b, WINDOW - steps:] = vn
        kk = jnp.concatenate([kc, kn, zpad], axis=0).astype(BF16)
        vv = jnp.concatenate([vc, vn, zpad], axis=0).astype(BF16)
        q = q_ref[new]
        for g in range(N_KV_HEADS):
            heads = range(g * GQA_GROUP, (g + 1) * GQA_GROUP)
            qg = jnp.concatenate([q[:, h * HEAD_DIM:(h + 1) * HEAD_DIM] for h in heads], axis=0).astype(BF16)
            sc = lax.dot_general(qg, kk[:, g * HEAD_DIM:(g + 1) * HEAD_DIM], (((1,), (1,)), ((), ())),
                                 preferred_element_type=F32)
            p, denom = _softmax_with_sink(jnp.where(ok, sc, -jnp.inf), sink_ref[g])
            og = jnp.dot(p.astype(BF16), vv[:, g * HEAD_DIM:(g + 1) * HEAD_DIM],
                         preferred_element_type=F32) / denom
            for i, h in enumerate(heads):
                o_ref[new, h * HEAD_DIM:(h + 1) * HEAD_DIM] = og[i * steps:(i + 1) * steps]


def _attn_sample(q, k, v, cache_k, cache_v, layer, sinks, batch, steps):
    rows = batch * steps
    sink_rows = jnp.repeat(sinks.reshape(N_KV_HEADS, GQA_GROUP), steps, axis=1)[..., None]
    seqs = min(SAMPLE_SEQS, batch)
    assert batch % seqs == 0
    steps_blk = seqs * steps
    row = lambda b: (b, 0)
    win = lambda b: (b, 0, 0)
    cache_block = (seqs, WINDOW, KV_WIDTH)
    return pl.pallas_call(
        functools.partial(_attn_sample_kernel, steps=steps),
        grid=(batch // seqs,),
        in_specs=[
            pl.BlockSpec((N_KV_HEADS, GQA_GROUP * steps, 1), lambda b: (0, 0, 0)),
            pl.BlockSpec((steps_blk, D_MODEL), row),
            pl.BlockSpec((steps_blk, KV_WIDTH), row), pl.BlockSpec((steps_blk, KV_WIDTH), row),
            pl.BlockSpec((None,) + cache_block, lambda b: (layer, b, 0, 0)),
            pl.BlockSpec((None,) + cache_block, lambda b: (layer, b, 0, 0)),
        ],
        out_specs=[
            pl.BlockSpec((steps_blk, D_MODEL), row),
            pl.BlockSpec(cache_block, win), pl.BlockSpec(cache_block, win),
        ],
        out_shape=[
            jax.ShapeDtypeStruct((rows, D_MODEL), F32),
            jax.ShapeDtypeStruct((batch, WINDOW, KV_WIDTH), F32),
            jax.ShapeDtypeStruct((batch, WINDOW, KV_WIDTH), F32),
        ],
        compiler_params=_compiler_params(("parallel",), 32),
        name="attn_sample",
    )(sink_rows, q, k, v, cache_k, cache_v)


SUBLANES = 8


def _levels_from_matrix(levels):
    return [b for b in levels if 2 < b <= SUBLANES]


def _scan_matrix(seq_rows, levels, with_suffix):
    n = SCAN_ROWS
    t = np.arange(n)[:, None]
    r = np.arange(n)[None, :]
    same = (t // seq_rows) == (r // seq_rows)
    mats = [same & (r <= t)] + ([same & (r > t)] if with_suffix else [])
    for b in _levels_from_matrix(levels):
        mid = (t // b) * b + b // 2
        upper = (t & (b // 2)) != 0
        mats.append(np.where(upper, (r >= mid) & (r <= t), (r > t) & (r < mid)))
    stacked = np.concatenate(mats, axis=0)
    return jnp.asarray(np.concatenate([stacked, stacked], axis=1), dtype=BF16)


def _scan_kernel(h_ref, wq_ref, wf_ref, wi_ref, wg_ref, lbl_ref, og_ref, mat_ref, *rest,
                 layer, n_seq, levels, has_s0, n_cast):
    rest = list(rest)
    s0_ref = rest.pop(0) if has_s0 else None
    cast_src, (a_ref, st_ref), cast_dst, g_scr = rest[:n_cast], rest[n_cast:n_cast + 2], rest[n_cast + 2:-1], rest[-1]
    _SideCast.run(cast_src, cast_dst)
    n = SCAN_ROWS
    heads = st_ref.shape[1]
    width = heads * LANES
    seq_rows = n // n_seq
    chunk = pl.program_id(2)

    @pl.when(chunk == 0)
    def _():
        for s in range(n_seq):
            for h in range(heads):
                st_ref[s, h] = jnp.zeros((REC_DV, REC_DK), F32) if s0_ref is None else s0_ref[s, h].T

    logits = lbl_ref[...]
    pexp = jnp.exp(logits - jnp.max(logits, axis=0, keepdims=True))
    psum = jnp.sum(pexp, axis=0, keepdims=True)
    lb = jnp.zeros_like(psum)
    for i in range(1, layer + 1):
        lb = lb + pexp[i:i + 1] / psum

    hb = h_ref[...]
    project = lambda w_ref: jnp.dot(hb, w_ref[...], preferred_element_type=F32)
    qs_all = _silu(project(wq_ref))
    fz = project(wf_ref)
    e = jnp.exp(-jnp.abs(fz))
    r = 1.0 / (1.0 + e)
    pos = fz >= 0
    f_all = lb + (1.0 - lb) * jnp.where(pos, r, e * r)
    logf_all = jnp.log(f_all)
    kin_all = (1.0 - lb) * jnp.where(pos, e * r, r)
    val_all = project(wi_ref)
    gate_all = _silu(project(wg_ref))
    mat = mat_ref[...]
    t_row = lax.broadcasted_iota(jnp.int32, (n, width), 0)
    t_lane = lax.broadcasted_iota(jnp.int32, (n, LANES), 0)
    tt = lax.broadcasted_iota(jnp.int32, (n, n), 0)
    ss = lax.broadcasted_iota(jnp.int32, (n, n), 1)
    same_block = {b: (tt ^ ss) < b for b in levels if b < n}
    for u in range(g_scr.shape[0]):
        rows = slice(u * n, (u + 1) * n)
        _scan_chunk(qs_all[rows], f_all[rows], logf_all[rows], kin_all[rows], val_all[rows], gate_all[rows], mat,
                    og_ref, g_scr.at[u], st_ref, a_ref.at[rows], t_row, t_lane, same_block, n_seq, levels)

    @pl.when(chunk == pl.num_programs(2) - 1)
    def _():
        for s in range(n_seq):
            for h in range(heads):
                st_ref[s, h] = st_ref[s, h].T


def _scan_chunk(qs, f, logf, kin, val, gate, mat, og_ref, g_scr, st_ref, a_ref, t_row, t_lane, same_block,
                n_seq, levels):
    n = SCAN_ROWS
    heads = st_ref.shape[1]
    width = heads * LANES
    seq_rows = n // n_seq
    sums = jnp.dot(mat, jnp.concatenate(_split_bf16(logf, 2), axis=0), preferred_element_type=F32)
    gcum = sums[:n]
    g_scr[...] = gcum
    if n_seq == 1:
        gtail = g_scr[n - 1:n, :] - gcum
        block = 1
    else:
        gtail = sums[n:2 * n]
        block = 2
    level_sums = {b: sums[(block + i) * n:(block + i + 1) * n] for i, b in enumerate(_levels_from_matrix(levels))}

    q_lv, k_lv = [], []
    for b in levels:
        half = b // 2
        if b > SUBLANES:
            zero = jnp.zeros((half, width), F32)
            qparts, kparts = [], []
            for lo in range(0, n, b):
                mid, hi = lo + half, lo + b
                ref = g_scr[mid - 1:mid, :]
                kparts += [kin[lo:mid] * jnp.exp(ref - gcum[lo:mid]), zero]
                qparts += [zero, qs[mid:hi] * jnp.exp(gcum[mid:hi] - ref)]
            qt, kt = jnp.concatenate(qparts, axis=0), jnp.concatenate(kparts, axis=0)
        else:
            upper = (t_row & half) != 0
            if b == 2:
                qt, kt = jnp.where(upper, qs * f, 0.0), jnp.where(upper, 0.0, kin)
            else:
                decay = jnp.exp(level_sums[b])
                qt, kt = jnp.where(upper, qs * decay, 0.0), jnp.where(upper, 0.0, kin * decay)
        q_lv.append(qt.astype(BF16))
        k_lv.append(kt.astype(BF16))
    qe_all = qs * jnp.exp(gcum)
    kt_all = kin * jnp.exp(gtail)
    contract_lanes = (((1,), (1,)), ((), ()))
    contract_rows = (((0,), (0,)), ((), ()))

    for h in range(heads):
        sl = slice(h * LANES, (h + 1) * LANES)
        vh = val[:, sl]
        vb = vh.astype(BF16)
        a = None
        for l, b in enumerate(levels):
            al = lax.dot_general(q_lv[l][:, sl], k_lv[l][:, sl], contract_lanes, preferred_element_type=F32)
            if b == n:
                a = al
            else:
                a = jnp.where(same_block[b], al, 0.0 if a is None else a)
        qe, kt = qe_all[:, sl], kt_all[:, sl]
        diag = jnp.sum(qs[:, sl] * kin[:, sl], axis=-1, keepdims=True) * vh
        if n_seq == 1:
            st = st_ref[0, h]
            lhs = jnp.concatenate([a.astype(BF16), qe.astype(BF16)], axis=1)
            rhs = jnp.concatenate([vb, st.T.astype(BF16)], axis=0)
            o = jnp.dot(lhs, rhs, preferred_element_type=F32) + diag
            st_ref[0, h] = st * jnp.exp(g_scr[n - 1:n, sl]) + lax.dot_general(
                vb, kt.astype(BF16), contract_rows, preferred_element_type=F32)
        else:
            o = jnp.dot(a.astype(BF16), vb, preferred_element_type=F32) + diag
        for s in range(n_seq if n_seq > 1 else 0):
            last = (s + 1) * seq_rows - 1
            if n_seq == 1:
                qe_s, kt_s = qe, kt
            else:
                inside = (t_lane >= s * seq_rows) & (t_lane <= last)
                qe_s, kt_s = jnp.where(inside, qe, 0.0), jnp.where(inside, kt, 0.0)
            st = st_ref[s, h]
            o = o + lax.dot_general(qe_s.astype(BF16), st.astype(BF16), contract_lanes,
                                    preferred_element_type=F32)
            st_ref[s, h] = st * jnp.exp(g_scr[last:last + 1, sl]) + lax.dot_general(
                vb, kt_s.astype(BF16), contract_rows, preferred_element_type=F32)
        rs = lax.rsqrt(jnp.mean(o * o, axis=-1, keepdims=True) + EPS)
        a_ref[:, sl] = (o * rs * og_ref[:, sl] * gate[:, sl]).astype(a_ref.dtype)


def _hgrn_scan(h, w_in, lb_logits, o_gain, s0, layer, n_state, seq_rows, cast_items=()):
    rows = h.shape[0]
    n = SCAN_ROWS
    n_seq = max(n // seq_rows, 1)
    sub = SCAN_CHUNKS if seq_rows % (SCAN_CHUNKS * n) == 0 else 1
    chunks = max(seq_rows // (sub * n), 1)
    levels = [min(seq_rows, n) >> l for l in range(int(np.log2(min(seq_rows, n))))]
    mat = _scan_matrix(min(seq_rows, n), levels, with_suffix=n_seq > 1)
    heads = min(SCAN_HEADS, max(1, SCAN_STATES // n_seq))
    hg = REC_HEADS // heads
    wb = heads * LANES
    sect = lambda k: pl.BlockSpec((D_MODEL, wb), lambda g, b, c: (0, k * hg + g))
    head_vec = lambda r: pl.BlockSpec((r, wb), lambda g, b, c: (0, g))
    state_block = (n_seq, heads, REC_DK, REC_DV)
    state = pl.BlockSpec(state_block, lambda g, b, c: (b, g, 0, 0))
    state_in = [] if s0 is None else [pl.BlockSpec((None,) + state_block, lambda g, b, c: (layer, b, g, 0, 0))]
    groups = n_state // n_seq
    cast = _SideCast(cast_items, hg * groups * chunks, lambda g, b, c: (g * groups + b) * chunks + c)
    a, state_out, *casted = pl.pallas_call(
        functools.partial(_scan_kernel, layer=layer, n_seq=n_seq, levels=levels, has_s0=s0 is not None,
                          n_cast=len(cast_items)),
        grid=(hg, groups, chunks),
        in_specs=[
            pl.BlockSpec((sub * n, D_MODEL), lambda g, b, c: (b * chunks + c, 0)),
            sect(0), sect(1), sect(2), sect(3),
            head_vec(lb_logits.shape[0]), head_vec(1),
            pl.BlockSpec(mat.shape, lambda g, b, c: (0, 0)),
        ] + state_in + cast.in_specs,
        out_specs=[pl.BlockSpec((sub * n, wb), lambda g, b, c: (b * chunks + c, g)), state] + cast.out_specs,
        out_shape=[
            jax.ShapeDtypeStruct((rows, D_MODEL), BF16),
            jax.ShapeDtypeStruct((n_state, REC_HEADS, REC_DK, REC_DV), F32),
        ] + cast.out_shape,
        scratch_shapes=[pltpu.VMEM((sub, n, wb), F32)],
        compiler_params=_compiler_params(("parallel", "parallel", "arbitrary"), 56),
        name="hgrn_scan",
    )(h, w_in, w_in, w_in, w_in, lb_logits, o_gain.reshape(1, D_MODEL), mat, *([] if s0 is None else [s0]),
      *cast.stacks)
    return a, state_out, casted


def _outproj_kernel(a_ref, x_ref, g_ref, w_ref, o_ref):
    y = jnp.dot(a_ref[...].astype(BF16), w_ref[...], preferred_element_type=F32)
    o_ref[...] = x_ref[...] + g_ref[0] * y


def _outproj(a, x, mod, w, tm):
    rows = x.shape[0]
    row = lambda i: (i, 0)
    return pl.pallas_call(
        _outproj_kernel,
        grid=(rows // tm,),
        in_specs=[
            pl.BlockSpec((tm, D_MODEL), row),
            pl.BlockSpec((tm, D_MODEL), row),
            mod.spec(GATE1, tm),
            pl.BlockSpec((D_MODEL, D_MODEL), lambda i: (0, 0)),
        ],
        out_specs=pl.BlockSpec((tm, D_MODEL), row),
        out_shape=jax.ShapeDtypeStruct((rows, D_MODEL), F32),
        compiler_params=_compiler_params(("parallel",), 48),
        name="outproj",
    )(a, x, mod.table, w)


def _mlp_kernel(x_ref, gain_ref, sc_ref, sh_ref, xo_ref, g_ref, wu_ref, wd_ref, o_ref, h_scr, u_scr, *, n_up, tf):
    j = pl.program_id(1)

    @pl.when(j == 0)
    def _():
        h_scr[...] = _modnorm(x_ref[...], gain_ref[...], sc_ref[0], sh_ref[0]).astype(BF16)

    @pl.when(j < n_up)
    def _():
        u = jnp.maximum(jnp.dot(h_scr[...], wu_ref[...], preferred_element_type=F32), 0.0)
        u2 = (u * u).astype(BF16)
        for c in range(n_up):
            @pl.when(j == c)
            def _():
                u_scr[:, c * tf:(c + 1) * tf] = u2

    @pl.when(j >= n_up)
    def _():
        d = jnp.dot(u_scr[...], wd_ref[...], preferred_element_type=F32)
        o_ref[...] = xo_ref[...] + g_ref[0] * d


MLP_UP_BLOCK = 2048
MLP_DOWN_BLOCK = 512


def _mlp(x, gain, mod, w_up, w_down, tm, tf=MLP_UP_BLOCK, tn=MLP_DOWN_BLOCK):
    rows = x.shape[0]
    n_up, n_down = D_FF // tf, D_MODEL // tn
    row = lambda i, j: (i, 0)
    out_col = lambda i, j: jnp.maximum(j - n_up, 0)
    return pl.pallas_call(
        functools.partial(_mlp_kernel, n_up=n_up, tf=tf),
        grid=(rows // tm, n_up + n_down),
        in_specs=[
            pl.BlockSpec((tm, D_MODEL), row),
            pl.BlockSpec((1, D_MODEL), lambda i, j: (0, 0)),
            mod.spec(SCALE2, tm), mod.spec(SHIFT2, tm),
            pl.BlockSpec((tm, tn), lambda i, j: (i, out_col(i, j))),
            mod.spec(GATE2, tm, tn, out_col),
            pl.BlockSpec((D_MODEL, tf), lambda i, j: (0, jnp.minimum(j, n_up - 1))),
            pl.BlockSpec((D_FF, tn), lambda i, j: (0, out_col(i, j))),
        ],
        out_specs=pl.BlockSpec((tm, tn), lambda i, j: (i, out_col(i, j))),
        out_shape=jax.ShapeDtypeStruct((rows, D_MODEL), F32),
        scratch_shapes=[pltpu.VMEM((tm, D_MODEL), BF16), pltpu.VMEM((tm, D_FF), BF16)],
        compiler_params=_compiler_params(("parallel", "arbitrary"), 60),
        name="mlp",
    )(x, gain, mod.table, mod.table, x, mod.table, w_up, w_down)


def _row_tile(rows):
    return min(rows, 512)


def _attn_prompt_layer(x, mod, gain_mix, w_qkv, q_gain, k_gain, sinks, w_o_f32, layer, batch, seq,
                       qkv_cast, attn_cast):
    tm = _row_tile(batch * seq)
    tables = _rope_tables(jnp.arange(seq, dtype=jnp.int32))
    q, k, v, qkv_casted = _qkv(x, gain_mix, mod, w_qkv, q_gain, k_gain, tables, seq, tm, BF16, qkv_cast)
    o, (w_o, *attn_casted) = _attn_prompt(q, k, v, sinks, batch, seq, [(w_o_f32, layer)] + list(attn_cast))
    window = lambda t: t.reshape(batch, seq, KV_WIDTH)[:, seq - WINDOW:].reshape(batch, WINDOW, N_KV_HEADS, HEAD_DIM)
    return _outproj(o, x, mod, w_o, tm), window(k), window(v), w_o, qkv_casted, attn_casted


def _attn_sample_layer(x, mod, gain_mix, w_qkv, q_gain, k_gain, sinks, w_o, layer, batch, seq, cache_k, cache_v):
    rows = batch * seq
    tm = _row_tile(rows)
    pos = PAST_LEN + jnp.arange(seq, dtype=jnp.int32)
    tables = tuple(jnp.tile(t, (batch, 1)) for t in _rope_tables(pos))
    q, k, v, _ = _qkv(x, gain_mix, mod, w_qkv, q_gain, k_gain, tables, rows, tm, F32)
    ck = cache_k.reshape(-1, batch, WINDOW, KV_WIDTH)
    cv = cache_v.reshape(-1, batch, WINDOW, KV_WIDTH)
    o, k_win, v_win = _attn_sample(q, k, v, ck, cv, layer, sinks, batch, seq)
    k_win = k_win.reshape(batch, WINDOW, N_KV_HEADS, HEAD_DIM)
    v_win = v_win.reshape(batch, WINDOW, N_KV_HEADS, HEAD_DIM)
    return _outproj(o, x, mod, w_o, tm), k_win, v_win


def _hgrn_layer(x, mod, gain_mix, w_in, lb_logits, o_gain, w_o, s0, layer, batch, seq, cast_items=()):
    tm = _row_tile(batch * seq)
    h = _norm_rows(x, gain_mix, mod, tm)
    a, state, casted = _hgrn_scan(h, w_in, lb_logits, o_gain, s0, layer, batch, seq, cast_items)
    if w_o is None:
        w_o, *casted = casted
    return _outproj(a, x, mod, w_o, tm), state, w_o, casted


def kernel(x_prompt, x_sample, cache_k_win, cache_v_win, state_hgrn, c_prompt, c_sample, norm_gain, w_ada, b_ada,
           attn_w_qkv, attn_q_gain, attn_k_gain, attn_sinks, attn_w_o, rec_w_in, rec_lb_logits, rec_o_gain,
           rec_w_o, mlp_w_up, mlp_w_down):
    bp, lp, _ = x_prompt.shape
    bs, ls, _ = x_sample.shape
    n_c = bp + bs
    c_all = jnp.concatenate([c_prompt, c_sample, jnp.zeros((-n_c % 16, D_MODEL), F32)], axis=0)
    mods = _ada_mods(c_all, w_ada, b_ada)

    xp = x_prompt.reshape(bp * lp, D_MODEL)
    xs = x_sample.reshape(bs * ls, D_MODEL)
    lb_logits = rec_lb_logits.astype(F32)
    table_p = mods[:, :bp].reshape(DEPTH, bp, 1, 6 * D_MODEL)
    table_s = jnp.repeat(mods[:, bp:n_c], ls, axis=1).reshape(DEPTH, 1, bs * ls, 6 * D_MODEL)
    w_qkv = attn_w_qkv[0].astype(BF16)
    kwp, vwp, kws, vws, sp, ss = [], [], [], [], [], []
    for i in range(DEPTH):
        j = i // N_MIXERS
        mod_p = _Mod(table_p, i, lp, per_row=False)
        mod_s = _Mod(table_s, i, ls, per_row=True)
        gain_mix = norm_gain[i, 0].reshape(1, D_MODEL)
        gain_mlp = norm_gain[i, 1].reshape(1, D_MODEL)
        mlp_cast = [(mlp_w_up, i), (mlp_w_down, i)]
        if i % N_MIXERS == 0:
            aw = (w_qkv, attn_q_gain[j], attn_k_gain[j], attn_sinks[j])
            xp, kp, vp, w_o, (w_in,), (w_up, w_down) = _attn_prompt_layer(
                xp, mod_p, gain_mix, *aw, attn_w_o, j, bp, lp, [(rec_w_in, j)], mlp_cast)
            xs, kn, vn = _attn_sample_layer(xs, mod_s, gain_mix, *aw, w_o, j, bs, ls, cache_k_win, cache_v_win)
            kwp.append(kp); vwp.append(vp); kws.append(kn); vws.append(vn)
        else:
            next_qkv = [(attn_w_qkv, j + 1)] if j + 1 < attn_w_qkv.shape[0] else []
            rw = (w_in, lb_logits, rec_o_gain[j])
            xp, s_p, w_o, (w_up, w_down, *w_next) = _hgrn_layer(
                xp, mod_p, gain_mix, *rw, None, None, j, bp, lp, [(rec_w_o, j)] + mlp_cast + next_qkv)
            xs, s_s, _, _ = _hgrn_layer(xs, mod_s, gain_mix, *rw, w_o, state_hgrn, j, bs, ls)
            sp.append(s_p); ss.append(s_s)
            if w_next:
                w_qkv = w_next[0]
        xp = _mlp(xp, gain_mlp, mod_p, w_up, w_down, _row_tile(bp * lp))
        xs = _mlp(xs, gain_mlp, mod_s, w_up, w_down, _row_tile(bs * ls))
    return (xp.reshape(bp, lp, D_MODEL), xs.reshape(bs, ls, D_MODEL),
            jnp.stack(kwp), jnp.stack(vwp), jnp.stack(kws), jnp.stack(vws), jnp.stack(sp), jnp.stack(ss))
```

```python
import functools

import numpy as np
import jax
import jax.numpy as jnp
from jax import lax
from jax.experimental import pallas as pl
from jax.experimental.pallas import tpu as pltpu

F32 = jnp.float32
BF16 = jnp.bfloat16

D_MODEL = 2048
DEPTH = 4
N_MIXERS = 2
PAST_LEN = 16384
HEAD_DIM = 64
N_Q_HEADS = D_MODEL // HEAD_DIM
N_KV_HEADS = N_Q_HEADS // 8
GQA_GROUP = N_Q_HEADS // N_KV_HEADS
KV_WIDTH = N_KV_HEADS * HEAD_DIM
WINDOW = 128
ROT_DIM = HEAD_DIM // 4
ROPE_THETA = 500000.0
REC_HEADS = 16
REC_DK = 128
REC_DV = D_MODEL // REC_HEADS
D_FF = 4 * D_MODEL
EPS = 1e-6

LANES = 128
BF16_SUBLANES = 16
MIB = 1024 * 1024
SCAN_ROWS = 128
SCAN_CHUNKS = 2
SCAN_HEADS = 8
SCAN_STATES = 64


def _compiler_params(semantics, vmem_mib):
    return pltpu.CompilerParams(dimension_semantics=semantics, vmem_limit_bytes=vmem_mib * MIB)


def _silu(x):
    half = 0.5 * x
    return half + half * jnp.tanh(half)


def _modnorm(x, gain, scale, shift):
    var = jnp.mean(x * x, axis=-1, keepdims=True)
    y = x * lax.rsqrt(var + EPS) * gain
    return y * (1.0 + scale) + shift


def _split_bf16(x, parts):
    out = []
    for _ in range(parts - 1):
        hi = x.astype(BF16)
        out.append(hi)
        x = x - hi.astype(F32)
    out.append(x.astype(BF16))
    return out


SHIFT1, SCALE1, GATE1, SHIFT2, SCALE2, GATE2 = range(6)


class _Mod:
    def __init__(self, table, layer, rows_per_group, per_row):
        self.table, self.layer, self.rows_per_group, self.per_row = table, layer, rows_per_group, per_row

    def spec(self, chunk, tm, tn=D_MODEL, col=lambda *_: 0):
        layer, first = self.layer, chunk * (D_MODEL // tn)
        if self.per_row:
            return pl.BlockSpec((None, 1, tm, tn), lambda i, *rest: (layer, 0, i, first + col(i, *rest)))
        assert self.rows_per_group % tm == 0
        per = self.rows_per_group // tm
        return pl.BlockSpec((None, 1, 1, tn), lambda i, *rest: (layer, i // per, 0, first + col(i, *rest)))


class _SideCast:
    def __init__(self, items, n_steps, step_of):
        self.stacks = [w for w, _ in items]
        self.in_specs, self.out_specs, self.out_shape = [], [], []
        for w, layer in items:
            _, rows, cols = w.shape
            assert rows % (n_steps * BF16_SUBLANES) == 0
            block = rows // n_steps
            self.in_specs.append(pl.BlockSpec((None, block, cols), lambda *g, layer=layer: (layer, step_of(*g), 0)))
            self.out_specs.append(pl.BlockSpec((block, cols), lambda *g: (step_of(*g), 0)))
            self.out_shape.append(jax.ShapeDtypeStruct((rows, cols), BF16))

    @staticmethod
    def run(src_refs, dst_refs):
        for src, dst in zip(src_refs, dst_refs):
            dst[...] = src[...].astype(BF16)


def _ada_kernel(c_ref, w_ref, b_ref, o_ref):
    s = _silu(c_ref[...]).astype(BF16)
    o_ref[0] = jnp.dot(s, w_ref[0].astype(BF16), preferred_element_type=F32) + b_ref[0]


def _ada_mods(c_all, w_ada, b_ada):
    rows = c_all.shape[0]
    tn = 1024
    n = 6 * D_MODEL
    return pl.pallas_call(
        _ada_kernel,
        grid=(DEPTH, n // tn),
        in_specs=[
            pl.BlockSpec((rows, D_MODEL), lambda l, j: (0, 0)),
            pl.BlockSpec((1, D_MODEL, tn), lambda l, j: (l, 0, j)),
            pl.BlockSpec((1, 1, tn), lambda l, j: (l, 0, j)),
        ],
        out_specs=pl.BlockSpec((1, rows, tn), lambda l, j: (l, 0, j)),
        out_shape=jax.ShapeDtypeStruct((DEPTH, rows, n), F32),
        compiler_params=_compiler_params(("parallel", "parallel"), 40),
        name="ada_mods",
    )(c_all, w_ada, b_ada.reshape(DEPTH, 1, n))


def _norm_rows_kernel(x_ref, gain_ref, sc_ref, sh_ref, o_ref):
    o_ref[...] = _modnorm(x_ref[...], gain_ref[...], sc_ref[0], sh_ref[0]).astype(o_ref.dtype)


def _norm_rows(x, gain, mod, tm):
    rows = x.shape[0]
    return pl.pallas_call(
        _norm_rows_kernel,
        grid=(rows // tm,),
        in_specs=[
            pl.BlockSpec((tm, D_MODEL), lambda i: (i, 0)),
            pl.BlockSpec((1, D_MODEL), lambda i: (0, 0)),
            mod.spec(SCALE1, tm),
            mod.spec(SHIFT1, tm),
        ],
        out_specs=pl.BlockSpec((tm, D_MODEL), lambda i: (i, 0)),
        out_shape=jax.ShapeDtypeStruct((rows, D_MODEL), BF16),
        compiler_params=_compiler_params(("parallel",), 32),
        name="norm_rows",
    )(x, gain, mod.table, mod.table)


def _qkv_kernel(x_ref, gain_ref, sc_ref, sh_ref, w_ref, qg_ref, kg_ref, cos_ref, s1_ref, s2_ref, *rest):
    n_cast = (len(rest) - 3) // 2
    q_ref, k_ref, v_ref = rest[n_cast:n_cast + 3]
    _SideCast.run(rest[:n_cast], rest[n_cast + 3:])
    h = _modnorm(x_ref[...], gain_ref[...], sc_ref[0], sh_ref[0]).astype(BF16)
    acc = jnp.dot(h, w_ref[...], preferred_element_type=F32)
    cos, s1, s2 = cos_ref[...], s1_ref[...], s2_ref[...]
    qk_width = D_MODEL + KV_WIDTH
    first_head = lax.broadcasted_iota(jnp.int32, (x_ref.shape[0], LANES), 1) < HEAD_DIM
    for c in range(qk_width // LANES):
        xc = acc[:, c * LANES:(c + 1) * LANES]
        x2 = xc * xc
        ssq = jnp.where(first_head,
                        jnp.sum(jnp.where(first_head, x2, 0.0), axis=-1, keepdims=True),
                        jnp.sum(jnp.where(first_head, 0.0, x2), axis=-1, keepdims=True))
        is_q = c * LANES < D_MODEL
        y = xc * lax.rsqrt(ssq * (1.0 / HEAD_DIM) + EPS) * (qg_ref[...] if is_q else kg_ref[...])
        r = y * cos + pltpu.roll(y, LANES - ROT_DIM // 2, 1) * s1 + pltpu.roll(y, ROT_DIM // 2, 1) * s2
        if is_q:
            q_ref[:, c * LANES:(c + 1) * LANES] = (r * (HEAD_DIM ** -0.5)).astype(q_ref.dtype)
        else:
            k_ref[:, c * LANES - D_MODEL:(c + 1) * LANES - D_MODEL] = r
    v_ref[...] = acc[:, qk_width:]


def _rope_tables(pos):
    half = ROT_DIM // 2
    inv = ROPE_THETA ** (-jnp.arange(half, dtype=F32) / half)
    ang = pos.astype(F32)[:, None] * inv[None, :]
    cos, sin = jnp.cos(ang), jnp.sin(ang)
    n = pos.shape[0]
    pad = jnp.zeros((n, HEAD_DIM - ROT_DIM), F32)
    zero = jnp.zeros((n, half), F32)
    c = jnp.concatenate([cos, cos, pad + 1.0], axis=1)
    s1 = jnp.concatenate([-sin, zero, pad], axis=1)
    s2 = jnp.concatenate([zero, sin, pad], axis=1)
    return tuple(jnp.tile(t, (1, LANES // HEAD_DIM)) for t in (c, s1, s2))


def _qkv(x, gain, mod, w, q_gain, k_gain, tables, table_rows, tm, q_dtype, cast_items=()):
    rows = x.shape[0]
    n = w.shape[-1]
    per = table_rows // tm
    tspec = pl.BlockSpec((tm, LANES), lambda i: (i % per, 0))
    vspec = pl.BlockSpec((1, LANES), lambda i: (0, 0))
    reps = LANES // HEAD_DIM
    cast = _SideCast(cast_items, rows // tm, lambda i: i)
    q, k, v, *casted = pl.pallas_call(
        _qkv_kernel,
        grid=(rows // tm,),
        in_specs=[
            pl.BlockSpec((tm, D_MODEL), lambda i: (i, 0)),
            pl.BlockSpec((1, D_MODEL), lambda i: (0, 0)),
            mod.spec(SCALE1, tm),
            mod.spec(SHIFT1, tm),
            pl.BlockSpec((D_MODEL, n), lambda i: (0, 0)),
            vspec, vspec, tspec, tspec, tspec,
        ] + cast.in_specs,
        out_specs=[
            pl.BlockSpec((tm, D_MODEL), lambda i: (i, 0)),
            pl.BlockSpec((tm, KV_WIDTH), lambda i: (i, 0)),
            pl.BlockSpec((tm, KV_WIDTH), lambda i: (i, 0)),
        ] + cast.out_specs,
        out_shape=[
            jax.ShapeDtypeStruct((rows, D_MODEL), q_dtype),
            jax.ShapeDtypeStruct((rows, KV_WIDTH), F32),
            jax.ShapeDtypeStruct((rows, KV_WIDTH), F32),
        ] + cast.out_shape,
        compiler_params=_compiler_params(("parallel",), 56),
        name="qkv",
    )(x, gain, mod.table, mod.table, w,
      jnp.tile(q_gain.reshape(1, HEAD_DIM), (1, reps)), jnp.tile(k_gain.reshape(1, HEAD_DIM), (1, reps)),
      *tables, *cast.stacks)
    return q, k, v, casted


def _softmax_with_sink(s, sink):
    m = jnp.maximum(jnp.max(s, axis=-1, keepdims=True), sink)
    p = jnp.exp(s - m)
    denom = jnp.sum(p, axis=-1, keepdims=True) + jnp.exp(sink - m)
    return p, denom


def _attn_prompt_kernel(sink_ref, q_ref, kp_ref, kc_ref, vp_ref, vc_ref, *rest):
    n_cast = len(rest) // 2
    o_ref = rest[n_cast]
    _SideCast.run(rest[:n_cast], rest[n_cast + 1:])
    j = pl.program_id(1)
    kk = jnp.concatenate([kp_ref[...], kc_ref[...]], axis=0).astype(BF16)
    vt = jnp.concatenate([vp_ref[...], vc_ref[...]], axis=0).T.astype(BF16)
    s = lax.broadcasted_iota(jnp.int32, (2 * WINDOW, WINDOW), 0)
    t = lax.broadcasted_iota(jnp.int32, (2 * WINDOW, WINDOW), 1)
    ok = (s > t) & (s <= t + WINDOW) & (s >= jnp.where(j > 0, 0, WINDOW))
    bias = jnp.where(ok, 0.0, -jnp.inf)
    zk = jnp.zeros((2 * WINDOW, HEAD_DIM), BF16)
    zv = jnp.zeros((HEAD_DIM, 2 * WINDOW), BF16)
    for g in range(N_KV_HEADS):
        kg = kk[:, g * HEAD_DIM:(g + 1) * HEAD_DIM]
        vg = vt[g * HEAD_DIM:(g + 1) * HEAD_DIM]
        k2 = jnp.concatenate([jnp.concatenate([kg, zk], axis=1), jnp.concatenate([zk, kg], axis=1)], axis=0)
        v2 = jnp.concatenate([jnp.concatenate([vg, zv], axis=1), jnp.concatenate([zv, vg], axis=1)], axis=0)
        for hp in range(GQA_GROUP // 2):
            h0 = g * GQA_GROUP + 2 * hp
            lanes = slice(h0 * HEAD_DIM, (h0 + 2) * HEAD_DIM)
            st = lax.dot_general(k2, q_ref[:, lanes], (((1,), (1,)), ((), ())), preferred_element_type=F32)
            probs = []
            for i in range(2):
                sh = st[i * 2 * WINDOW:(i + 1) * 2 * WINDOW] + bias
                sink = sink_ref[h0 + i]
                m = jnp.maximum(jnp.max(sh, axis=0, keepdims=True), sink)
                p = jnp.exp(sh - m)
                denom = jnp.sum(p, axis=0, keepdims=True) + jnp.exp(sink - m)
                probs.append((p * (1.0 / denom)).astype(BF16))
            ot = jnp.dot(v2, jnp.concatenate(probs, axis=0), preferred_element_type=F32)
            o_ref[:, lanes] = ot.T.astype(o_ref.dtype)


def _attn_prompt(q, k, v, sinks, batch, seq, cast_items=()):
    nb = seq // WINDOW
    rows = batch * seq
    cur = lambda b, j: (b * nb + j, 0)
    prev = lambda b, j: (b * nb + jnp.maximum(j - 1, 0), 0)
    kv_block = (WINDOW, KV_WIDTH)
    cast = _SideCast(cast_items, batch * nb, lambda b, j: b * nb + j)
    o, *casted = pl.pallas_call(
        _attn_prompt_kernel,
        grid=(batch, nb),
        in_specs=[
            pl.BlockSpec(memory_space=pltpu.SMEM),
            pl.BlockSpec((WINDOW, D_MODEL), cur),
            pl.BlockSpec(kv_block, prev), pl.BlockSpec(kv_block, cur),
            pl.BlockSpec(kv_block, prev), pl.BlockSpec(kv_block, cur),
        ] + cast.in_specs,
        out_specs=[pl.BlockSpec((WINDOW, D_MODEL), cur)] + cast.out_specs,
        out_shape=[jax.ShapeDtypeStruct((rows, D_MODEL), BF16)] + cast.out_shape,
        compiler_params=_compiler_params(("parallel", "parallel"), 40),
        name="attn_prompt",
    )(sinks, q, k, k, v, v, *cast.stacks)
    return o, casted


SAMPLE_SEQS = 8


def _attn_sample_kernel(sink_ref, q_ref, k_ref, v_ref, ck_ref, cv_ref, o_ref, nk_ref, nv_ref, *, steps):
    rows = GQA_GROUP * steps
    assert steps & (steps - 1) == 0
    t = lax.broadcasted_iota(jnp.int32, (rows, 2 * WINDOW), 0) & (steps - 1)
    s = lax.broadcasted_iota(jnp.int32, (rows, 2 * WINDOW), 1)
    ok = ((s < WINDOW) & (s > t)) | ((s >= WINDOW) & (s - WINDOW <= t))
    zpad = jnp.zeros((WINDOW - steps, KV_WIDTH), F32)
    for b in range(ck_ref.shape[0]):
        new = slice(b * steps, (b + 1) * steps)
        kc, vc = ck_ref[b], cv_ref[b]
        kn, vn = k_ref[new], v_ref[new]
        nk_ref[b, :WINDOW - steps] = kc[steps:]
        nk_ref[b, WINDOW - steps:] = kn
        nv_ref[b, :WINDOW - steps] = vc[steps:]
        nv_ref[b, WINDOW - steps:] = vn
        kk = jnp.concatenate([kc, kn, zpad], axis=0).astype(BF16)
        vv = jnp.concatenate([vc, vn, zpad], axis=0).astype(BF16)
        q = q_ref[new]
        for g in range(N_KV_HEADS):
            heads = range(g * GQA_GROUP, (g + 1) * GQA_GROUP)
            qg = jnp.concatenate([q[:, h * HEAD_DIM:(h + 1) * HEAD_DIM] for h in heads], axis=0).astype(BF16)
            sc = lax.dot_general(qg, kk[:, g * HEAD_DIM:(g + 1) * HEAD_DIM], (((1,), (1,)), ((), ())),
                                 preferred_element_type=F32)
            p, denom = _softmax_with_sink(jnp.where(ok, sc, -jnp.inf), sink_ref[g])
            og = jnp.dot(p.astype(BF16), vv[:, g * HEAD_DIM:(g + 1) * HEAD_DIM],
                         preferred_element_type=F32) / denom
            for i, h in enumerate(heads):
                o_ref[new, h * HEAD_DIM:(h + 1) * HEAD_DIM] = og[i * steps:(i + 1) * steps]


def _attn_sample(q, k, v, cache_k, cache_v, layer, sinks, batch, steps):
    rows = batch * steps
    sink_rows = jnp.repeat(sinks.reshape(N_KV_HEADS, GQA_GROUP), steps, axis=1)[..., None]
    seqs = min(SAMPLE_SEQS, batch)
    assert batch % seqs == 0
    steps_blk = seqs * steps
    row = lambda b: (b, 0)
    win = lambda b: (b, 0, 0)
    cache_block = (seqs, WINDOW, KV_WIDTH)
    return pl.pallas_call(
        functools.partial(_attn_sample_kernel, steps=steps),
        grid=(batch // seqs,),
        in_specs=[
            pl.BlockSpec((N_KV_HEADS, GQA_GROUP * steps, 1), lambda b: (0, 0, 0)),
            pl.BlockSpec((steps_blk, D_MODEL), row),
            pl.BlockSpec((steps_blk, KV_WIDTH), row), pl.BlockSpec((steps_blk, KV_WIDTH), row),
            pl.BlockSpec((None,) + cache_block, lambda b: (layer, b, 0, 0)),
            pl.BlockSpec((None,) + cache_block, lambda b: (layer, b, 0, 0)),
        ],
        out_specs=[
            pl.BlockSpec((steps_blk, D_MODEL), row),
            pl.BlockSpec(cache_block, win), pl.BlockSpec(cache_block, win),
        ],
        out_shape=[
            jax.ShapeDtypeStruct((rows, D_MODEL), F32),
            jax.ShapeDtypeStruct((batch, WINDOW, KV_WIDTH), F32),
            jax.ShapeDtypeStruct((batch, WINDOW, KV_WIDTH), F32),
        ],
        compiler_params=_compiler_params(("parallel",), 32),
        name="attn_sample",
    )(sink_rows, q, k, v, cache_k, cache_v)


SUBLANES = 8


def _levels_from_matrix(levels):
    return [b for b in levels if 2 < b <= SUBLANES]


def _scan_matrix(seq_rows, levels, with_suffix):
    n = SCAN_ROWS
    t = np.arange(n)[:, None]
    r = np.arange(n)[None, :]
    same = (t // seq_rows) == (r // seq_rows)
    mats = [same & (r <= t)] + ([same & (r > t)] if with_suffix else [])
    for b in _levels_from_matrix(levels):
        mid = (t // b) * b + b // 2
        upper = (t & (b // 2)) != 0
        mats.append(np.where(upper, (r >= mid) & (r <= t), (r > t) & (r < mid)))
    stacked = np.concatenate(mats, axis=0)
    return jnp.asarray(np.concatenate([stacked, stacked], axis=1), dtype=BF16)


def _scan_kernel(h_ref, wq_ref, wf_ref, wi_ref, wg_ref, lbl_ref, og_ref, mat_ref, *rest,
                 layer, n_seq, levels, has_s0, n_cast):
    rest = list(rest)
    s0_ref = rest.pop(0) if has_s0 else None
    cast_src, (a_ref, st_ref), cast_dst, g_scr = rest[:n_cast], rest[n_cast:n_cast + 2], rest[n_cast + 2:-1], rest[-1]
    _SideCast.run(cast_src, cast_dst)
    n = SCAN_ROWS
    heads = st_ref.shape[1]
    width = heads * LANES
    seq_rows = n // n_seq
    chunk = pl.program_id(2)

    @pl.when(chunk == 0)
    def _():
        for s in range(n_seq):
            for h in range(heads):
                st_ref[s, h] = jnp.zeros((REC_DV, REC_DK), F32) if s0_ref is None else s0_ref[s, h].T

    logits = lbl_ref[...]
    pexp = jnp.exp(logits - jnp.max(logits, axis=0, keepdims=True))
    psum = jnp.sum(pexp, axis=0, keepdims=True)
    lb = jnp.zeros_like(psum)
    for i in range(1, layer + 1):
        lb = lb + pexp[i:i + 1] / psum

    hb = h_ref[...]
    project = lambda w_ref: jnp.dot(hb, w_ref[...], preferred_element_type=F32)
    qs_all = _silu(project(wq_ref))
    fz = project(wf_ref)
    e = jnp.exp(-jnp.abs(fz))
    r = 1.0 / (1.0 + e)
    pos = fz >= 0
    f_all = lb + (1.0 - lb) * jnp.where(pos, r, e * r)
    logf_all = jnp.log(f_all)
    kin_all = (1.0 - lb) * jnp.where(pos, e * r, r)
    val_all = project(wi_ref)
    gate_all = _silu(project(wg_ref))
    mat = mat_ref[...]
    t_row = lax.broadcasted_iota(jnp.int32, (n, width), 0)
    t_lane = lax.broadcasted_iota(jnp.int32, (n, LANES), 0)
    tt = lax.broadcasted_iota(jnp.int32, (n, n), 0)
    ss = lax.broadcasted_iota(jnp.int32, (n, n), 1)
    same_block = {b: (tt ^ ss) < b for b in levels if b < n}
    for u in range(g_scr.shape[0]):
        rows = slice(u * n, (u + 1) * n)
        _scan_chunk(qs_all[rows], f_all[rows], logf_all[rows], kin_all[rows], val_all[rows], gate_all[rows], mat,
                    og_ref, g_scr.at[u], st_ref, a_ref.at[rows], t_row, t_lane, same_block, n_seq, levels)

    @pl.when(chunk == pl.num_programs(2) - 1)
    def _():
        for s in range(n_seq):
            for h in range(heads):
                st_ref[s, h] = st_ref[s, h].T


def _scan_chunk(qs, f, logf, kin, val, gate, mat, og_ref, g_scr, st_ref, a_ref, t_row, t_lane, same_block,
                n_seq, levels):
    n = SCAN_ROWS
    heads = st_ref.shape[1]
    width = heads * LANES
    seq_rows = n // n_seq
    sums = jnp.dot(mat, jnp.concatenate(_split_bf16(logf, 2), axis=0), preferred_element_type=F32)
    gcum = sums[:n]
    g_scr[...] = gcum
    if n_seq == 1:
        gtail = g_scr[n - 1:n, :] - gcum
        block = 1
    else:
        gtail = sums[n:2 * n]
        block = 2
    level_sums = {b: sums[(block + i) * n:(block + i + 1) * n] for i, b in enumerate(_levels_from_matrix(levels))}

    q_lv, k_lv = [], []
    for b in levels:
        half = b // 2
        if b > SUBLANES:
            zero = jnp.zeros((half, width), F32)
            qparts, kparts = [], []
            for lo in range(0, n, b):
                mid, hi = lo + half, lo + b
                ref = g_scr[mid - 1:mid, :]
                kparts += [kin[lo:mid] * jnp.exp(ref - gcum[lo:mid]), zero]
                qparts += [zero, qs[mid:hi] * jnp.exp(gcum[mid:hi] - ref)]
            qt, kt = jnp.concatenate(qparts, axis=0), jnp.concatenate(kparts, axis=0)
        else:
            upper = (t_row & half) != 0
            if b == 2:
                qt, kt = jnp.where(upper, qs * f, 0.0), jnp.where(upper, 0.0, kin)
            else:
                decay = jnp.exp(level_sums[b])
                qt, kt = jnp.where(upper, qs * decay, 0.0), jnp.where(upper, 0.0, kin * decay)
        q_lv.append(qt.astype(BF16))
        k_lv.append(kt.astype(BF16))
    qe_all = qs * jnp.exp(gcum)
    kt_all = kin * jnp.exp(gtail)
    contract_lanes = (((1,), (1,)), ((), ()))
    contract_rows = (((0,), (0,)), ((), ()))

    for h in range(heads):
        sl = slice(h * LANES, (h + 1) * LANES)
        vh = val[:, sl]
        vb = vh.astype(BF16)
        a = None
        for l, b in enumerate(levels):
            al = lax.dot_general(q_lv[l][:, sl], k_lv[l][:, sl], contract_lanes, preferred_element_type=F32)
            if b == n:
                a = al
            else:
                a = jnp.where(same_block[b], al, 0.0 if a is None else a)
        qe, kt = qe_all[:, sl], kt_all[:, sl]
        diag = jnp.sum(qs[:, sl] * kin[:, sl], axis=-1, keepdims=True) * vh
        if n_seq == 1:
            st = st_ref[0, h]
            lhs = jnp.concatenate([a.astype(BF16), qe.astype(BF16)], axis=1)
            rhs = jnp.concatenate([vb, st.T.astype(BF16)], axis=0)
            o = jnp.dot(lhs, rhs, preferred_element_type=F32) + diag
            st_ref[0, h] = st * jnp.exp(g_scr[n - 1:n, sl]) + lax.dot_general(
                vb, kt.astype(BF16), contract_rows, preferred_element_type=F32)
        else:
            o = jnp.dot(a.astype(BF16), vb, preferred_element_type=F32) + diag
        for s in range(n_seq if n_seq > 1 else 0):
            last = (s + 1) * seq_rows - 1
            if n_seq == 1:
                qe_s, kt_s = qe, kt
            else:
                inside = (t_lane >= s * seq_rows) & (t_lane <= last)
                qe_s, kt_s = jnp.where(inside, qe, 0.0), jnp.where(inside, kt, 0.0)
            st = st_ref[s, h]
            o = o + lax.dot_general(qe_s.astype(BF16), st.astype(BF16), contract_lanes,
                                    preferred_element_type=F32)
            st_ref[s, h] = st * jnp.exp(g_scr[last:last + 1, sl]) + lax.dot_general(
                vb, kt_s.astype(BF16), contract_rows, preferred_element_type=F32)
        rs = lax.rsqrt(jnp.mean(o * o, axis=-1, keepdims=True) + EPS)
        a_ref[:, sl] = (o * rs * og_ref[:, sl] * gate[:, sl]).astype(a_ref.dtype)


def _hgrn_scan(h, w_in, lb_logits, o_gain, s0, layer, n_state, seq_rows, cast_items=()):
    rows = h.shape[0]
    n = SCAN_ROWS
    n_seq = max(n // seq_rows, 1)
    sub = SCAN_CHUNKS if seq_rows % (SCAN_CHUNKS * n) == 0 else 1
    chunks = max(seq_rows // (sub * n), 1)
    levels = [min(seq_rows, n) >> l for l in range(int(np.log2(min(seq_rows, n))))]
    mat = _scan_matrix(min(seq_rows, n), levels, with_suffix=n_seq > 1)
    heads = min(SCAN_HEADS, max(1, SCAN_STATES // n_seq))
    hg = REC_HEADS // heads
    wb = heads * LANES
    sect = lambda k: pl.BlockSpec((D_MODEL, wb), lambda g, b, c: (0, k * hg + g))
    head_vec = lambda r: pl.BlockSpec((r, wb), lambda g, b, c: (0, g))
    state_block = (n_seq, heads, REC_DK, REC_DV)
    state = pl.BlockSpec(state_block, lambda g, b, c: (b, g, 0, 0))
    state_in = [] if s0 is None else [pl.BlockSpec((None,) + state_block, lambda g, b, c: (layer, b, g, 0, 0))]
    groups = n_state // n_seq
    cast = _SideCast(cast_items, hg * groups * chunks, lambda g, b, c: (g * groups + b) * chunks + c)
    a, state_out, *casted = pl.pallas_call(
        functools.partial(_scan_kernel, layer=layer, n_seq=n_seq, levels=levels, has_s0=s0 is not None,
                          n_cast=len(cast_items)),
        grid=(hg, groups, chunks),
        in_specs=[
            pl.BlockSpec((sub * n, D_MODEL), lambda g, b, c: (b * chunks + c, 0)),
            sect(0), sect(1), sect(2), sect(3),
            head_vec(lb_logits.shape[0]), head_vec(1),
            pl.BlockSpec(mat.shape, lambda g, b, c: (0, 0)),
        ] + state_in + cast.in_specs,
        out_specs=[pl.BlockSpec((sub * n, wb), lambda g, b, c: (b * chunks + c, g)), state] + cast.out_specs,
        out_shape=[
            jax.ShapeDtypeStruct((rows, D_MODEL), BF16),
            jax.ShapeDtypeStruct((n_state, REC_HEADS, REC_DK, REC_DV), F32),
        ] + cast.out_shape,
        scratch_shapes=[pltpu.VMEM((sub, n, wb), F32)],
        compiler_params=_compiler_params(("parallel", "parallel", "arbitrary"), 56),
        name="hgrn_scan",
    )(h, w_in, w_in, w_in, w_in, lb_logits, o_gain.reshape(1, D_MODEL), mat, *([] if s0 is None else [s0]),
      *cast.stacks)
    return a, state_out, casted


def _outproj_kernel(a_ref, x_ref, g_ref, w_ref, o_ref):
    y = jnp.dot(a_ref[...].astype(BF16), w_ref[...], preferred_element_type=F32)
    o_ref[...] = x_ref[...] + g_ref[0] * y


def _outproj(a, x, mod, w, tm):
    rows = x.shape[0]
    row = lambda i: (i, 0)
    return pl.pallas_call(
        _outproj_kernel,
        grid=(rows // tm,),
        in_specs=[
            pl.BlockSpec((tm, D_MODEL), row),
            pl.BlockSpec((tm, D_MODEL), row),
            mod.spec(GATE1, tm),
            pl.BlockSpec((D_MODEL, D_MODEL), lambda i: (0, 0)),
        ],
        out_specs=pl.BlockSpec((tm, D_MODEL), row),
        out_shape=jax.ShapeDtypeStruct((rows, D_MODEL), F32),
        compiler_params=_compiler_params(("parallel",), 48),
        name="outproj",
    )(a, x, mod.table, w)


def _mlp_kernel(x_ref, gain_ref, sc_ref, sh_ref, xo_ref, g_ref, wu_ref, wd_ref, o_ref, h_scr, u_scr, *, n_up, tf):
    j = pl.program_id(1)

    @pl.when(j == 0)
    def _():
        h_scr[...] = _modnorm(x_ref[...], gain_ref[...], sc_ref[0], sh_ref[0]).astype(BF16)

    @pl.when(j < n_up)
    def _():
        u = jnp.maximum(jnp.dot(h_scr[...], wu_ref[...], preferred_element_type=F32), 0.0)
        u2 = (u * u).astype(BF16)
        for c in range(n_up):
            @pl.when(j == c)
            def _():
                u_scr[:, c * tf:(c + 1) * tf] = u2

    @pl.when(j >= n_up)
    def _():
        d = jnp.dot(u_scr[...], wd_ref[...], preferred_element_type=F32)
        o_ref[...] = xo_ref[...] + g_ref[0] * d


MLP_UP_BLOCK = 2048
MLP_DOWN_BLOCK = 512


def _mlp(x, gain, mod, w_up, w_down, tm, tf=MLP_UP_BLOCK, tn=MLP_DOWN_BLOCK):
    rows = x.shape[0]
    n_up, n_down = D_FF // tf, D_MODEL // tn
    row = lambda i, j: (i, 0)
    out_col = lambda i, j: jnp.maximum(j - n_up, 0)
    return pl.pallas_call(
        functools.partial(_mlp_kernel, n_up=n_up, tf=tf),
        grid=(rows // tm, n_up + n_down),
        in_specs=[
            pl.BlockSpec((tm, D_MODEL), row),
            pl.BlockSpec((1, D_MODEL), lambda i, j: (0, 0)),
            mod.spec(SCALE2, tm), mod.spec(SHIFT2, tm),
            pl.BlockSpec((tm, tn), lambda i, j: (i, out_col(i, j))),
            mod.spec(GATE2, tm, tn, out_col),
            pl.BlockSpec((D_MODEL, tf), lambda i, j: (0, jnp.minimum(j, n_up - 1))),
            pl.BlockSpec((D_FF, tn), lambda i, j: (0, out_col(i, j))),
        ],
        out_specs=pl.BlockSpec((tm, tn), lambda i, j: (i, out_col(i, j))),
        out_shape=jax.ShapeDtypeStruct((rows, D_MODEL), F32),
        scratch_shapes=[pltpu.VMEM((tm, D_MODEL), BF16), pltpu.VMEM((tm, D_FF), BF16)],
        compiler_params=_compiler_params(("parallel", "arbitrary"), 60),
        name="mlp",
    )(x, gain, mod.table, mod.table, x, mod.table, w_up, w_down)


def _row_tile(rows):
    return min(rows, 512)


def _attn_prompt_layer(x, mod, gain_mix, w_qkv, q_gain, k_gain, sinks, w_o_f32, layer, batch, seq,
                       qkv_cast, attn_cast):
    tm = _row_tile(batch * seq)
    tables = _rope_tables(jnp.arange(seq, dtype=jnp.int32))
    q, k, v, qkv_casted = _qkv(x, gain_mix, mod, w_qkv, q_gain, k_gain, tables, seq, tm, BF16, qkv_cast)
    o, (w_o, *attn_casted) = _attn_prompt(q, k, v, sinks, batch, seq, [(w_o_f32, layer)] + list(attn_cast))
    window = lambda t: t.reshape(batch, seq, KV_WIDTH)[:, seq - WINDOW:].reshape(batch, WINDOW, N_KV_HEADS, HEAD_DIM)
    return _outproj(o, x, mod, w_o, tm), window(k), window(v), w_o, qkv_casted, attn_casted


def _attn_sample_layer(x, mod, gain_mix, w_qkv, q_gain, k_gain, sinks, w_o, layer, batch, seq, cache_k, cache_v):
    rows = batch * seq
    tm = _row_tile(rows)
    pos = PAST_LEN + jnp.arange(seq, dtype=jnp.int32)
    tables = tuple(jnp.tile(t, (batch, 1)) for t in _rope_tables(pos))
    q, k, v, _ = _qkv(x, gain_mix, mod, w_qkv, q_gain, k_gain, tables, rows, tm, F32)
    ck = cache_k.reshape(-1, batch, WINDOW, KV_WIDTH)
    cv = cache_v.reshape(-1, batch, WINDOW, KV_WIDTH)
    o, k_win, v_win = _attn_sample(q, k, v, ck, cv, layer, sinks, batch, seq)
    k_win = k_win.reshape(batch, WINDOW, N_KV_HEADS, HEAD_DIM)
    v_win = v_win.reshape(batch, WINDOW, N_KV_HEADS, HEAD_DIM)
    return _outproj(o, x, mod, w_o, tm), k_win, v_win


def _hgrn_layer(x, mod, gain_mix, w_in, lb_logits, o_gain, w_o, s0, layer, batch, seq, cast_items=()):
    tm = _row_tile(batch * seq)
    h = _norm_rows(x, gain_mix, mod, tm)
    a, state, casted = _hgrn_scan(h, w_in, lb_logits, o_gain, s0, layer, batch, seq, cast_items)
    if w_o is None:
        w_o, *casted = casted
    return _outproj(a, x, mod, w_o, tm), state, w_o, casted


def kernel(x_prompt, x_sample, cache_k_win, cache_v_win, state_hgrn, c_prompt, c_sample, norm_gain, w_ada, b_ada,
           attn_w_qkv, attn_q_gain, attn_k_gain, attn_sinks, attn_w_o, rec_w_in, rec_lb_logits, rec_o_gain,
           rec_w_o, mlp_w_up, mlp_w_down):
    bp, lp, _ = x_prompt.shape
    bs, ls, _ = x_sample.shape
    n_c = bp + bs
    c_all = jnp.concatenate([c_prompt, c_sample, jnp.zeros((-n_c % 16, D_MODEL), F32)], axis=0)
    mods = _ada_mods(c_all, w_ada, b_ada)

    xp = x_prompt.reshape(bp * lp, D_MODEL)
    xs = x_sample.reshape(bs * ls, D_MODEL)
    lb_logits = rec_lb_logits.astype(F32)
    table_p = mods[:, :bp].reshape(DEPTH, bp, 1, 6 * D_MODEL)
    table_s = jnp.repeat(mods[:, bp:n_c], ls, axis=1).reshape(DEPTH, 1, bs * ls, 6 * D_MODEL)
    w_qkv = attn_w_qkv[0].astype(BF16)
    kwp, vwp, kws, vws, sp, ss = [], [], [], [], [], []
    for i in range(DEPTH):
        j = i // N_MIXERS
        mod_p = _Mod(table_p, i, lp, per_row=False)
        mod_s = _Mod(table_s, i, ls, per_row=True)
        gain_mix = norm_gain[i, 0].reshape(1, D_MODEL)
        gain_mlp = norm_gain[i, 1].reshape(1, D_MODEL)
        mlp_cast = [(mlp_w_up, i), (mlp_w_down, i)]
        if i % N_MIXERS == 0:
            aw = (w_qkv, attn_q_gain[j], attn_k_gain[j], attn_sinks[j])
            xp, kp, vp, w_o, (w_in,), (w_up, w_down) = _attn_prompt_layer(
                xp, mod_p, gain_mix, *aw, attn_w_o, j, bp, lp, [(rec_w_in, j)], mlp_cast)
            xs, kn, vn = _attn_sample_layer(xs, mod_s, gain_mix, *aw, w_o, j, bs, ls, cache_k_win, cache_v_win)
            kwp.append(kp); vwp.append(vp); kws.append(kn); vws.append(vn)
        else:
            next_qkv = [(attn_w_qkv, j + 1)] if j + 1 < attn_w_qkv.shape[0] else []
            rw = (w_in, lb_logits, rec_o_gain[j])
            xp, s_p, w_o, (w_up, w_down, *w_next) = _hgrn_layer(
                xp, mod_p, gain_mix, *rw, None, None, j, bp, lp, [(rec_w_o, j)] + mlp_cast + next_qkv)
            xs, s_s, _, _ = _hgrn_layer(xs, mod_s, gain_mix, *rw, w_o, state_hgrn, j, bs, ls)
            sp.append(s_p); ss.append(s_s)
            if w_next:
                w_qkv = w_next[0]
        xp = _mlp(xp, gain_mlp, mod_p, w_up, w_down, _row_tile(bp * lp))
        xs = _mlp(xs, gain_mlp, mod_s, w_up, w_down, _row_tile(bs * ls))
    return (xp.reshape(bp, lp, D_MODEL), xs.reshape(bs, ls, D_MODEL),
            jnp.stack(kwp), jnp.stack(vwp), jnp.stack(kws), jnp.stack(vws), jnp.stack(sp), jnp.stack(ss))
```

```python
import functools

import numpy as np
import jax
import jax.numpy as jnp
from jax import lax
from jax.experimental import pallas as pl
from jax.experimental.pallas import tpu as pltpu

F32 = jnp.float32
BF16 = jnp.bfloat16

D_MODEL = 2048
DEPTH = 4
N_MIXERS = 2
PAST_LEN = 16384
HEAD_DIM = 64
N_Q_HEADS = D_MODEL // HEAD_DIM
N_KV_HEADS = N_Q_HEADS // 8
GQA_GROUP = N_Q_HEADS // N_KV_HEADS
KV_WIDTH = N_KV_HEADS * HEAD_DIM
WINDOW = 128
ROT_DIM = HEAD_DIM // 4
ROPE_THETA = 500000.0
REC_HEADS = 16
REC_DK = 128
REC_DV = D_MODEL // REC_HEADS
D_FF = 4 * D_MODEL
EPS = 1e-6

LANES = 128
BF16_SUBLANES = 16
MIB = 1024 * 1024
SCAN_ROWS = 128
SCAN_CHUNKS = 2
SCAN_HEADS = 8
SCAN_STATES = 64


def _compiler_params(semantics, vmem_mib):
    return pltpu.CompilerParams(dimension_semantics=semantics, vmem_limit_bytes=vmem_mib * MIB)


def _silu(x):
    half = 0.5 * x
    return half + half * jnp.tanh(half)


def _modnorm(x, gain, scale, shift):
    var = jnp.mean(x * x, axis=-1, keepdims=True)
    y = x * lax.rsqrt(var + EPS) * gain
    return y * (1.0 + scale) + shift


def _split_bf16(x, parts):
    out = []
    for _ in range(parts - 1):
        hi = x.astype(BF16)
        out.append(hi)
        x = x - hi.astype(F32)
    out.append(x.astype(BF16))
    return out


SHIFT1, SCALE1, GATE1, SHIFT2, SCALE2, GATE2 = range(6)


class _Mod:
    def __init__(self, table, layer, rows_per_group, per_row):
        self.table, self.layer, self.rows_per_group, self.per_row = table, layer, rows_per_group, per_row

    def spec(self, chunk, tm, tn=D_MODEL, col=lambda *_: 0):
        layer, first = self.layer, chunk * (D_MODEL // tn)
        if self.per_row:
            return pl.BlockSpec((None, 1, tm, tn), lambda i, *rest: (layer, 0, i, first + col(i, *rest)))
        assert self.rows_per_group % tm == 0
        per = self.rows_per_group // tm
        return pl.BlockSpec((None, 1, 1, tn), lambda i, *rest: (layer, i // per, 0, first + col(i, *rest)))


class _SideCast:
    def __init__(self, items, n_steps, step_of):
        self.stacks = [w for w, _ in items]
        self.in_specs, self.out_specs, self.out_shape = [], [], []
        for w, layer in items:
            _, rows, cols = w.shape
            assert rows % (n_steps * BF16_SUBLANES) == 0
            block = rows // n_steps
            self.in_specs.append(pl.BlockSpec((None, block, cols), lambda *g, layer=layer: (layer, step_of(*g), 0)))
            self.out_specs.append(pl.BlockSpec((block, cols), lambda *g: (step_of(*g), 0)))
            self.out_shape.append(jax.ShapeDtypeStruct((rows, cols), BF16))

    @staticmethod
    def run(src_refs, dst_refs):
        for src, dst in zip(src_refs, dst_refs):
            dst[...] = src[...].astype(BF16)


def _ada_kernel(c_ref, w_ref, b_ref, o_ref):
    s = _silu(c_ref[...]).astype(BF16)
    o_ref[0] = jnp.dot(s, w_ref[0].astype(BF16), preferred_element_type=F32) + b_ref[0]


def _ada_mods(c_all, w_ada, b_ada):
    rows = c_all.shape[0]
    tn = 1024
    n = 6 * D_MODEL
    return pl.pallas_call(
        _ada_kernel,
        grid=(DEPTH, n // tn),
        in_specs=[
            pl.BlockSpec((rows, D_MODEL), lambda l, j: (0, 0)),
            pl.BlockSpec((1, D_MODEL, tn), lambda l, j: (l, 0, j)),
            pl.BlockSpec((1, 1, tn), lambda l, j: (l, 0, j)),
        ],
        out_specs=pl.BlockSpec((1, rows, tn), lambda l, j: (l, 0, j)),
        out_shape=jax.ShapeDtypeStruct((DEPTH, rows, n), F32),
        compiler_params=_compiler_params(("parallel", "parallel"), 40),
        name="ada_mods",
    )(c_all, w_ada, b_ada.reshape(DEPTH, 1, n))


def _norm_rows_kernel(x_ref, gain_ref, sc_ref, sh_ref, o_ref):
    o_ref[...] = _modnorm(x_ref[...], gain_ref[...], sc_ref[0], sh_ref[0]).astype(o_ref.dtype)


def _norm_rows(x, gain, mod, tm):
    rows = x.shape[0]
    return pl.pallas_call(
        _norm_rows_kernel,
        grid=(rows // tm,),
        in_specs=[
            pl.BlockSpec((tm, D_MODEL), lambda i: (i, 0)),
            pl.BlockSpec((1, D_MODEL), lambda i: (0, 0)),
            mod.spec(SCALE1, tm),
            mod.spec(SHIFT1, tm),
        ],
        out_specs=pl.BlockSpec((tm, D_MODEL), lambda i: (i, 0)),
        out_shape=jax.ShapeDtypeStruct((rows, D_MODEL), BF16),
        compiler_params=_compiler_params(("parallel",), 32),
        name="norm_rows",
    )(x, gain, mod.table, mod.table)


def _qkv_kernel(x_ref, gain_ref, sc_ref, sh_ref, w_ref, qg_ref, kg_ref, cos_ref, s1_ref, s2_ref, *rest):
    n_cast = (len(rest) - 3) // 2
    q_ref, k_ref, v_ref = rest[n_cast:n_cast + 3]
    _SideCast.run(rest[:n_cast], rest[n_cast + 3:])
    h = _modnorm(x_ref[...], gain_ref[...], sc_ref[0], sh_ref[0]).astype(BF16)
    acc = jnp.dot(h, w_ref[...], preferred_element_type=F32)
    cos, s1, s2 = cos_ref[...], s1_ref[...], s2_ref[...]
    qk_width = D_MODEL + KV_WIDTH
    first_head = lax.broadcasted_iota(jnp.int32, (x_ref.shape[0], LANES), 1) < HEAD_DIM
    for c in range(qk_width // LANES):
        xc = acc[:, c * LANES:(c + 1) * LANES]
        x2 = xc * xc
        ssq = jnp.where(first_head,
                        jnp.sum(jnp.where(first_head, x2, 0.0), axis=-1, keepdims=True),
                        jnp.sum(jnp.where(first_head, 0.0, x2), axis=-1, keepdims=True))
        is_q = c * LANES < D_MODEL
        y = xc * lax.rsqrt(ssq * (1.0 / HEAD_DIM) + EPS) * (qg_ref[...] if is_q else kg_ref[...])
        r = y * cos + pltpu.roll(y, LANES - ROT_DIM // 2, 1) * s1 + pltpu.roll(y, ROT_DIM // 2, 1) * s2
        if is_q:
            q_ref[:, c * LANES:(c + 1) * LANES] = (r * (HEAD_DIM ** -0.5)).astype(q_ref.dtype)
        else:
            k_ref[:, c * LANES - D_MODEL:(c + 1) * LANES - D_MODEL] = r
    v_ref[...] = acc[:, qk_width:]


def _rope_tables(pos):
    half = ROT_DIM // 2
    inv = ROPE_THETA ** (-jnp.arange(half, dtype=F32) / half)
    ang = pos.astype(F32)[:, None] * inv[None, :]
    cos, sin = jnp.cos(ang), jnp.sin(ang)
    n = pos.shape[0]
    pad = jnp.zeros((n, HEAD_DIM - ROT_DIM), F32)
    zero = jnp.zeros((n, half), F32)
    c = jnp.concatenate([cos, cos, pad + 1.0], axis=1)
    s1 = jnp.concatenate([-sin, zero, pad], axis=1)
    s2 = jnp.concatenate([zero, sin, pad], axis=1)
    return tuple(jnp.tile(t, (1, LANES // HEAD_DIM)) for t in (c, s1, s2))


def _qkv(x, gain, mod, w, q_gain, k_gain, tables, table_rows, tm, q_dtype, cast_items=()):
    rows = x.shape[0]
    n = w.shape[-1]
    per = table_rows // tm
    tspec = pl.BlockSpec((tm, LANES), lambda i: (i % per, 0))
    vspec = pl.BlockSpec((1, LANES), lambda i: (0, 0))
    reps = LANES // HEAD_DIM
    cast = _SideCast(cast_items, rows // tm, lambda i: i)
    q, k, v, *casted = pl.pallas_call(
        _qkv_kernel,
        grid=(rows // tm,),
        in_specs=[
            pl.BlockSpec((tm, D_MODEL), lambda i: (i, 0)),
            pl.BlockSpec((1, D_MODEL), lambda i: (0, 0)),
            mod.spec(SCALE1, tm),
            mod.spec(SHIFT1, tm),
            pl.BlockSpec((D_MODEL, n), lambda i: (0, 0)),
            vspec, vspec, tspec, tspec, tspec,
        ] + cast.in_specs,
        out_specs=[
            pl.BlockSpec((tm, D_MODEL), lambda i: (i, 0)),
            pl.BlockSpec((tm, KV_WIDTH), lambda i: (i, 0)),
            pl.BlockSpec((tm, KV_WIDTH), lambda i: (i, 0)),
        ] + cast.out_specs,
        out_shape=[
            jax.ShapeDtypeStruct((rows, D_MODEL), q_dtype),
            jax.ShapeDtypeStruct((rows, KV_WIDTH), F32),
            jax.ShapeDtypeStruct((rows, KV_WIDTH), F32),
        ] + cast.out_shape,
        compiler_params=_compiler_params(("parallel",), 56),
        name="qkv",
    )(x, gain, mod.table, mod.table, w,
      jnp.tile(q_gain.reshape(1, HEAD_DIM), (1, reps)), jnp.tile(k_gain.reshape(1, HEAD_DIM), (1, reps)),
      *tables, *cast.stacks)
    return q, k, v, casted


def _softmax_with_sink(s, sink):
    m = jnp.maximum(jnp.max(s, axis=-1, keepdims=True), sink)
    p = jnp.exp(s - m)
    denom = jnp.sum(p, axis=-1, keepdims=True) + jnp.exp(sink - m)
    return p, denom


def _attn_prompt_kernel(sink_ref, q_ref, kp_ref, kc_ref, vp_ref, vc_ref, *rest):
    n_cast = len(rest) // 2
    o_ref = rest[n_cast]
    _SideCast.run(rest[:n_cast], rest[n_cast + 1:])
    j = pl.program_id(1)
    kk = jnp.concatenate([kp_ref[...], kc_ref[...]], axis=0).astype(BF16)
    vt = jnp.concatenate([vp_ref[...], vc_ref[...]], axis=0).T.astype(BF16)
    s = lax.broadcasted_iota(jnp.int32, (2 * WINDOW, WINDOW), 0)
    t = lax.broadcasted_iota(jnp.int32, (2 * WINDOW, WINDOW), 1)
    ok = (s > t) & (s <= t + WINDOW) & (s >= jnp.where(j > 0, 0, WINDOW))
    bias = jnp.where(ok, 0.0, -jnp.inf)
    zk = jnp.zeros((2 * WINDOW, HEAD_DIM), BF16)
    zv = jnp.zeros((HEAD_DIM, 2 * WINDOW), BF16)
    for g in range(N_KV_HEADS):
        kg = kk[:, g * HEAD_DIM:(g + 1) * HEAD_DIM]
        vg = vt[g * HEAD_DIM:(g + 1) * HEAD_DIM]
        k2 = jnp.concatenate([jnp.concatenate([kg, zk], axis=1), jnp.concatenate([zk, kg], axis=1)], axis=0)
        v2 = jnp.concatenate([jnp.concatenate([vg, zv], axis=1), jnp.concatenate([zv, vg], axis=1)], axis=0)
        for hp in range(GQA_GROUP // 2):
            h0 = g * GQA_GROUP + 2 * hp
            lanes = slice(h0 * HEAD_DIM, (h0 + 2) * HEAD_DIM)
            st = lax.dot_general(k2, q_ref[:, lanes], (((1,), (1,)), ((), ())), preferred_element_type=F32)
            probs = []
            for i in range(2):
                sh = st[i * 2 * WINDOW:(i + 1) * 2 * WINDOW] + bias
                sink = sink_ref[h0 + i]
                m = jnp.maximum(jnp.max(sh, axis=0, keepdims=True), sink)
                p = jnp.exp(sh - m)
                denom = jnp.sum(p, axis=0, keepdims=True) + jnp.exp(sink - m)
                probs.append((p * (1.0 / denom)).astype(BF16))
            ot = jnp.dot(v2, jnp.concatenate(probs, axis=0), preferred_element_type=F32)
            o_ref[:, lanes] = ot.T.astype(o_ref.dtype)


def _attn_prompt(q, k, v, sinks, batch, seq, cast_items=()):
    nb = seq // WINDOW
    rows = batch * seq
    cur = lambda b, j: (b * nb + j, 0)
    prev = lambda b, j: (b * nb + jnp.maximum(j - 1, 0), 0)
    kv_block = (WINDOW, KV_WIDTH)
    cast = _SideCast(cast_items, batch * nb, lambda b, j: b * nb + j)
    o, *casted = pl.pallas_call(
        _attn_prompt_kernel,
        grid=(batch, nb),
        in_specs=[
            pl.BlockSpec(memory_space=pltpu.SMEM),
            pl.BlockSpec((WINDOW, D_MODEL), cur),
            pl.BlockSpec(kv_block, prev), pl.BlockSpec(kv_block, cur),
            pl.BlockSpec(kv_block, prev), pl.BlockSpec(kv_block, cur),
        ] + cast.in_specs,
        out_specs=[pl.BlockSpec((WINDOW, D_MODEL), cur)] + cast.out_specs,
        out_shape=[jax.ShapeDtypeStruct((rows, D_MODEL), BF16)] + cast.out_shape,
        compiler_params=_compiler_params(("parallel", "parallel"), 40),
        name="attn_prompt",
    )(sinks, q, k, k, v, v, *cast.stacks)
    return o, casted


SAMPLE_SEQS = 8


def _attn_sample_kernel(sink_ref, q_ref, k_ref, v_ref, ck_ref, cv_ref, o_ref, nk_ref, nv_ref, *, steps):
    rows = GQA_GROUP * steps
    assert steps & (steps - 1) == 0
    t = lax.broadcasted_iota(jnp.int32, (rows, 2 * WINDOW), 0) & (steps - 1)
    s = lax.broadcasted_iota(jnp.int32, (rows, 2 * WINDOW), 1)
    ok = ((s < WINDOW) & (s > t)) | ((s >= WINDOW) & (s - WINDOW <= t))
    zpad = jnp.zeros((WINDOW - steps, KV_WIDTH), F32)
    for b in range(ck_ref.shape[0]):
        new = slice(b * steps, (b + 1) * steps)
        kc, vc = ck_ref[b], cv_ref[b]
        kn, vn = k_ref[new], v_ref[new]
        nk_ref[b, :WINDOW - steps] = kc[steps:]
        nk_ref[b, WINDOW - steps:] = kn
        nv_ref[b, :WINDOW - steps] = vc[steps:]
        nv_ref[b, WINDOW - steps:] = vn
        kk = jnp.concatenate([kc, kn, zpad], axis=0).astype(BF16)
        vv = jnp.concatenate([vc, vn, zpad], axis=0).astype(BF16)
        q = q_ref[new]
        for g in range(N_KV_HEADS):
            heads = range(g * GQA_GROUP, (g + 1) * GQA_GROUP)
            qg = jnp.concatenate([q[:, h * HEAD_DIM:(h + 1) * HEAD_DIM] for h in heads], axis=0).astype(BF16)
            sc = lax.dot_general(qg, kk[:, g * HEAD_DIM:(g + 1) * HEAD_DIM], (((1,), (1,)), ((), ())),
                                 preferred_element_type=F32)
            p, denom = _softmax_with_sink(jnp.where(ok, sc, -jnp.inf), sink_ref[g])
            og = jnp.dot(p.astype(BF16), vv[:, g * HEAD_DIM:(g + 1) * HEAD_DIM],
                         preferred_element_type=F32) / denom
            for i, h in enumerate(heads):
                o_ref[new, h * HEAD_DIM:(h + 1) * HEAD_DIM] = og[i * steps:(i + 1) * steps]


def _attn_sample(q, k, v, cache_k, cache_v, layer, sinks, batch, steps):
    rows = batch * steps
    sink_rows = jnp.repeat(sinks.reshape(N_KV_HEADS, GQA_GROUP), steps, axis=1)[..., None]
    seqs = min(SAMPLE_SEQS, batch)
    assert batch % seqs == 0
    steps_blk = seqs * steps
    row = lambda b: (b, 0)
    win = lambda b: (b, 0, 0)
    cache_block = (seqs, WINDOW, KV_WIDTH)
    return pl.pallas_call(
        functools.partial(_attn_sample_kernel, steps=steps),
        grid=(batch // seqs,),
        in_specs=[
            pl.BlockSpec((N_KV_HEADS, GQA_GROUP * steps, 1), lambda b: (0, 0, 0)),
            pl.BlockSpec((steps_blk, D_MODEL), row),
            pl.BlockSpec((steps_blk, KV_WIDTH), row), pl.BlockSpec((steps_blk, KV_WIDTH), row),
            pl.BlockSpec((None,) + cache_block, lambda b: (layer, b, 0, 0)),
            pl.BlockSpec((None,) + cache_block, lambda b: (layer, b, 0, 0)),
        ],
        out_specs=[
            pl.BlockSpec((steps_blk, D_MODEL), row),
            pl.BlockSpec(cache_block, win), pl.BlockSpec(cache_block, win),
        ],
        out_shape=[
            jax.ShapeDtypeStruct((rows, D_MODEL), F32),
            jax.ShapeDtypeStruct((batch, WINDOW, KV_WIDTH), F32),
            jax.ShapeDtypeStruct((batch, WINDOW, KV_WIDTH), F32),
        ],
        compiler_params=_compiler_params(("parallel",), 32),
        name="attn_sample",
    )(sink_rows, q, k, v, cache_k, cache_v)


SUBLANES = 8


def _levels_from_matrix(levels):
    return [b for b in levels if 2 < b <= SUBLANES]


def _scan_matrix(seq_rows, levels, with_suffix):
    n = SCAN_ROWS
    t = np.arange(n)[:, None]
    r = np.arange(n)[None, :]
    same = (t // seq_rows) == (r // seq_rows)
    mats = [same & (r <= t)] + ([same & (r > t)] if with_suffix else [])
    for b in _levels_from_matrix(levels):
        mid = (t // b) * b + b // 2
        upper = (t & (b // 2)) != 0
        mats.append(np.where(upper, (r >= mid) & (r <= t), (r > t) & (r < mid)))
    stacked = np.concatenate(mats, axis=0)
    return jnp.asarray(np.concatenate([stacked, stacked], axis=1), dtype=BF16)


def _scan_kernel(h_ref, wq_ref, wf_ref, wi_ref, wg_ref, lbl_ref, og_ref, mat_ref, *rest,
                 layer, n_seq, levels, has_s0, n_cast):
    rest = list(rest)
    s0_ref = rest.pop(0) if has_s0 else None
    cast_src, (a_ref, st_ref), cast_dst, g_scr = rest[:n_cast], rest[n_cast:n_cast + 2], rest[n_cast + 2:-1], rest[-1]
    _SideCast.run(cast_src, cast_dst)
    n = SCAN_ROWS
    heads = st_ref.shape[1]
    width = heads * LANES
    seq_rows = n // n_seq
    chunk = pl.program_id(2)

    @pl.when(chunk == 0)
    def _():
        for s in range(n_seq):
            for h in range(heads):
                st_ref[s, h] = jnp.zeros((REC_DV, REC_DK), F32) if s0_ref is None else s0_ref[s, h].T

    logits = lbl_ref[...]
    pexp = jnp.exp(logits - jnp.max(logits, axis=0, keepdims=True))
    psum = jnp.sum(pexp, axis=0, keepdims=True)
    lb = jnp.zeros_like(psum)
    for i in range(1, layer + 1):
        lb = lb + pexp[i:i + 1] / psum

    hb = h_ref[...]
    project = lambda w_ref: jnp.dot(hb, w_ref[...], preferred_element_type=F32)
    qs_all = _silu(project(wq_ref))
    fz = project(wf_ref)
    e = jnp.exp(-jnp.abs(fz))
    r = 1.0 / (1.0 + e)
    pos = fz >= 0
    f_all = lb + (1.0 - lb) * jnp.where(pos, r, e * r)
    logf_all = jnp.log(f_all)
    kin_all = (1.0 - lb) * jnp.where(pos, e * r, r)
    val_all = project(wi_ref)
    gate_all = _silu(project(wg_ref))
    mat = mat_ref[...]
    t_row = lax.broadcasted_iota(jnp.int32, (n, width), 0)
    t_lane = lax.broadcasted_iota(jnp.int32, (n, LANES), 0)
    tt = lax.broadcasted_iota(jnp.int32, (n, n), 0)
    ss = lax.broadcasted_iota(jnp.int32, (n, n), 1)
    same_block = {b: (tt ^ ss) < b for b in levels if b < n}
    for u in range(g_scr.shape[0]):
        rows = slice(u * n, (u + 1) * n)
        _scan_chunk(qs_all[rows], f_all[rows], logf_all[rows], kin_all[rows], val_all[rows], gate_all[rows], mat,
                    og_ref, g_scr.at[u], st_ref, a_ref.at[rows], t_row, t_lane, same_block, n_seq, levels)

    @pl.when(chunk == pl.num_programs(2) - 1)
    def _():
        for s in range(n_seq):
            for h in range(heads):
                st_ref[s, h] = st_ref[s, h].T


def _scan_chunk(qs, f, logf, kin, val, gate, mat, og_ref, g_scr, st_ref, a_ref, t_row, t_lane, same_block,
                n_seq, levels):
    n = SCAN_ROWS
    heads = st_ref.shape[1]
    width = heads * LANES
    seq_rows = n // n_seq
    sums = jnp.dot(mat, jnp.concatenate(_split_bf16(logf, 2), axis=0), preferred_element_type=F32)
    gcum = sums[:n]
    g_scr[...] = gcum
    if n_seq == 1:
        gtail = g_scr[n - 1:n, :] - gcum
        block = 1
    else:
        gtail = sums[n:2 * n]
        block = 2
    level_sums = {b: sums[(block + i) * n:(block + i + 1) * n] for i, b in enumerate(_levels_from_matrix(levels))}

    q_lv, k_lv = [], []
    for b in levels:
        half = b // 2
        if b > SUBLANES:
            zero = jnp.zeros((half, width), F32)
            qparts, kparts = [], []
            for lo in range(0, n, b):
                mid, hi = lo + half, lo + b
                ref = g_scr[mid - 1:mid, :]
                kparts += [kin[lo:mid] * jnp.exp(ref - gcum[lo:mid]), zero]
                qparts += [zero, qs[mid:hi] * jnp.exp(gcum[mid:hi] - ref)]
            qt, kt = jnp.concatenate(qparts, axis=0), jnp.concatenate(kparts, axis=0)
        else:
            upper = (t_row & half) != 0
            if b == 2:
                qt, kt = jnp.where(upper, qs * f, 0.0), jnp.where(upper, 0.0, kin)
            else:
                decay = jnp.exp(level_sums[b])
                qt, kt = jnp.where(upper, qs * decay, 0.0), jnp.where(upper, 0.0, kin * decay)
        q_lv.append(qt.astype(BF16))
        k_lv.append(kt.astype(BF16))
    qe_all = qs * jnp.exp(gcum)
    kt_all = kin * jnp.exp(gtail)
    contract_lanes = (((1,), (1,)), ((), ()))
    contract_rows = (((0,), (0,)), ((), ()))

    for h in range(heads):
        sl = slice(h * LANES, (h + 1) * LANES)
        vh = val[:, sl]
        vb = vh.astype(BF16)
        a = None
        for l, b in enumerate(levels):
            al = lax.dot_general(q_lv[l][:, sl], k_lv[l][:, sl], contract_lanes, preferred_element_type=F32)
            if b == n:
                a = al
            else:
                a = jnp.where(same_block[b], al, 0.0 if a is None else a)
        qe, kt = qe_all[:, sl], kt_all[:, sl]
        diag = jnp.sum(qs[:, sl] * kin[:, sl], axis=-1, keepdims=True) * vh
        if n_seq == 1:
            st = st_ref[0, h]
            lhs = jnp.concatenate([a.astype(BF16), qe.astype(BF16)], axis=1)
            rhs = jnp.concatenate([vb, st.T.astype(BF16)], axis=0)
            o = jnp.dot(lhs, rhs, preferred_element_type=F32) + diag
            st_ref[0, h] = st * jnp.exp(g_scr[n - 1:n, sl]) + lax.dot_general(
                vb, kt.astype(BF16), contract_rows, preferred_element_type=F32)
        else:
            o = jnp.dot(a.astype(BF16), vb, preferred_element_type=F32) + diag
        for s in range(n_seq if n_seq > 1 else 0):
            last = (s + 1) * seq_rows - 1
            if n_seq == 1:
                qe_s, kt_s = qe, kt
            else:
                inside = (t_lane >= s * seq_rows) & (t_lane <= last)
                qe_s, kt_s = jnp.where(inside, qe, 0.0), jnp.where(inside, kt, 0.0)
            st = st_ref[s, h]
            o = o + lax.dot_general(qe_s.astype(BF16), st.astype(BF16), contract_lanes,
                                    preferred_element_type=F32)
            st_ref[s, h] = st * jnp.exp(g_scr[last:last + 1, sl]) + lax.dot_general(
                vb, kt_s.astype(BF16), contract_rows, preferred_element_type=F32)
        rs = lax.rsqrt(jnp.mean(o * o, axis=-1, keepdims=True) + EPS)
        a_ref[:, sl] = (o * rs * og_ref[:, sl] * gate[:, sl]).astype(a_ref.dtype)


def _hgrn_scan(h, w_in, lb_logits, o_gain, s0, layer, n_state, seq_rows, cast_items=()):
    rows = h.shape[0]
    n = SCAN_ROWS
    n_seq = max(n // seq_rows, 1)
    sub = SCAN_CHUNKS if seq_rows % (SCAN_CHUNKS * n) == 0 else 1
    chunks = max(seq_rows // (sub * n), 1)
    levels = [min(seq_rows, n) >> l for l in range(int(np.log2(min(seq_rows, n))))]
    mat = _scan_matrix(min(seq_rows, n), levels, with_suffix=n_seq > 1)
    heads = min(SCAN_HEADS, max(1, SCAN_STATES // n_seq))
    hg = REC_HEADS // heads
    wb = heads * LANES
    sect = lambda k: pl.BlockSpec((D_MODEL, wb), lambda g, b, c: (0, k * hg + g))
    head_vec = lambda r: pl.BlockSpec((r, wb), lambda g, b, c: (0, g))
    state_block = (n_seq, heads, REC_DK, REC_DV)
    state = pl.BlockSpec(state_block, lambda g, b, c: (b, g, 0, 0))
    state_in = [] if s0 is None else [pl.BlockSpec((None,) + state_block, lambda g, b, c: (layer, b, g, 0, 0))]
    groups = n_state // n_seq
    cast = _SideCast(cast_items, hg * groups * chunks, lambda g, b, c: (g * groups + b) * chunks + c)
    a, state_out, *casted = pl.pallas_call(
        functools.partial(_scan_kernel, layer=layer, n_seq=n_seq, levels=levels, has_s0=s0 is not None,
                          n_cast=len(cast_items)),
        grid=(hg, groups, chunks),
        in_specs=[
            pl.BlockSpec((sub * n, D_MODEL), lambda g, b, c: (b * chunks + c, 0)),
            sect(0), sect(1), sect(2), sect(3),
            head_vec(lb_logits.shape[0]), head_vec(1),
            pl.BlockSpec(mat.shape, lambda g, b, c: (0, 0)),
        ] + state_in + cast.in_specs,
        out_specs=[pl.BlockSpec((sub * n, wb), lambda g, b, c: (b * chunks + c, g)), state] + cast.out_specs,
        out_shape=[
            jax.ShapeDtypeStruct((rows, D_MODEL), BF16),
            jax.ShapeDtypeStruct((n_state, REC_HEADS, REC_DK, REC_DV), F32),
        ] + cast.out_shape,
        scratch_shapes=[pltpu.VMEM((sub, n, wb), F32)],
        compiler_params=_compiler_params(("parallel", "parallel", "arbitrary"), 56),
        name="hgrn_scan",
    )(h, w_in, w_in, w_in, w_in, lb_logits, o_gain.reshape(1, D_MODEL), mat, *([] if s0 is None else [s0]),
      *cast.stacks)
    return a, state_out, casted


def _outproj_kernel(a_ref, x_ref, g_ref, w_ref, o_ref):
    y = jnp.dot(a_ref[...].astype(BF16), w_ref[...], preferred_element_type=F32)
    o_ref[...] = x_ref[...] + g_ref[0] * y


def _outproj(a, x, mod, w, tm):
    rows = x.shape[0]
    tm = min(rows, 2 * tm)
    row = lambda i: (i, 0)
    return pl.pallas_call(
        _outproj_kernel,
        grid=(rows // tm,),
        in_specs=[
            pl.BlockSpec((tm, D_MODEL), row),
            pl.BlockSpec((tm, D_MODEL), row),
            mod.spec(GATE1, tm),
            pl.BlockSpec((D_MODEL, D_MODEL), lambda i: (0, 0), pipeline_mode=pl.Buffered(1)),
        ],
        out_specs=pl.BlockSpec((tm, D_MODEL), row),
        out_shape=jax.ShapeDtypeStruct((rows, D_MODEL), F32),
        compiler_params=_compiler_params(("parallel",), 56),
        name="outproj",
    )(a, x, mod.table, w)


def _mlp_kernel(x_ref, gain_ref, sc_ref, sh_ref, xo_ref, g_ref, wu_ref, wd_ref, o_ref, h_scr, u_scr, *, n_up, tf):
    j = pl.program_id(1)

    @pl.when(j == 0)
    def _():
        h_scr[...] = _modnorm(x_ref[...], gain_ref[...], sc_ref[0], sh_ref[0]).astype(BF16)

    @pl.when(j < n_up)
    def _():
        u = jnp.maximum(jnp.dot(h_scr[...], wu_ref[...], preferred_element_type=F32), 0.0)
        u2 = (u * u).astype(BF16)
        for c in range(n_up):
            @pl.when(j == c)
            def _():
                u_scr[:, c * tf:(c + 1) * tf] = u2

    @pl.when(j >= n_up)
    def _():
        d = jnp.dot(u_scr[...], wd_ref[...], preferred_element_type=F32)
        o_ref[...] = xo_ref[...] + g_ref[0] * d


MLP_UP_BLOCK = 2048
MLP_DOWN_BLOCK = 512


def _mlp(x, gain, mod, w_up, w_down, tm, tf=MLP_UP_BLOCK, tn=MLP_DOWN_BLOCK):
    rows = x.shape[0]
    n_up, n_down = D_FF // tf, D_MODEL // tn
    row = lambda i, j: (i, 0)
    out_col = lambda i, j: jnp.maximum(j - n_up, 0)
    return pl.pallas_call(
        functools.partial(_mlp_kernel, n_up=n_up, tf=tf),
        grid=(rows // tm, n_up + n_down),
        in_specs=[
            pl.BlockSpec((tm, D_MODEL), row),
            pl.BlockSpec((1, D_MODEL), lambda i, j: (0, 0)),
            mod.spec(SCALE2, tm), mod.spec(SHIFT2, tm),
            pl.BlockSpec((tm, tn), lambda i, j: (i, out_col(i, j))),
            mod.spec(GATE2, tm, tn, out_col),
            pl.BlockSpec((D_MODEL, tf), lambda i, j: (0, jnp.minimum(j, n_up - 1))),
            pl.BlockSpec((D_FF, tn), lambda i, j: (0, out_col(i, j))),
        ],
        out_specs=pl.BlockSpec((tm, tn), lambda i, j: (i, out_col(i, j))),
        out_shape=jax.ShapeDtypeStruct((rows, D_MODEL), F32),
        scratch_shapes=[pltpu.VMEM((tm, D_MODEL), BF16), pltpu.VMEM((tm, D_FF), BF16)],
        compiler_params=_compiler_params(("parallel", "arbitrary"), 60),
        name="mlp",
    )(x, gain, mod.table, mod.table, x, mod.table, w_up, w_down)


def _row_tile(rows):
    return min(rows, 512)


def _attn_prompt_layer(x, mod, gain_mix, w_qkv, q_gain, k_gain, sinks, w_o_f32, layer, batch, seq,
                       qkv_cast, attn_cast):
    tm = _row_tile(batch * seq)
    tables = _rope_tables(jnp.arange(seq, dtype=jnp.int32))
    q, k, v, qkv_casted = _qkv(x, gain_mix, mod, w_qkv, q_gain, k_gain, tables, seq, tm, BF16, qkv_cast)
    o, (w_o, *attn_casted) = _attn_prompt(q, k, v, sinks, batch, seq, [(w_o_f32, layer)] + list(attn_cast))
    window = lambda t: t.reshape(batch, seq, KV_WIDTH)[:, seq - WINDOW:].reshape(batch, WINDOW, N_KV_HEADS, HEAD_DIM)
    return _outproj(o, x, mod, w_o, tm), window(k), window(v), w_o, qkv_casted, attn_casted


def _attn_sample_layer(x, mod, gain_mix, w_qkv, q_gain, k_gain, sinks, w_o, layer, batch, seq, cache_k, cache_v):
    rows = batch * seq
    tm = _row_tile(rows)
    pos = PAST_LEN + jnp.arange(seq, dtype=jnp.int32)
    tables = tuple(jnp.tile(t, (batch, 1)) for t in _rope_tables(pos))
    q, k, v, _ = _qkv(x, gain_mix, mod, w_qkv, q_gain, k_gain, tables, rows, tm, F32)
    ck = cache_k.reshape(-1, batch, WINDOW, KV_WIDTH)
    cv = cache_v.reshape(-1, batch, WINDOW, KV_WIDTH)
    o, k_win, v_win = _attn_sample(q, k, v, ck, cv, layer, sinks, batch, seq)
    k_win = k_win.reshape(batch, WINDOW, N_KV_HEADS, HEAD_DIM)
    v_win = v_win.reshape(batch, WINDOW, N_KV_HEADS, HEAD_DIM)
    return _outproj(o, x, mod, w_o, tm), k_win, v_win


def _hgrn_layer(x, mod, gain_mix, w_in, lb_logits, o_gain, w_o, s0, layer, batch, seq, cast_items=()):
    tm = _row_tile(batch * seq)
    h = _norm_rows(x, gain_mix, mod, tm)
    a, state, casted = _hgrn_scan(h, w_in, lb_logits, o_gain, s0, layer, batch, seq, cast_items)
    if w_o is None:
        w_o, *casted = casted
    return _outproj(a, x, mod, w_o, tm), state, w_o, casted


def kernel(x_prompt, x_sample, cache_k_win, cache_v_win, state_hgrn, c_prompt, c_sample, norm_gain, w_ada, b_ada,
           attn_w_qkv, attn_q_gain, attn_k_gain, attn_sinks, attn_w_o, rec_w_in, rec_lb_logits, rec_o_gain,
           rec_w_o, mlp_w_up, mlp_w_down):
    bp, lp, _ = x_prompt.shape
    bs, ls, _ = x_sample.shape
    n_c = bp + bs
    c_all = jnp.concatenate([c_prompt, c_sample, jnp.zeros((-n_c % 16, D_MODEL), F32)], axis=0)
    mods = _ada_mods(c_all, w_ada, b_ada)

    xp = x_prompt.reshape(bp * lp, D_MODEL)
    xs = x_sample.reshape(bs * ls, D_MODEL)
    lb_logits = rec_lb_logits.astype(F32)
    table_p = mods[:, :bp].reshape(DEPTH, bp, 1, 6 * D_MODEL)
    table_s = jnp.repeat(mods[:, bp:n_c], ls, axis=1).reshape(DEPTH, 1, bs * ls, 6 * D_MODEL)
    w_qkv = attn_w_qkv[0].astype(BF16)
    kwp, vwp, kws, vws, sp, ss = [], [], [], [], [], []
    for i in range(DEPTH):
        j = i // N_MIXERS
        mod_p = _Mod(table_p, i, lp, per_row=False)
        mod_s = _Mod(table_s, i, ls, per_row=True)
        gain_mix = norm_gain[i, 0].reshape(1, D_MODEL)
        gain_mlp = norm_gain[i, 1].reshape(1, D_MODEL)
        mlp_cast = [(mlp_w_up, i), (mlp_w_down, i)]
        if i % N_MIXERS == 0:
            aw = (w_qkv, attn_q_gain[j], attn_k_gain[j], attn_sinks[j])
            xp, kp, vp, w_o, (w_in,), (w_up, w_down) = _attn_prompt_layer(
                xp, mod_p, gain_mix, *aw, attn_w_o, j, bp, lp, [(rec_w_in, j)], mlp_cast)
            xs, kn, vn = _attn_sample_layer(xs, mod_s, gain_mix, *aw, w_o, j, bs, ls, cache_k_win, cache_v_win)
            kwp.append(kp); vwp.append(vp); kws.append(kn); vws.append(vn)
        else:
            next_qkv = [(attn_w_qkv, j + 1)] if j + 1 < attn_w_qkv.shape[0] else []
            rw = (w_in, lb_logits, rec_o_gain[j])
            xp, s_p, w_o, (w_up, w_down, *w_next) = _hgrn_layer(
                xp, mod_p, gain_mix, *rw, None, None, j, bp, lp, [(rec_w_o, j)] + mlp_cast + next_qkv)
            xs, s_s, _, _ = _hgrn_layer(xs, mod_s, gain_mix, *rw, w_o, state_hgrn, j, bs, ls)
            sp.append(s_p); ss.append(s_s)
            if w_next:
                w_qkv = w_next[0]
        xp = _mlp(xp, gain_mlp, mod_p, w_up, w_down, _row_tile(bp * lp))
        xs = _mlp(xs, gain_mlp, mod_s, w_up, w_down, _row_tile(bs * ls))
    return (xp.reshape(bp, lp, D_MODEL), xs.reshape(bs, ls, D_MODEL),
            jnp.stack(kwp), jnp.stack(vwp), jnp.stack(kws), jnp.stack(vws), jnp.stack(sp), jnp.stack(ss))
```
